```python
import jax, jax.numpy as jnp
from jax import lax
import numpy as np

D_MODEL = 1024
BATCH = 16
SEQ = 256
DEPTH = 4
DEC_BATCH = 8
DEC_SEQ = 2048
PAST_LEN = 256

GRID_W = 64
N_EVEN = (DEPTH + 1) // 2
N_ODD = DEPTH // 2
A_HEADS = 8
A_KV_HEADS = 2
A_GROUP = A_HEADS // A_KV_HEADS
HEAD_DIM = 64
WINDOW = 128
BLOCK = 128
B_HEADS = 8
Q_LORA = 192
KV_LORA = 128
QK_NOPE = 64
QK_ROPE = 32
V_DIM = 64
MLA_SCALE = (QK_NOPE + QK_ROPE) ** -0.5
A_Q_W = A_HEADS * HEAD_DIM
A_KV_W = A_KV_HEADS * HEAD_DIM
ATTN_IN_SIZES = (A_Q_W, A_KV_W, A_KV_W, Q_LORA, KV_LORA, QK_ROPE)
ATTN_IN_W = sum(ATTN_IN_SIZES)
ATTN_OUT_W = A_HEADS * HEAD_DIM + B_HEADS * V_DIM
CONV_CH = D_MODEL // 2
CONV_WIDTH = 31
POOL_CH = D_MODEL // 2
POOL_SIZES = (2, 4, 8, 16)
N_POOL_GROUPS = len(POOL_SIZES)
POOL_GROUP_W = POOL_CH // N_POOL_GROUPS
CONV_IN_SIZES = (CONV_CH, CONV_CH, POOL_CH)
CONV_IN_W = sum(CONV_IN_SIZES)
CONV_OUT_W = CONV_CH + POOL_CH
D_FF = 4 * D_MODEL
ROPE_BASE = 10000.0
EPS = 1e-6
NEG_INF = -1e30

kernel_name = "hybrid_diffusion_prefix_trunk_step"


def _split(x, sizes):
    offs = [int(v) for v in np.cumsum(sizes)[:-1]]
    return jnp.split(x, offs, axis=-1)


def rms_norm(x, g):
    xf = x.astype(jnp.float32)
    y = xf * lax.rsqrt(jnp.mean(xf * xf, axis=-1, keepdims=True) + EPS)
    return (y * g.astype(jnp.float32)).astype(x.dtype)


def layer_norm(x, g, b):
    xf = x.astype(jnp.float32)
    mu = jnp.mean(xf, axis=-1, keepdims=True)
    var = jnp.mean(jnp.square(xf - mu), axis=-1, keepdims=True)
    y = (xf - mu) * lax.rsqrt(var + EPS)
    return (y * g.astype(jnp.float32) + b.astype(jnp.float32)).astype(x.dtype)


def adaln(cond, w, b):
    m = jax.nn.silu(cond) @ w + b
    return [t[:, None, :] for t in jnp.split(m, 6, axis=-1)]


def modulate(h, shift, scale):
    return h * (1.0 + scale) + shift


def axial_rope(n, dim):
    rows = n // GRID_W
    row = jnp.repeat(jnp.arange(rows), GRID_W).astype(jnp.float32)
    col = jnp.tile(jnp.arange(GRID_W), rows).astype(jnp.float32)
    quarter = dim // 4
    inv_freq = ROPE_BASE ** (-jnp.arange(quarter, dtype=jnp.float32) / quarter)
    ang = jnp.concatenate([row[:, None] * inv_freq, col[:, None] * inv_freq], axis=-1)
    return jnp.cos(ang), jnp.sin(ang)


def apply_rope(x, cos, sin):
    half = x.shape[-1] // 2
    xf = x.astype(jnp.float32)
    x1, x2 = xf[..., :half], xf[..., half:]
    cs, sn = cos[:, None, :], sin[:, None, :]
    return jnp.concatenate([x1 * cs - x2 * sn, x1 * sn + x2 * cs], axis=-1).astype(x.dtype)


def _softmax_with_sink(s, sink):
    if sink is None:
        return jax.nn.softmax(s, axis=-1)
    col = jnp.broadcast_to(sink.astype(jnp.float32), s.shape[:-1] + (1,))
    return jax.nn.softmax(jnp.concatenate([s, col], axis=-1), axis=-1)[..., :-1]


def dense_attention(q, k, v, scale, sink=None):
    b, n, kh, g, dq = q.shape
    nb = n // BLOCK
    qb = jnp.moveaxis(q.reshape(b, nb, BLOCK, kh, g, dq), 1, 0)
    sink_b = None if sink is None else sink[None, :, :, None, None]

    def one(qi):
        s = jnp.einsum('bqhgd,bkhd->bhgqk', qi, k, preferred_element_type=jnp.float32) * scale
        p = _softmax_with_sink(s, sink_b)
        return jnp.einsum('bhgqk,bkhd->bqhgd', p.astype(v.dtype), v)

    o = lax.map(one, qb)
    return jnp.moveaxis(o, 0, 1).reshape(b, n, kh, g, v.shape[-1])


def window_attention(q, k, v, k_ctx, v_ctx, sink):
    b, n, kh, g, d = q.shape
    nb = n // BLOCK
    scale = d ** -0.5
    qb = q.reshape(b, nb, BLOCK, kh, g, d)
    pad = ((0, 0), (BLOCK, BLOCK), (0, 0), (0, 0))
    kp, vp = jnp.pad(k, pad), jnp.pad(v, pad)
    idx = (jnp.arange(nb) * BLOCK)[:, None] + jnp.arange(3 * BLOCK)[None, :]
    kb, vb = kp[:, idx], vp[:, idx]
    qpos = (jnp.arange(nb) * BLOCK)[:, None] + jnp.arange(BLOCK)[None, :]
    kpos = idx - BLOCK
    valid = ((jnp.abs(qpos[:, :, None] - kpos[:, None, :]) <= WINDOW)
             & (kpos[:, None, :] >= 0) & (kpos[:, None, :] < n))
    s_loc = jnp.einsum('bnqhgd,bnkhd->bnhgqk', qb, kb, preferred_element_type=jnp.float32) * scale
    s_loc = jnp.where(valid[None, :, None, None], s_loc, NEG_INF)
    s_ctx = jnp.einsum('bnqhgd,bkhd->bnhgqk', qb, k_ctx, preferred_element_type=jnp.float32) * scale
    n_loc = 3 * BLOCK
    p = _softmax_with_sink(jnp.concatenate([s_loc, s_ctx], axis=-1), sink[None, None, :, :, None, None])
    p_loc = p[..., :n_loc].astype(v.dtype)
    p_ctx = p[..., n_loc:].astype(v.dtype)
    o = (jnp.einsum('bnhgqk,bnkhd->bnqhgd', p_loc, vb)
         + jnp.einsum('bnhgqk,bkhd->bnqhgd', p_ctx, v_ctx))
    return o.reshape(b, n, kh, g, d)


def _attn_project(h, w_in, q_norm, kv_norm, w_qb):
    b, n, _ = h.shape
    qa, ka, va, cq, ckv, kr = _split(h @ w_in, ATTN_IN_SIZES)
    qa = qa.reshape(b, n, A_HEADS, HEAD_DIM)
    ka = ka.reshape(b, n, A_KV_HEADS, HEAD_DIM)
    va = va.reshape(b, n, A_KV_HEADS, HEAD_DIM)
    q_mla = (rms_norm(cq, q_norm) @ w_qb).reshape(b, n, B_HEADS, QK_NOPE + QK_ROPE)
    ckv = rms_norm(ckv, kv_norm)
    return qa, ka, va, q_mla, ckv, kr


def _mla_keys(ckv, kr, w_kvb):
    b, m, _ = ckv.shape
    kv = (ckv @ w_kvb).reshape(b, m, B_HEADS, QK_NOPE + V_DIM)
    k = jnp.concatenate([kv[..., :QK_NOPE],
                         jnp.broadcast_to(kr[:, :, None, :], (b, m, B_HEADS, QK_ROPE))], axis=-1)
    return k, kv[..., QK_NOPE:]


def attn_mixer_context(h, w_in, sink, q_norm, kv_norm, w_qb, w_kvb, w_out):
    b, n, _ = h.shape
    qa, ka, va, q_mla, ckv, kr = _attn_project(h, w_in, q_norm, kv_norm, w_qb)
    oa = dense_attention(qa.reshape(b, n, A_KV_HEADS, A_GROUP, HEAD_DIM), ka, va, HEAD_DIM ** -0.5,
                         sink.reshape(A_KV_HEADS, A_GROUP))
    k_m, v_m = _mla_keys(ckv, kr, w_kvb)
    ob = dense_attention(q_mla[:, :, :, None, :], k_m, v_m, MLA_SCALE)
    out = jnp.concatenate([oa.reshape(b, n, -1), ob.reshape(b, n, -1)], axis=-1) @ w_out
    return out, ka, va, ckv, kr


def attn_mixer_latent(h, ck, cv, cckv, ckr, w_in, sink, q_norm, kv_norm, w_qb, w_kvb, w_out):
    b, n, _ = h.shape
    qa, ka, va, q_mla, ckv, kr = _attn_project(h, w_in, q_norm, kv_norm, w_qb)
    cos_a, sin_a = axial_rope(n, HEAD_DIM)
    qa = apply_rope(qa, cos_a, sin_a)
    ka = apply_rope(ka, cos_a, sin_a)
    oa = window_attention(qa.reshape(b, n, A_KV_HEADS, A_GROUP, HEAD_DIM), ka, va, ck, cv,
                          sink.reshape(A_KV_HEADS, A_GROUP))
    cos_b, sin_b = axial_rope(n, QK_ROPE)
    q_mla = jnp.concatenate([q_mla[..., :QK_NOPE], apply_rope(q_mla[..., QK_NOPE:], cos_b, sin_b)], axis=-1)
    kr = apply_rope(kr[:, :, None, :], cos_b, sin_b)[:, :, 0, :]
    k_lat, v_lat = _mla_keys(ckv, kr, w_kvb)
    k_ctx, v_ctx = _mla_keys(cckv, ckr, w_kvb)
    ob = dense_attention(q_mla[:, :, :, None, :], jnp.concatenate([k_lat, k_ctx], axis=1),
                         jnp.concatenate([v_lat, v_ctx], axis=1), MLA_SCALE)
    return jnp.concatenate([oa.reshape(b, n, -1), ob.reshape(b, n, -1)], axis=-1) @ w_out


def multiscale_pool(z, w_grp, scale):
    b, n, _ = z.shape
    zf = z.astype(jnp.float32)
    cs = jnp.concatenate([jnp.zeros((b, 1, POOL_CH), jnp.float32), jnp.cumsum(zf, axis=1)], axis=1)
    t = jnp.arange(n)
    outs = []
    for gi, w in enumerate(POOL_SIZES):
        lo = w // 2
        hi = w - lo - 1
        start = jnp.clip(t - lo, 0, n)
        end = jnp.clip(t + hi + 1, 0, n)
        sl = slice(gi * POOL_GROUP_W, (gi + 1) * POOL_GROUP_W)
        csg = cs[:, :, sl]
        mean = (csg[:, end] - csg[:, start]) / (end - start).astype(jnp.float32)[None, :, None]
        outs.append(mean - zf[:, :, sl])
    d = jnp.stack(outs, axis=2).astype(z.dtype)
    y = jnp.einsum('bngc,gcd->bngd', d, w_grp).reshape(b, n, POOL_CH)
    return y * scale


def conv_pool_mixer(h, w_in, w_dw, b_dw, ln_g, ln_b, w_grp, p_scale, w_out):
    a, gate, z = _split(h @ w_in, CONV_IN_SIZES)
    u = a * jax.nn.sigmoid(gate)
    u = lax.conv_general_dilated(u, w_dw[:, None, :], window_strides=(1,),
                                 padding=[(CONV_WIDTH // 2, CONV_WIDTH // 2)],
                                 dimension_numbers=('NWC', 'WIO', 'NWC'),
                                 feature_group_count=CONV_CH) + b_dw
    u = jax.nn.silu(layer_norm(u, ln_g, ln_b))
    pz = multiscale_pool(z, w_grp, p_scale)
    return jnp.concatenate([u, pz], axis=-1) @ w_out


def sq_relu_mlp(h, w1, w2):
    return jnp.square(jax.nn.relu(h @ w1)) @ w2


def setup_inputs(seed: int = 0) -> dict:
    key = jax.random.key(seed)
    keys = iter(jax.random.split(key, 40))

    def nrm(shape, s):
        return jax.random.normal(next(keys), shape, jnp.float32) * s

    def gain(shape):
        return 1.0 + nrm(shape, 0.05)

    return {
        'x_prompt': nrm((BATCH, SEQ, D_MODEL), 1.0),
        'x_sample': nrm((DEC_BATCH, DEC_SEQ, D_MODEL), 1.0),
        'cache_win_k': nrm((DEC_BATCH, N_EVEN, PAST_LEN, A_KV_HEADS, HEAD_DIM), 1.0),
        'cache_win_v': nrm((DEC_BATCH, N_EVEN, PAST_LEN, A_KV_HEADS, HEAD_DIM), 1.0),
        'cache_mla_ckv': nrm((DEC_BATCH, N_EVEN, PAST_LEN, KV_LORA), 1.0),
        'cache_mla_krope': nrm((DEC_BATCH, N_EVEN, PAST_LEN, QK_ROPE), 1.0),
        'c': nrm((DEC_BATCH, D_MODEL), 1.0),
        'c_ctx': nrm((D_MODEL,), 1.0),
        'w_mod': nrm((DEPTH, D_MODEL, 6 * D_MODEL), 0.5 * D_MODEL ** -0.5),
        'b_mod': nrm((DEPTH, 6 * D_MODEL), 0.02),
        'norm_g': gain((DEPTH, 2, D_MODEL)),
        'attn_w_in': nrm((N_EVEN, D_MODEL, ATTN_IN_W), D_MODEL ** -0.5),
        'attn_sink': nrm((N_EVEN, A_HEADS), 0.5),
        'mla_q_norm': gain((N_EVEN, Q_LORA)),
        'mla_kv_norm': gain((N_EVEN, KV_LORA)),
        'mla_w_qb': nrm((N_EVEN, Q_LORA, B_HEADS * (QK_NOPE + QK_ROPE)), Q_LORA ** -0.5),
        'mla_w_kvb': nrm((N_EVEN, KV_LORA, B_HEADS * (QK_NOPE + V_DIM)), KV_LORA ** -0.5),
        'attn_w_out': nrm((N_EVEN, ATTN_OUT_W, D_MODEL), ATTN_OUT_W ** -0.5),
        'conv_w_in': nrm((N_ODD, D_MODEL, CONV_IN_W), D_MODEL ** -0.5),
        'conv_dw': nrm((N_ODD, CONV_WIDTH, CONV_CH), CONV_WIDTH ** -0.5),
        'conv_dw_b': nrm((N_ODD, CONV_CH), 0.02),
        'conv_ln_g': gain((N_ODD, CONV_CH)),
        'conv_ln_b': nrm((N_ODD, CONV_CH), 0.02),
        'pool_w': nrm((N_ODD, N_POOL_GROUPS, POOL_GROUP_W, POOL_GROUP_W), POOL_GROUP_W ** -0.5),
        'pool_scale': 0.5 + nrm((N_ODD, POOL_CH), 0.05),
        'conv_w_out': nrm((N_ODD, CONV_OUT_W, D_MODEL), CONV_OUT_W ** -0.5),
        'mlp_w1': nrm((DEPTH, D_MODEL, D_FF), D_MODEL ** -0.5),
        'mlp_w2': nrm((DEPTH, D_FF, D_MODEL), D_FF ** -0.5),
        'final_g': gain((D_MODEL,)),
    }


def reference(x_prompt, x_sample, cache_win_k, cache_win_v, cache_mla_ckv, cache_mla_krope, c, c_ctx,
              w_mod, b_mod, norm_g, attn_w_in, attn_sink, mla_q_norm, mla_kv_norm, mla_w_qb, mla_w_kvb,
              attn_w_out, conv_w_in, conv_dw, conv_dw_b, conv_ln_g, conv_ln_b, pool_w, pool_scale,
              conv_w_out, mlp_w1, mlp_w2, final_g):
    xp, xs = x_prompt, x_sample
    ks, vs, ckvs, krs = [], [], [], []
    for l in range(DEPTH):
        mp = adaln(c_ctx[None, :], w_mod[l], b_mod[l])
        ms = adaln(c, w_mod[l], b_mod[l])
        hp = modulate(rms_norm(xp, norm_g[l, 0]), mp[0], mp[1])
        hs = modulate(rms_norm(xs, norm_g[l, 0]), ms[0], ms[1])
        if l % 2 == 0:
            i = l // 2
            ap = (attn_w_in[i], attn_sink[i], mla_q_norm[i], mla_kv_norm[i], mla_w_qb[i], mla_w_kvb[i],
                  attn_w_out[i])
            o_p, k_i, v_i, ckv_i, kr_i = attn_mixer_context(hp, *ap)
            o_s = attn_mixer_latent(hs, cache_win_k[:, i], cache_win_v[:, i], cache_mla_ckv[:, i],
                                    cache_mla_krope[:, i], *ap)
            ks.append(k_i)
            vs.append(v_i)
            ckvs.append(ckv_i)
            krs.append(kr_i)
        else:
            j = l // 2
            cp = (conv_w_in[j], conv_dw[j], conv_dw_b[j], conv_ln_g[j], conv_ln_b[j], pool_w[j],
                  pool_scale[j], conv_w_out[j])
            o_p = conv_pool_mixer(hp, *cp)
            o_s = conv_pool_mixer(hs, *cp)
        xp = xp + mp[2] * o_p
        xs = xs + ms[2] * o_s
        hp = modulate(rms_norm(xp, norm_g[l, 1]), mp[3], mp[4])
        hs = modulate(rms_norm(xs, norm_g[l, 1]), ms[3], ms[4])
        xp = xp + mp[5] * sq_relu_mlp(hp, mlp_w1[l], mlp_w2[l])
        xs = xs + ms[5] * sq_relu_mlp(hs, mlp_w1[l], mlp_w2[l])
    y_prompt = rms_norm(xp, final_g)
    y_sample = rms_norm(xs, final_g)
    new_win_k = jnp.stack(ks, axis=1)
    new_win_v = jnp.stack(vs, axis=1)
    new_mla_ckv = jnp.stack(ckvs, axis=1)
    new_mla_krope = jnp.stack(krs, axis=1)
    return (y_prompt, y_sample, new_win_k, new_win_v, new_mla_ckv, new_mla_krope)
```

```python
import functools
import math

import numpy as np
import jax
import jax.numpy as jnp
from jax import lax
from jax.experimental import pallas as pl
from jax.experimental.pallas import tpu as pltpu

F32 = jnp.float32
BF16 = jnp.bfloat16

D_MODEL = 1024
BATCH = 16
SEQ = 256
DEPTH = 4
DEC_BATCH = 8
DEC_SEQ = 2048
PAST_LEN = 256
GRID_W = 64
N_EVEN = (DEPTH + 1) // 2
N_ODD = DEPTH // 2
A_HEADS = 8
A_KV_HEADS = 2
HEAD_DIM = 64
WINDOW = 128
B_HEADS = 8
Q_LORA = 192
KV_LORA = 128
QK_NOPE = 64
QK_ROPE = 32
V_DIM = 64
MLA_SCALE = (QK_NOPE + QK_ROPE) ** -0.5
CONV_CH = D_MODEL // 2
CONV_WIDTH = 31
POOL_CH = D_MODEL // 2
POOL_SIZES = (2, 4, 8, 16)
POOL_GROUP_W = POOL_CH // len(POOL_SIZES)
D_FF = 4 * D_MODEL
ROPE_BASE = 10000.0
EPS = 1e-6
NEG_INF = -1e30
LOG2E = math.log2(math.e)

LANES = 128
VMEM_LIMIT_BYTES = 56 * 1024 * 1024

TILE = 256
T_CTX = BATCH * SEQ
T_LAT = DEC_BATCH * DEC_SEQ
T_ALL = T_CTX + T_LAT
N_CTX_TILES = T_CTX // TILE
N_TILES = T_ALL // TILE
LAT_TILES_PER_SEQ = DEC_SEQ // TILE
N_COND = 16
BQ = 256
HALO = 16
Q_LORA_PAD = 256
ATTN_IN_COLS = 512 + 128 + 128 + Q_LORA_PAD + 128 + 128


def _cparams(sem):
    return pltpu.CompilerParams(dimension_semantics=sem, vmem_limit_bytes=VMEM_LIMIT_BYTES)


def _mod_row(i):
    return jnp.where(i < N_CTX_TILES, 0, 1 + (i - N_CTX_TILES) // LAT_TILES_PER_SEQ)


def _pos_block(i):
    return jnp.where(i < N_CTX_TILES, 0, 1 + (i - N_CTX_TILES) % LAT_TILES_PER_SEQ)


def _dot(a, b):
    return jnp.dot(a, b, preferred_element_type=F32)


def _dot_nt(a, b):
    return lax.dot_general(a, b, (((1,), (1,)), ((), ())), preferred_element_type=F32)


def _rms_mod(x, g, shift, scale):
    ms = jnp.mean(x * x, axis=-1, keepdims=True)
    return x * lax.rsqrt(ms + EPS) * (g * (1.0 + scale)) + shift


def _rope(x, c, s1, s2, half):
    return x * c + pltpu.roll(x, LANES - half, 1) * s1 + pltpu.roll(x, half, 1) * s2


def _lane_lt64(shape):
    return lax.broadcasted_iota(jnp.int32, shape, len(shape) - 1) < 64


def _mod_kernel(cond_ref, w_ref, b_ref, o_ref):
    c = cond_ref[...]
    s = c * jax.nn.sigmoid(c)
    o_ref[...] = _dot(s.astype(BF16), w_ref[...].astype(BF16)) + b_ref[...]


def _mod_table(cond, w_mod, b_mod):
    nb = 6 * D_MODEL // 1024
    return pl.pallas_call(
        _mod_kernel,
        grid=(DEPTH, nb),
        in_specs=[
            pl.BlockSpec((N_COND, D_MODEL), lambda l, n: (0, 0)),
            pl.BlockSpec((None, D_MODEL, 1024), lambda l, n: (l, 0, n)),
            pl.BlockSpec((None, 1, 1024), lambda l, n: (l, 0, n)),
        ],
        out_specs=pl.BlockSpec((None, N_COND, 1024), lambda l, n: (l, 0, n)),
        out_shape=jax.ShapeDtypeStruct((DEPTH, N_COND, 6 * D_MODEL), F32),
        compiler_params=_cparams(("arbitrary", "arbitrary")),
        name="mod_table",
    )(cond, w_mod, b_mod.reshape(DEPTH, 1, 6 * D_MODEL))


def _attn_in_kernel(x_ref, mod_ref, g_ref, w_ref, qn_ref, kvn_ref, wqb_ref, wkk_ref, wkv_ref, rope_ref,
                    qa_ref, kd_ref, vd_ref, qm_ref, km_ref, vm_ref,
                    ka32_ref, va32_ref, ckv32_ref, kr32_ref):
    i = pl.program_id(0)
    h = _rms_mod(x_ref[...], g_ref[...], mod_ref[0:1, :], mod_ref[1:2, :])
    y = _dot(h.astype(BF16), w_ref[...])

    ca, s1a, s2a = rope_ref[0], rope_ref[1], rope_ref[2]
    cb, s1b, s2b = rope_ref[3], rope_ref[4], rope_ref[5]
    lt64 = _lane_lt64((TILE, LANES))

    for c in range(4):
        q = _rope(y[:, c * LANES:(c + 1) * LANES], ca, s1a, s2a, HEAD_DIM // 2)
        qa_ref[:, c * LANES:(c + 1) * LANES] = (q * (HEAD_DIM ** -0.5 * LOG2E)).astype(BF16)

    ka = _rope(y[:, 512:640], ca, s1a, s2a, HEAD_DIM // 2)
    va = y[:, 640:768]
    ka_sw = pltpu.roll(ka, 64, 1)
    va_sw = pltpu.roll(va, 64, 1)
    kd_ref[:, 0:LANES] = jnp.where(lt64, ka, ka_sw).astype(BF16)
    kd_ref[:, LANES:2 * LANES] = jnp.where(lt64, ka_sw, ka).astype(BF16)
    vd_ref[:, 0:LANES] = jnp.where(lt64, va, va_sw).astype(BF16)
    vd_ref[:, LANES:2 * LANES] = jnp.where(lt64, va_sw, va).astype(BF16)

    cq = y[:, 768:768 + Q_LORA_PAD]
    cqn = cq * lax.rsqrt(jnp.sum(cq * cq, axis=-1, keepdims=True) * (1.0 / Q_LORA) + EPS) * qn_ref[...]
    qm = _dot(cqn.astype(BF16), wqb_ref[...])
    for hh in range(B_HEADS):
        q = _rope(qm[:, hh * LANES:(hh + 1) * LANES], cb, s1b, s2b, QK_ROPE // 2)
        qm_ref[:, hh * LANES:(hh + 1) * LANES] = (q * (MLA_SCALE * LOG2E)).astype(BF16)

    ckv = y[:, 1024:1152]
    ckvn = ckv * lax.rsqrt(jnp.mean(ckv * ckv, axis=-1, keepdims=True) + EPS) * kvn_ref[...]
    kr = y[:, 1152:1280]
    ckvn_b = ckvn.astype(BF16)
    kn = _dot(ckvn_b, wkk_ref[...])
    krr = _rope(kr, cb, s1b, s2b, QK_ROPE // 2)
    for hh in range(B_HEADS):
        km_ref[:, hh * LANES:(hh + 1) * LANES] = (kn[:, hh * LANES:(hh + 1) * LANES] + krr).astype(BF16)
    vm_ref[...] = _dot(ckvn_b, wkv_ref[...]).astype(BF16)

    @pl.when(i < N_CTX_TILES)
    def _():
        ka32_ref[...] = ka
        va32_ref[...] = va
        ckv32_ref[...] = ckvn
        kr32_ref[...] = kr


def _attn_in(x, mod, g, wp, rope, layer):
    tile_map = lambda i: (i, 0)
    const2 = lambda i: (0, 0)
    ctx_map = lambda i: (jnp.minimum(i, N_CTX_TILES - 1), 0)
    bf = lambda w: jax.ShapeDtypeStruct((T_ALL, w), BF16)
    c32 = jax.ShapeDtypeStruct((T_CTX, LANES), F32)
    return pl.pallas_call(
        _attn_in_kernel,
        grid=(N_TILES,),
        in_specs=[
            pl.BlockSpec((TILE, D_MODEL), tile_map),
            pl.BlockSpec((None, None, 6, D_MODEL), lambda i: (layer, _mod_row(i), 0, 0)),
            pl.BlockSpec((1, D_MODEL), const2),
            pl.BlockSpec((D_MODEL, ATTN_IN_COLS), const2),
            pl.BlockSpec((1, Q_LORA_PAD), const2),
            pl.BlockSpec((1, KV_LORA), const2),
            pl.BlockSpec((Q_LORA_PAD, B_HEADS * LANES), const2),
            pl.BlockSpec((KV_LORA, B_HEADS * LANES), const2),
            pl.BlockSpec((KV_LORA, B_HEADS * V_DIM), const2),
            pl.BlockSpec((6, TILE, LANES), lambda i: (0, _pos_block(i), 0)),
        ],
        out_specs=[
            pl.BlockSpec((TILE, 512), tile_map),
            pl.BlockSpec((TILE, 256), tile_map),
            pl.BlockSpec((TILE, 256), tile_map),
            pl.BlockSpec((TILE, 1024), tile_map),
            pl.BlockSpec((TILE, 1024), tile_map),
            pl.BlockSpec((TILE, 512), tile_map),
            pl.BlockSpec((TILE, LANES), ctx_map),
            pl.BlockSpec((TILE, LANES), ctx_map),
            pl.BlockSpec((TILE, LANES), ctx_map),
            pl.BlockSpec((TILE, LANES), ctx_map),
        ],
        out_shape=[bf(512), bf(256), bf(256), bf(1024), bf(1024), bf(512), c32, c32, c32, c32],
        compiler_params=_cparams(("arbitrary",)),
        name="attn_in",
    )(x, mod, g, wp["w_in"], wp["q_norm"], wp["kv_norm"], wp["w_qb"], wp["w_kk"], wp["w_kv"], rope)


def _ctx_kv_kernel(ck_ref, cv_ref, cckv_ref, ckr_ref, wkk_ref, wkv_ref, kd_ref, vd_ref, km_ref, vm_ref):
    lt64 = _lane_lt64((PAST_LEN, LANES))
    k = ck_ref[...]
    v = cv_ref[...]
    k_sw = pltpu.roll(k, 64, 1)
    v_sw = pltpu.roll(v, 64, 1)
    kd_ref[:, 0:LANES] = jnp.where(lt64, k, k_sw).astype(BF16)
    kd_ref[:, LANES:2 * LANES] = jnp.where(lt64, k_sw, k).astype(BF16)
    vd_ref[:, 0:LANES] = jnp.where(lt64, v, v_sw).astype(BF16)
    vd_ref[:, LANES:2 * LANES] = jnp.where(lt64, v_sw, v).astype(BF16)
    cb = cckv_ref[...].astype(BF16)
    kn = _dot(cb, wkk_ref[...])
    kr = ckr_ref[...]
    for hh in range(B_HEADS):
        km_ref[:, hh * LANES:(hh + 1) * LANES] = (kn[:, hh * LANES:(hh + 1) * LANES] + kr).astype(BF16)
    vm_ref[...] = _dot(cb, wkv_ref[...]).astype(BF16)


def _ctx_kv(cache_k, cache_v, cache_ckv, cache_kr128, w_kk, w_kv):
    cache_map = lambda i, b: (b, i, 0, 0)
    w_map = lambda i, b: (i, 0, 0)
    out_map = lambda i, b: (i, b, 0, 0)
    o = lambda w: jax.ShapeDtypeStruct((N_EVEN, DEC_BATCH, PAST_LEN, w), BF16)
    return pl.pallas_call(
        _ctx_kv_kernel,
        grid=(N_EVEN, DEC_BATCH),
        in_specs=[
            pl.BlockSpec((None, None, PAST_LEN, LANES), cache_map),
            pl.BlockSpec((None, None, PAST_LEN, LANES), cache_map),
            pl.BlockSpec((None, None, PAST_LEN, KV_LORA), cache_map),
            pl.BlockSpec((None, None, PAST_LEN, LANES), cache_map),
            pl.BlockSpec((None, KV_LORA, B_HEADS * LANES), w_map),
            pl.BlockSpec((None, KV_LORA, B_HEADS * V_DIM), w_map),
        ],
        out_specs=[
            pl.BlockSpec((None, None, PAST_LEN, 256), out_map),
            pl.BlockSpec((None, None, PAST_LEN, 256), out_map),
            pl.BlockSpec((None, None, PAST_LEN, 1024), out_map),
            pl.BlockSpec((None, None, PAST_LEN, 512), out_map),
        ],
        out_shape=[o(256), o(256), o(1024), o(512)],
        compiler_params=_cparams(("arbitrary", "arbitrary")),
        name="ctx_kv",
    )(cache_k, cache_v, cache_ckv, cache_kr128, w_kk, w_kv)


def _softmax_pv(q, segs, sink):
    ss = []
    for k, _, valid in segs:
        s = _dot_nt(q, k)
        if valid is not None:
            s = jnp.where(valid, s, NEG_INF)
        ss.append(s)
    m = ss[0].max(axis=-1, keepdims=True)
    for s in ss[1:]:
        m = jnp.maximum(m, s.max(axis=-1, keepdims=True))
    if sink is not None:
        m = jnp.maximum(m, sink)
    l = None
    o = None
    for s, (_, v, _) in zip(ss, segs):
        p = jnp.exp2(s - m)
        ls = p.sum(axis=-1, keepdims=True)
        os_ = _dot(p.astype(BF16), v)
        l = ls if l is None else l + ls
        o = os_ if o is None else o + os_
    if sink is not None:
        l = l + jnp.exp2(sink - m)
    return o / l


def _attn_heads(qa_ref, qm_ref, a_segs, m_segs, sink_ref, o_ref, rows):
    lt64 = _lane_lt64((rows, LANES))
    lane = lax.broadcasted_iota(jnp.int32, (1, LANES), 1)
    keep_lo = (lane < 64).astype(BF16)
    keep_hi = (lane >= 64).astype(BF16)
    col = lambda n: slice(n * LANES, (n + 1) * LANES)
    for c in range(4):
        kh = c // 2
        qc = qa_ref[:, col(c)]
        segs = [(kd[rs, col(kh)], vd[rs, col(kh)], valid) for kd, vd, rs, valid in a_segs]
        o_lo = _softmax_pv(qc * keep_lo, segs, sink_ref[2 * c])
        o_hi = _softmax_pv(qc * keep_hi, segs, sink_ref[2 * c + 1])
        o_ref[:, col(c)] = jnp.where(lt64, o_lo, o_hi).astype(BF16)
    for c in range(4):
        outs = []
        for half in range(2):
            hh = 2 * c + half
            segs = [(km[:, col(hh)], vm[:, col(c)], None) for km, vm in m_segs]
            outs.append(_softmax_pv(qm_ref[:, col(hh)], segs, None))
        o_ref[:, col(4 + c)] = jnp.where(lt64, outs[0], outs[1]).astype(BF16)


def _attn_lat_kernel(sink_ref, qa_ref, qm_ref, kd_ref, vd_ref, km_ref, vm_ref,
                     kdc_ref, vdc_ref, kmc_ref, vmc_ref, o_ref):
    j = pl.program_id(1)
    q0 = j * BQ
    nloc = BQ + 2 * WINDOW
    start = pl.multiple_of(jnp.clip(q0 - WINDOW, 0, DEC_SEQ - nloc), WINDOW)
    qpos = q0 + lax.broadcasted_iota(jnp.int32, (BQ, nloc), 0)
    kpos = start + lax.broadcasted_iota(jnp.int32, (BQ, nloc), 1)
    valid = jnp.abs(qpos - kpos) <= WINDOW
    everything = slice(None)
    a_segs = [(kd_ref, vd_ref, pl.ds(start, nloc), valid), (kdc_ref, vdc_ref, everything, None)]
    m_segs = [(km_ref, vm_ref), (kmc_ref, vmc_ref)]
    _attn_heads(qa_ref, qm_ref, a_segs, m_segs, sink_ref, o_ref, BQ)


def _attn_ctx_kernel(sink_ref, qa_ref, qm_ref, kd_ref, vd_ref, km_ref, vm_ref, o_ref):
    a_segs = [(kd_ref, vd_ref, slice(None), None)]
    m_segs = [(km_ref, vm_ref)]
    _attn_heads(qa_ref, qm_ref, a_segs, m_segs, sink_ref, o_ref, SEQ)


def _attention(sink2, qa, kd, vd, qm, km, vm, kdc, vdc, kmc, vmc, layer_i):
    smem = pl.BlockSpec(memory_space=pltpu.SMEM)
    nq = DEC_SEQ // BQ
    q_off = T_CTX // BQ
    s_off = T_CTX // DEC_SEQ
    q_map = lambda b, j: (q_off + b * nq + j, 0)
    kv_map = lambda b, j: (s_off + b, 0)
    c_map = lambda b, j: (layer_i, b, 0, 0)
    o_lat = pl.pallas_call(
        _attn_lat_kernel,
        grid=(DEC_BATCH, nq),
        in_specs=[
            smem,
            pl.BlockSpec((BQ, 512), q_map),
            pl.BlockSpec((BQ, 1024), q_map),
            pl.BlockSpec((DEC_SEQ, 256), kv_map),
            pl.BlockSpec((DEC_SEQ, 256), kv_map),
            pl.BlockSpec((DEC_SEQ, 1024), kv_map),
            pl.BlockSpec((DEC_SEQ, 512), kv_map),
            pl.BlockSpec((None, None, PAST_LEN, 256), c_map),
            pl.BlockSpec((None, None, PAST_LEN, 256), c_map),
            pl.BlockSpec((None, None, PAST_LEN, 1024), c_map),
            pl.BlockSpec((None, None, PAST_LEN, 512), c_map),
        ],
        out_specs=pl.BlockSpec((BQ, 1024), lambda b, j: (b * nq + j, 0)),
        out_shape=jax.ShapeDtypeStruct((T_LAT, 1024), BF16),
        compiler_params=_cparams(("arbitrary", "arbitrary")),
        name="attn_latent",
    )(sink2, qa, qm, kd, vd, km, vm, kdc, vdc, kmc, vmc)
    b_map = lambda b: (b, 0)
    o_ctx = pl.pallas_call(
        _attn_ctx_kernel,
        grid=(BATCH,),
        in_specs=[
            smem,
            pl.BlockSpec((SEQ, 512), b_map),
            pl.BlockSpec((SEQ, 1024), b_map),
            pl.BlockSpec((SEQ, 256), b_map),
            pl.BlockSpec((SEQ, 256), b_map),
            pl.BlockSpec((SEQ, 1024), b_map),
            pl.BlockSpec((SEQ, 512), b_map),
        ],
        out_specs=pl.BlockSpec((SEQ, 1024), b_map),
        out_shape=jax.ShapeDtypeStruct((T_CTX, 1024), BF16),
        compiler_params=_cparams(("arbitrary",)),
        name="attn_context",
    )(sink2, qa, qm, kd, vd, km, vm)
    return jnp.concatenate([o_ctx, o_lat], axis=0)


def _conv_in_kernel(x_ref, mod_ref, g_ref, w_ref, u_ref, z_ref):
    h = _rms_mod(x_ref[...], g_ref[...], mod_ref[0:1, :], mod_ref[1:2, :])
    y = _dot(h.astype(BF16), w_ref[...])
    a = y[:, 0:CONV_CH]
    gate = y[:, CONV_CH:2 * CONV_CH]
    u_ref[...] = a * jax.nn.sigmoid(gate)
    z_ref[...] = y[:, 2 * CONV_CH:]


def _conv_in(x, mod, g, w_in, layer):
    tile_map = lambda i: (i, 0)
    const2 = lambda i: (0, 0)
    return pl.pallas_call(
        _conv_in_kernel,
        grid=(N_TILES,),
        in_specs=[
            pl.BlockSpec((TILE, D_MODEL), tile_map),
            pl.BlockSpec((None, None, 6, D_MODEL), lambda i: (layer, _mod_row(i), 0, 0)),
            pl.BlockSpec((1, D_MODEL), const2),
            pl.BlockSpec((D_MODEL, 3 * CONV_CH), const2),
        ],
        out_specs=[pl.BlockSpec((TILE, CONV_CH), tile_map), pl.BlockSpec((TILE, POOL_CH), tile_map)],
        out_shape=[jax.ShapeDtypeStruct((T_ALL, CONV_CH), F32), jax.ShapeDtypeStruct((T_ALL, POOL_CH), F32)],
        compiler_params=_cparams(("arbitrary",)),
        name="conv_in",
    )(x, mod, g, w_in)


CONV_ROWS = 32


def _conv_mix_kernel(u_ref, up_ref, un_ref, z_ref, zp_ref, zn_ref, wdw_ref, bdw_ref, lng_ref, lnb_ref,
                     wg_ref, ps_ref, o_ref, ue_ref, ze_ref):
    i = pl.program_id(0)
    j = (i - N_CTX_TILES) % LAT_TILES_PER_SEQ
    is_lat = i >= N_CTX_TILES
    has_prev = jnp.logical_and(is_lat, j > 0)
    has_next = jnp.logical_and(is_lat, j < LAT_TILES_PER_SEQ - 1)
    seq_len = jnp.where(is_lat, DEC_SEQ, SEQ)
    t0 = jnp.where(is_lat, j * TILE, 0)

    ue_ref[0:HALO, :] = jnp.where(has_prev, up_ref[...], 0.0)
    ue_ref[HALO:HALO + TILE, :] = u_ref[...]
    ue_ref[HALO + TILE:, :] = jnp.where(has_next, un_ref[...], 0.0)
    ze_ref[0:HALO, :] = jnp.where(has_prev, zp_ref[...], 0.0)
    ze_ref[HALO:HALO + TILE, :] = z_ref[...]
    ze_ref[HALO + TILE:, :] = jnp.where(has_next, zn_ref[...], 0.0)

    pad = CONV_WIDTH // 2
    for r in range(TILE // CONV_ROWS):
        r0 = r * CONV_ROWS
        acc = jnp.zeros((CONV_ROWS, CONV_CH), F32) + bdw_ref[...]
        for k in range(CONV_WIDTH):
            e0 = HALO + r0 + k - pad
            acc = acc + ue_ref[e0:e0 + CONV_ROWS, :] * wdw_ref[k:k + 1, :]
        mu = jnp.mean(acc, axis=-1, keepdims=True)
        d = acc - mu
        var = jnp.mean(d * d, axis=-1, keepdims=True)
        yn = d * lax.rsqrt(var + EPS) * lng_ref[...] + lnb_ref[...]
        o_ref[r0:r0 + CONV_ROWS, 0:CONV_CH] = (yn * jax.nn.sigmoid(yn)).astype(BF16)

    t = t0 + lax.broadcasted_iota(jnp.int32, (TILE, POOL_GROUP_W), 0)
    for gi, w in enumerate(POOL_SIZES):
        lo = w // 2
        hi = w - lo - 1
        cols = slice(gi * POOL_GROUP_W, (gi + 1) * POOL_GROUP_W)
        tot = ze_ref[HALO - lo:HALO - lo + TILE, cols]
        for s in range(-lo + 1, hi + 1):
            tot = tot + ze_ref[HALO + s:HALO + s + TILE, cols]
        cnt = jnp.minimum(t + hi + 1, seq_len) - jnp.maximum(t - lo, 0)
        d = tot / cnt.astype(F32) - ze_ref[HALO:HALO + TILE, cols]
        y = _dot(d.astype(BF16), wg_ref[gi]) * ps_ref[:, cols]
        o_ref[:, CONV_CH + gi * POOL_GROUP_W:CONV_CH + (gi + 1) * POOL_GROUP_W] = y.astype(BF16)


def _conv_mix(u, z, cp):
    tile_map = lambda i: (i, 0)
    const2 = lambda i: (0, 0)
    hb = TILE // HALO
    prev_map = lambda i: (jnp.maximum(i * hb - 1, 0), 0)
    next_map = lambda i: (jnp.minimum((i + 1) * hb, T_ALL // HALO - 1), 0)
    return pl.pallas_call(
        _conv_mix_kernel,
        grid=(N_TILES,),
        in_specs=[
            pl.BlockSpec((TILE, CONV_CH), tile_map),
            pl.BlockSpec((HALO, CONV_CH), prev_map),
            pl.BlockSpec((HALO, CONV_CH), next_map),
            pl.BlockSpec((TILE, POOL_CH), tile_map),
            pl.BlockSpec((HALO, POOL_CH), prev_map),
            pl.BlockSpec((HALO, POOL_CH), next_map),
            pl.BlockSpec((CONV_WIDTH, CONV_CH), const2),
            pl.BlockSpec((1, CONV_CH), const2),
            pl.BlockSpec((1, CONV_CH), const2),
            pl.BlockSpec((1, CONV_CH), const2),
            pl.BlockSpec((len(POOL_SIZES), POOL_GROUP_W, POOL_GROUP_W), lambda i: (0, 0, 0)),
            pl.BlockSpec((1, POOL_CH), const2),
        ],
        out_specs=pl.BlockSpec((TILE, D_MODEL), tile_map),
        out_shape=jax.ShapeDtypeStruct((T_ALL, D_MODEL), BF16),
        scratch_shapes=[pltpu.VMEM((TILE + 2 * HALO, CONV_CH), F32), pltpu.VMEM((TILE + 2 * HALO, POOL_CH), F32)],
        compiler_params=_cparams(("arbitrary",)),
        name="conv_mix",
    )(u, u, u, z, z, z, cp["w_dw"], cp["b_dw"], cp["ln_g"], cp["ln_b"], cp["w_grp"], cp["p_scale"])


def _post_kernel(o_ref, x_ref, mod_ref, g_ref, wo_ref, w1_ref, w2_ref, fg_ref, y_ref, *, final):
    x1 = x_ref[...] + mod_ref[2:3, :] * _dot(o_ref[...], wo_ref[...])
    h = _rms_mod(x1, g_ref[...], mod_ref[3:4, :], mod_ref[4:5, :])
    a = jnp.maximum(_dot(h.astype(BF16), w1_ref[...]), 0.0)
    x2 = x1 + mod_ref[5:6, :] * _dot((a * a).astype(BF16), w2_ref[...])
    if final:
        ms = jnp.mean(x2 * x2, axis=-1, keepdims=True)
        x2 = x2 * lax.rsqrt(ms + EPS) * fg_ref[...]
    y_ref[...] = x2


def _post(o, x, mod, g, w_out, w1, w2, final_g, layer, final):
    tile_map = lambda i: (i, 0)
    const2 = lambda i: (0, 0)
    return pl.pallas_call(
        functools.partial(_post_kernel, final=final),
        grid=(N_TILES,),
        in_specs=[
            pl.BlockSpec((TILE, D_MODEL), tile_map),
            pl.BlockSpec((TILE, D_MODEL), tile_map),
            pl.BlockSpec((None, None, 6, D_MODEL), lambda i: (layer, _mod_row(i), 0, 0)),
            pl.BlockSpec((1, D_MODEL), const2),
            pl.BlockSpec((D_MODEL, D_MODEL), const2, pipeline_mode=pl.Buffered(1)),
            pl.BlockSpec((D_MODEL, D_FF), const2, pipeline_mode=pl.Buffered(1)),
            pl.BlockSpec((D_FF, D_MODEL), const2, pipeline_mode=pl.Buffered(1)),
            pl.BlockSpec((1, D_MODEL), const2),
        ],
        out_specs=pl.BlockSpec((TILE, D_MODEL), tile_map),
        out_shape=jax.ShapeDtypeStruct((T_ALL, D_MODEL), F32),
        compiler_params=_cparams(("arbitrary",)),
        name="post_final" if final else "post",
    )(o, x, mod, g, w_out, w1, w2, final_g)


def _rope_tables():
    n = DEC_SEQ
    rows = n // GRID_W
    row = jnp.repeat(jnp.arange(rows), GRID_W).astype(F32)
    col = jnp.tile(jnp.arange(GRID_W), rows).astype(F32)

    def angles(dim):
        quarter = dim // 4
        inv_freq = ROPE_BASE ** (-jnp.arange(quarter, dtype=F32) / quarter)
        return jnp.concatenate([row[:, None] * inv_freq, col[:, None] * inv_freq], axis=-1)

    ang_a = angles(HEAD_DIM)
    cos_a, sin_a = jnp.cos(ang_a), jnp.sin(ang_a)
    z32 = jnp.zeros((n, 32), F32)
    c_a = jnp.concatenate([cos_a] * 4, axis=-1)
    s1_a = jnp.concatenate([-sin_a, z32] * 2, axis=-1)
    s2_a = jnp.concatenate([z32, sin_a] * 2, axis=-1)
    ang_b = angles(QK_ROPE)
    cos_b, sin_b = jnp.cos(ang_b), jnp.sin(ang_b)
    one64, z64 = jnp.ones((n, 64), F32), jnp.zeros((n, 64), F32)
    one32 = jnp.ones((n, 32), F32)
    z16 = jnp.zeros((n, 16), F32)
    c_b = jnp.concatenate([one64, cos_b, cos_b, one32], axis=-1)
    s1_b = jnp.concatenate([z64, -sin_b, z16, z32], axis=-1)
    s2_b = jnp.concatenate([z64, z16, sin_b, z32], axis=-1)
    lat = jnp.stack([c_a, s1_a, s2_a, c_b, s1_b, s2_b])
    ident = jnp.stack([jnp.ones((TILE, LANES), F32), jnp.zeros((TILE, LANES), F32),
                       jnp.zeros((TILE, LANES), F32)] * 2)
    return jnp.concatenate([ident, lat], axis=1)


def _prep_even(attn_w_in, mla_q_norm, mla_kv_norm, mla_w_qb, mla_w_kvb, attn_w_out, i):
    w = attn_w_in[i]
    o = np.cumsum((0, 512, 128, 128, Q_LORA, KV_LORA, QK_ROPE))
    zeros = lambda n: jnp.zeros((D_MODEL, n), F32)
    w_in = jnp.concatenate([
        w[:, o[0]:o[3]],
        w[:, o[3]:o[4]], zeros(Q_LORA_PAD - Q_LORA),
        w[:, o[4]:o[5]],
        zeros(64), w[:, o[5]:o[6]], zeros(32),
    ], axis=1).astype(BF16)
    q_norm = jnp.pad(mla_q_norm[i], (0, Q_LORA_PAD - Q_LORA)).reshape(1, Q_LORA_PAD)
    wqb = mla_w_qb[i].reshape(Q_LORA, B_HEADS, QK_NOPE + QK_ROPE)
    wqb = jnp.pad(wqb, ((0, Q_LORA_PAD - Q_LORA), (0, 0), (0, LANES - QK_NOPE - QK_ROPE)))
    wkvb = mla_w_kvb[i].reshape(KV_LORA, B_HEADS, QK_NOPE + V_DIM)
    w_kk = jnp.pad(wkvb[:, :, :QK_NOPE], ((0, 0), (0, 0), (0, LANES - QK_NOPE)))
    w_kv = wkvb[:, :, QK_NOPE:]
    return {
        "w_in": w_in,
        "q_norm": q_norm,
        "kv_norm": mla_kv_norm[i].reshape(1, KV_LORA),
        "w_qb": wqb.reshape(Q_LORA_PAD, B_HEADS * LANES).astype(BF16),
        "w_kk": w_kk.reshape(KV_LORA, B_HEADS * LANES).astype(BF16),
        "w_kv": w_kv.reshape(KV_LORA, B_HEADS * V_DIM).astype(BF16),
        "w_out": attn_w_out[i].astype(BF16),
    }


def kernel(x_prompt, x_sample, cache_win_k, cache_win_v, cache_mla_ckv, cache_mla_krope, c, c_ctx, w_mod, b_mod,
           norm_g, attn_w_in, attn_sink, mla_q_norm, mla_kv_norm, mla_w_qb, mla_w_kvb, attn_w_out, conv_w_in,
           conv_dw, conv_dw_b, conv_ln_g, conv_ln_b, pool_w, pool_scale, conv_w_out, mlp_w1, mlp_w2, final_g):
    x = jnp.concatenate([x_prompt.reshape(T_CTX, D_MODEL), x_sample.reshape(T_LAT, D_MODEL)], axis=0)

    cond = jnp.concatenate([c_ctx[None, :], c, jnp.zeros((N_COND - 1 - DEC_BATCH, D_MODEL), F32)], axis=0)
    mod = _mod_table(cond, w_mod, b_mod).reshape(DEPTH, N_COND, 6, D_MODEL)

    rope = _rope_tables()
    even = [_prep_even(attn_w_in, mla_q_norm, mla_kv_norm, mla_w_qb, mla_w_kvb, attn_w_out, i)
            for i in range(N_EVEN)]
    cache_kr128 = jnp.pad(cache_mla_krope, ((0, 0), (0, 0), (0, 0), (64, 32)))
    kdc, vdc, kmc, vmc = _ctx_kv(
        cache_win_k.reshape(DEC_BATCH, N_EVEN, PAST_LEN, LANES),
        cache_win_v.reshape(DEC_BATCH, N_EVEN, PAST_LEN, LANES),
        cache_mla_ckv, cache_kr128,
        jnp.stack([e["w_kk"] for e in even]), jnp.stack([e["w_kv"] for e in even]))
    final_g2 = final_g.reshape(1, D_MODEL)

    ks, vs, ckvs, krs = [], [], [], []
    for l in range(DEPTH):
        g_mix = norm_g[l, 0].reshape(1, D_MODEL)
        g_mlp = norm_g[l, 1].reshape(1, D_MODEL)
        if l % 2 == 0:
            i = l // 2
            wp = even[i]
            qa, kd, vd, qm, km, vm, ka32, va32, ckv32, kr32 = _attn_in(x, mod, g_mix, wp, rope, l)
            ks.append(ka32.reshape(BATCH, SEQ, A_KV_HEADS, HEAD_DIM))
            vs.append(va32.reshape(BATCH, SEQ, A_KV_HEADS, HEAD_DIM))
            ckvs.append(ckv32.reshape(BATCH, SEQ, KV_LORA))
            krs.append(kr32[:, 64:64 + QK_ROPE].reshape(BATCH, SEQ, QK_ROPE))
            o = _attention(attn_sink[i] * LOG2E, qa, kd, vd, qm, km, vm, kdc, vdc, kmc, vmc, i)
            w_out = wp["w_out"]
        else:
            jj = l // 2
            u, z = _conv_in(x, mod, g_mix, conv_w_in[jj].astype(BF16), l)
            cp = {
                "w_dw": conv_dw[jj],
                "b_dw": conv_dw_b[jj].reshape(1, CONV_CH),
                "ln_g": conv_ln_g[jj].reshape(1, CONV_CH),
                "ln_b": conv_ln_b[jj].reshape(1, CONV_CH),
                "w_grp": pool_w[jj].astype(BF16),
                "p_scale": pool_scale[jj].reshape(1, POOL_CH),
            }
            o = _conv_mix(u, z, cp)
            w_out = conv_w_out[jj].astype(BF16)
        x = _post(o, x, mod, g_mlp, w_out, mlp_w1[l].astype(BF16), mlp_w2[l].astype(BF16), final_g2, l,
                  final=(l == DEPTH - 1))

    y_prompt = x[:T_CTX].reshape(BATCH, SEQ, D_MODEL)
    y_sample = x[T_CTX:].reshape(DEC_BATCH, DEC_SEQ, D_MODEL)
    return (y_prompt, y_sample, jnp.stack(ks, axis=1), jnp.stack(vs, axis=1),
            jnp.stack(ckvs, axis=1), jnp.stack(krs, axis=1))
```

```python
import functools
import math

import numpy as np
import jax
import jax.numpy as jnp
from jax import lax
from jax.experimental import pallas as pl
from jax.experimental.pallas import tpu as pltpu

F32 = jnp.float32
BF16 = jnp.bfloat16

D_MODEL = 1024
BATCH = 16
SEQ = 256
DEPTH = 4
DEC_BATCH = 8
DEC_SEQ = 2048
PAST_LEN = 256
GRID_W = 64
N_EVEN = (DEPTH + 1) // 2
N_ODD = DEPTH // 2
A_HEADS = 8
A_KV_HEADS = 2
HEAD_DIM = 64
WINDOW = 128
B_HEADS = 8
Q_LORA = 192
KV_LORA = 128
QK_NOPE = 64
QK_ROPE = 32
V_DIM = 64
MLA_SCALE = (QK_NOPE + QK_ROPE) ** -0.5
CONV_CH = D_MODEL // 2
CONV_WIDTH = 31
POOL_CH = D_MODEL // 2
POOL_SIZES = (2, 4, 8, 16)
POOL_GROUP_W = POOL_CH // len(POOL_SIZES)
D_FF = 4 * D_MODEL
ROPE_BASE = 10000.0
EPS = 1e-6
NEG_INF = -1e30
LOG2E = math.log2(math.e)

LANES = 128
SUBLANES = 8
VMEM_LIMIT_BYTES = 56 * 1024 * 1024

TILE = 256
T_CTX = BATCH * SEQ
T_LAT = DEC_BATCH * DEC_SEQ
T_ALL = T_CTX + T_LAT
N_CTX_TILES = T_CTX // TILE
N_TILES = T_ALL // TILE
LAT_TILES_PER_SEQ = DEC_SEQ // TILE
N_COND = 16
BQ = 256
HALO = 16
Q_LORA_PAD = 256
ATTN_IN_COLS = 512 + 128 + 128 + Q_LORA_PAD + 128 + 128


def _cparams(sem):
    return pltpu.CompilerParams(dimension_semantics=sem, vmem_limit_bytes=VMEM_LIMIT_BYTES)


def _mod_row(i):
    return jnp.where(i < N_CTX_TILES, 0, 1 + (i - N_CTX_TILES) // LAT_TILES_PER_SEQ)


def _pos_block(i):
    return jnp.where(i < N_CTX_TILES, 0, 1 + (i - N_CTX_TILES) % LAT_TILES_PER_SEQ)


def _tok_specs(a, width):
    if isinstance(a, tuple):
        return list(a), [pl.BlockSpec((TILE, width), lambda i: (jnp.minimum(i, N_CTX_TILES - 1), 0)),
                         pl.BlockSpec((TILE, width), lambda i: (jnp.maximum(i - N_CTX_TILES, 0), 0))]
    return [a], [pl.BlockSpec((TILE, width), lambda i: (i, 0))]


def _tok_load(refs):
    if len(refs) == 1:
        return refs[0][...]
    return jnp.where(pl.program_id(0) < N_CTX_TILES, refs[0][...], refs[1][...])


def _tok_store(refs, val):
    if len(refs) == 1:
        refs[0][...] = val
        return
    i = pl.program_id(0)

    @pl.when(i < N_CTX_TILES)
    def _():
        refs[0][...] = val

    @pl.when(i >= N_CTX_TILES)
    def _():
        refs[1][...] = val


def _dot(a, b):
    return jnp.dot(a, b, preferred_element_type=F32)


def _dot_nt(a, b):
    return lax.dot_general(a, b, (((1,), (1,)), ((), ())), preferred_element_type=F32)


def _rms_mod(x, g, shift, scale):
    ms = jnp.mean(x * x, axis=-1, keepdims=True)
    return x * lax.rsqrt(ms + EPS) * (g * (1.0 + scale)) + shift


def _rope(x, c, s1, s2, half):
    return x * c + pltpu.roll(x, LANES - half, 1) * s1 + pltpu.roll(x, half, 1) * s2


def _lane_lt64(shape):
    return lax.broadcasted_iota(jnp.int32, shape, len(shape) - 1) < 64


def _ones_pattern():
    lane = lax.broadcasted_iota(jnp.int32, (1, B_HEADS * LANES), 1) & (2 * LANES - 1)
    return jnp.logical_and(lane >= 64, lane < 192).astype(F32)


def _mod_kernel(cond_ref, w_ref, b_ref, o_ref):
    c = cond_ref[...]
    s = c * jax.nn.sigmoid(c)
    o_ref[...] = _dot(s.astype(BF16), w_ref[...].astype(BF16)) + b_ref[...]


def _mod_table(cond, w_mod, b_mod):
    nb = 6 * D_MODEL // 1024
    return pl.pallas_call(
        _mod_kernel,
        grid=(DEPTH, nb),
        in_specs=[
            pl.BlockSpec((N_COND, D_MODEL), lambda l, n: (0, 0)),
            pl.BlockSpec((None, D_MODEL, 1024), lambda l, n: (l, 0, n)),
            pl.BlockSpec((None, 1, 1024), lambda l, n: (l, 0, n)),
        ],
        out_specs=pl.BlockSpec((None, N_COND, 1024), lambda l, n: (l, 0, n)),
        out_shape=jax.ShapeDtypeStruct((DEPTH, N_COND, 6 * D_MODEL), F32),
        compiler_params=_cparams(("arbitrary", "arbitrary")),
        name="mod_table",
    )(cond, w_mod, b_mod.reshape(DEPTH, 1, 6 * D_MODEL))


def _attn_in_kernel(*refs, n_x):
    x_refs = refs[:n_x]
    (mod_ref, g_ref, w_ref, qn_ref, kvn_ref, wqb_ref, wkk_ref, wkv_ref, rope_ref,
     qa_ref, kd_ref, vd_ref, qm_ref, km_ref, vm_ref, ka32_ref, va32_ref, ckv32_ref, kr32_ref) = refs[n_x:]
    i = pl.program_id(0)
    h = _rms_mod(_tok_load(x_refs), g_ref[...], mod_ref[0:1, :], mod_ref[1:2, :])
    y = _dot(h.astype(BF16), w_ref[...])

    ca, s1a, s2a = rope_ref[0], rope_ref[1], rope_ref[2]
    cb, s1b, s2b = rope_ref[3], rope_ref[4], rope_ref[5]
    lt64 = _lane_lt64((TILE, LANES))

    for c in range(4):
        q = _rope(y[:, c * LANES:(c + 1) * LANES], ca, s1a, s2a, HEAD_DIM // 2)
        qa_ref[:, c * LANES:(c + 1) * LANES] = (q * (HEAD_DIM ** -0.5 * LOG2E)).astype(BF16)

    ka = _rope(y[:, 512:640], ca, s1a, s2a, HEAD_DIM // 2)
    va = y[:, 640:768]
    ka_sw = pltpu.roll(ka, 64, 1)
    kd_ref[:, 0:LANES] = jnp.where(lt64, ka, ka_sw).astype(BF16)
    kd_ref[:, LANES:2 * LANES] = jnp.where(lt64, ka_sw, ka).astype(BF16)
    vd_ref[:, 0:LANES] = jnp.where(lt64, va, 1.0).astype(BF16)
    vd_ref[:, LANES:2 * LANES] = jnp.where(lt64, pltpu.roll(va, 64, 1), 1.0).astype(BF16)

    cq = y[:, 768:768 + Q_LORA_PAD]
    cqn = cq * lax.rsqrt(jnp.sum(cq * cq, axis=-1, keepdims=True) * (1.0 / Q_LORA) + EPS) * qn_ref[...]
    qm = _dot(cqn.astype(BF16), wqb_ref[...])
    for hh in range(B_HEADS):
        q = _rope(qm[:, hh * LANES:(hh + 1) * LANES], cb, s1b, s2b, QK_ROPE // 2)
        qm_ref[:, hh * LANES:(hh + 1) * LANES] = (q * (MLA_SCALE * LOG2E)).astype(BF16)

    ckv = y[:, 1024:1152]
    ckvn = ckv * lax.rsqrt(jnp.mean(ckv * ckv, axis=-1, keepdims=True) + EPS) * kvn_ref[...]
    kr = y[:, 1152:1280]
    ckvn_b = ckvn.astype(BF16)
    kn = _dot(ckvn_b, wkk_ref[...])
    krr = _rope(kr, cb, s1b, s2b, QK_ROPE // 2)
    for hh in range(B_HEADS):
        km_ref[:, hh * LANES:(hh + 1) * LANES] = (kn[:, hh * LANES:(hh + 1) * LANES] + krr).astype(BF16)
    vm_ref[...] = (_dot(ckvn_b, wkv_ref[...]) + _ones_pattern()).astype(BF16)

    @pl.when(i < N_CTX_TILES)
    def _():
        ka32_ref[...] = ka
        va32_ref[...] = va
        ckv32_ref[...] = ckvn
        kr32_ref[...] = kr


def _attn_in(x, mod, g, wp, rope, layer):
    tile_map = lambda i: (i, 0)
    const2 = lambda i: (0, 0)
    ctx_map = lambda i: (jnp.minimum(i, N_CTX_TILES - 1), 0)
    bf = lambda w: jax.ShapeDtypeStruct((T_ALL, w), BF16)
    c32 = jax.ShapeDtypeStruct((T_CTX, LANES), F32)
    x_ops, x_specs = _tok_specs(x, D_MODEL)
    return pl.pallas_call(
        functools.partial(_attn_in_kernel, n_x=len(x_ops)),
        grid=(N_TILES,),
        in_specs=x_specs + [
            pl.BlockSpec((None, None, 6, D_MODEL), lambda i: (layer, _mod_row(i), 0, 0)),
            pl.BlockSpec((1, D_MODEL), const2),
            pl.BlockSpec((D_MODEL, ATTN_IN_COLS), const2),
            pl.BlockSpec((1, Q_LORA_PAD), const2),
            pl.BlockSpec((1, KV_LORA), const2),
            pl.BlockSpec((Q_LORA_PAD, B_HEADS * LANES), const2),
            pl.BlockSpec((KV_LORA, B_HEADS * LANES), const2),
            pl.BlockSpec((KV_LORA, B_HEADS * LANES), const2),
            pl.BlockSpec((6, TILE, LANES), lambda i: (0, _pos_block(i), 0)),
        ],
        out_specs=[
            pl.BlockSpec((TILE, 512), tile_map),
            pl.BlockSpec((TILE, 256), tile_map),
            pl.BlockSpec((TILE, 256), tile_map),
            pl.BlockSpec((TILE, 1024), tile_map),
            pl.BlockSpec((TILE, 1024), tile_map),
            pl.BlockSpec((TILE, 1024), tile_map),
            pl.BlockSpec((TILE, LANES), ctx_map),
            pl.BlockSpec((TILE, LANES), ctx_map),
            pl.BlockSpec((TILE, LANES), ctx_map),
            pl.BlockSpec((TILE, LANES), ctx_map),
        ],
        out_shape=[bf(512), bf(256), bf(256), bf(1024), bf(1024), bf(1024), c32, c32, c32, c32],
        compiler_params=_cparams(("arbitrary",)),
        name="attn_in",
    )(*x_ops, mod, g, wp["w_in"], wp["q_norm"], wp["kv_norm"], wp["w_qb"], wp["w_kk"], wp["w_kv"], rope)


def _ctx_kv_kernel(ck_ref, cv_ref, cckv_ref, ckr_ref, wkk_ref, wkv_ref, kd_ref, vd_ref, km_ref, vm_ref):
    lt64 = _lane_lt64((PAST_LEN, LANES))
    k = ck_ref[...]
    v = cv_ref[...]
    k_sw = pltpu.roll(k, 64, 1)
    kd_ref[:, 0:LANES] = jnp.where(lt64, k, k_sw).astype(BF16)
    kd_ref[:, LANES:2 * LANES] = jnp.where(lt64, k_sw, k).astype(BF16)
    vd_ref[:, 0:LANES] = jnp.where(lt64, v, 1.0).astype(BF16)
    vd_ref[:, LANES:2 * LANES] = jnp.where(lt64, pltpu.roll(v, 64, 1), 1.0).astype(BF16)
    cb = cckv_ref[...].astype(BF16)
    kn = _dot(cb, wkk_ref[...])
    kr = ckr_ref[...]
    for hh in range(B_HEADS):
        km_ref[:, hh * LANES:(hh + 1) * LANES] = (kn[:, hh * LANES:(hh + 1) * LANES] + kr).astype(BF16)
    vm_ref[...] = (_dot(cb, wkv_ref[...]) + _ones_pattern()).astype(BF16)


def _ctx_kv(cache_k, cache_v, cache_ckv, cache_kr128, w_kk, w_kv):
    cache_map = lambda i, b: (b, i, 0, 0)
    w_map = lambda i, b: (i, 0, 0)
    out_map = lambda i, b: (i, b, 0, 0)
    o = lambda w: jax.ShapeDtypeStruct((N_EVEN, DEC_BATCH, PAST_LEN, w), BF16)
    return pl.pallas_call(
        _ctx_kv_kernel,
        grid=(N_EVEN, DEC_BATCH),
        in_specs=[
            pl.BlockSpec((None, None, PAST_LEN, LANES), cache_map),
            pl.BlockSpec((None, None, PAST_LEN, LANES), cache_map),
            pl.BlockSpec((None, None, PAST_LEN, KV_LORA), cache_map),
            pl.BlockSpec((None, None, PAST_LEN, LANES), cache_map),
            pl.BlockSpec((None, KV_LORA, B_HEADS * LANES), w_map),
            pl.BlockSpec((None, KV_LORA, B_HEADS * LANES), w_map),
        ],
        out_specs=[
            pl.BlockSpec((None, None, PAST_LEN, 256), out_map),
            pl.BlockSpec((None, None, PAST_LEN, 256), out_map),
            pl.BlockSpec((None, None, PAST_LEN, 1024), out_map),
            pl.BlockSpec((None, None, PAST_LEN, 1024), out_map),
        ],
        out_shape=[o(256), o(256), o(1024), o(1024)],
        compiler_params=_cparams(("arbitrary", "arbitrary")),
        name="ctx_kv",
    )(cache_k, cache_v, cache_ckv, cache_kr128, w_kk, w_kv)


def _head_scores(q, ks, valids, sink):
    ss = []
    for k, valid in zip(ks, valids):
        s = _dot_nt(q, k)
        if valid is not None:
            s = jnp.where(valid, s, NEG_INF)
        ss.append(s)
    m = ss[0].max(axis=-1, keepdims=True)
    for s in ss[1:]:
        m = jnp.maximum(m, s.max(axis=-1, keepdims=True))
    if sink is not None:
        m = jnp.maximum(m, sink)
    return ss, m


def _head_pv(ss, m, vs):
    o = None
    for s, v in zip(ss, vs):
        os_ = _dot(jnp.exp2(s - m).astype(BF16), v)
        o = os_ if o is None else o + os_
    return o


def _attn_heads(qa_ref, qm_ref, a_segs, m_segs, sink_ref, o_ref, rows, depth):
    lt64 = _lane_lt64((rows, LANES))
    lane = lax.broadcasted_iota(jnp.int32, (1, LANES), 1)
    keep = ((lane < 64).astype(BF16), (lane >= 64).astype(BF16))
    col = lambda n: slice(n * LANES, (n + 1) * LANES)
    swap = lambda r: pltpu.roll(r, 64, 1)

    jobs = []
    for hh in range(A_HEADS):
        c, half, kh = hh // 2, hh % 2, hh // 4
        jobs.append(dict(
            out=c, half=half,
            q=lambda c=c, half=half: qa_ref[:, col(c)] * keep[half],
            ks=[lambda kd=kd, rs=rs, kh=kh: kd[rs, col(kh)] for kd, _, rs, _ in a_segs],
            vs=[lambda vd=vd, rs=rs, kh=kh: vd[rs, col(kh)] for _, vd, rs, _ in a_segs],
            valids=[valid for _, _, _, valid in a_segs],
            sink=lambda hh=hh: sink_ref[hh]))
    for hh in range(B_HEADS):
        jobs.append(dict(
            out=4 + hh // 2, half=hh % 2,
            q=lambda hh=hh: qm_ref[:, col(hh)],
            ks=[lambda km=km, hh=hh: km[:, col(hh)] for km, _ in m_segs],
            vs=[lambda vm=vm, hh=hh: vm[:, col(hh)] for _, vm in m_segs],
            valids=[None for _ in m_segs],
            sink=None))

    def scores(job):
        sink = None if job["sink"] is None else job["sink"]()
        job["ss"], job["m"] = _head_scores(job["q"](), [k() for k in job["ks"]], job["valids"], sink)
        job["e"] = None if sink is None else jnp.exp2(sink - job["m"])

    def finish(job, done):
        r = _head_pv(job.pop("ss"), job.pop("m"), [v() for v in job["vs"]])
        if job["sink"] is not None and job["half"] == 1:
            o = swap(r) / (r + job["e"])
        else:
            o = r / (swap(r) if job["e"] is None else swap(r) + job["e"])
        other = done.pop(job["out"], None)
        if other is None:
            done[job["out"]] = o
        else:
            lo, hi = (other, o) if job["half"] == 1 else (o, other)
            o_ref[:, col(job["out"])] = jnp.where(lt64, lo, hi).astype(BF16)

    done = {}
    for t in range(len(jobs) + depth):
        if t < len(jobs):
            scores(jobs[t])
        if t >= depth:
            finish(jobs[t - depth], done)


def _attn_lat_kernel(sink_ref, qa_ref, qm_ref, kd_ref, vd_ref, km_ref, vm_ref,
                     kdc_ref, vdc_ref, kmc_ref, vmc_ref, o_ref):
    j = pl.program_id(1)
    q0 = j * BQ
    nloc = BQ + 2 * WINDOW
    start = pl.multiple_of(jnp.clip(q0 - WINDOW, 0, DEC_SEQ - nloc), WINDOW)
    qpos = q0 + lax.broadcasted_iota(jnp.int32, (BQ, nloc), 0)
    kpos = start + lax.broadcasted_iota(jnp.int32, (BQ, nloc), 1)
    valid = jnp.abs(qpos - kpos) <= WINDOW
    everything = slice(None)
    a_segs = [(kd_ref, vd_ref, pl.ds(start, nloc), valid), (kdc_ref, vdc_ref, everything, None)]
    m_segs = [(km_ref, vm_ref), (kmc_ref, vmc_ref)]
    _attn_heads(qa_ref, qm_ref, a_segs, m_segs, sink_ref, o_ref, BQ, depth=2)


def _attn_ctx_kernel(sink_ref, qa_ref, qm_ref, kd_ref, vd_ref, km_ref, vm_ref, o_ref):
    a_segs = [(kd_ref, vd_ref, slice(None), None)]
    m_segs = [(km_ref, vm_ref)]
    _attn_heads(qa_ref, qm_ref, a_segs, m_segs, sink_ref, o_ref, SEQ, depth=1)


def _attention(sink2, qa, kd, vd, qm, km, vm, kdc, vdc, kmc, vmc, layer_i):
    smem = pl.BlockSpec(memory_space=pltpu.SMEM)
    nq = DEC_SEQ // BQ
    q_off = T_CTX // BQ
    s_off = T_CTX // DEC_SEQ
    q_map = lambda b, j: (q_off + b * nq + j, 0)
    kv_map = lambda b, j: (s_off + b, 0)
    c_map = lambda b, j: (layer_i, b, 0, 0)
    o_lat = pl.pallas_call(
        _attn_lat_kernel,
        grid=(DEC_BATCH, nq),
        in_specs=[
            smem,
            pl.BlockSpec((BQ, 512), q_map),
            pl.BlockSpec((BQ, 1024), q_map),
            pl.BlockSpec((DEC_SEQ, 256), kv_map),
            pl.BlockSpec((DEC_SEQ, 256), kv_map),
            pl.BlockSpec((DEC_SEQ, 1024), kv_map),
            pl.BlockSpec((DEC_SEQ, 1024), kv_map),
            pl.BlockSpec((None, None, PAST_LEN, 256), c_map),
            pl.BlockSpec((None, None, PAST_LEN, 256), c_map),
            pl.BlockSpec((None, None, PAST_LEN, 1024), c_map),
            pl.BlockSpec((None, None, PAST_LEN, 1024), c_map),
        ],
        out_specs=pl.BlockSpec((BQ, 1024), lambda b, j: (b * nq + j, 0)),
        out_shape=jax.ShapeDtypeStruct((T_LAT, 1024), BF16),
        compiler_params=_cparams(("arbitrary", "arbitrary")),
        name="attn_latent",
    )(sink2, qa, qm, kd, vd, km, vm, kdc, vdc, kmc, vmc)
    b_map = lambda b: (b, 0)
    o_ctx = pl.pallas_call(
        _attn_ctx_kernel,
        grid=(BATCH,),
        in_specs=[
            smem,
            pl.BlockSpec((SEQ, 512), b_map),
            pl.BlockSpec((SEQ, 1024), b_map),
            pl.BlockSpec((SEQ, 256), b_map),
            pl.BlockSpec((SEQ, 256), b_map),
            pl.BlockSpec((SEQ, 1024), b_map),
            pl.BlockSpec((SEQ, 1024), b_map),
        ],
        out_specs=pl.BlockSpec((SEQ, 1024), b_map),
        out_shape=jax.ShapeDtypeStruct((T_CTX, 1024), BF16),
        compiler_params=_cparams(("arbitrary",)),
        name="attn_context",
    )(sink2, qa, qm, kd, vd, km, vm)
    return (o_ctx, o_lat)


def _conv_in_kernel(x_ref, mod_ref, g_ref, w_ref, u_ref, z_ref):
    h = _rms_mod(x_ref[...], g_ref[...], mod_ref[0:1, :], mod_ref[1:2, :])
    y = _dot(h.astype(BF16), w_ref[...])
    a = y[:, 0:CONV_CH]
    gate = y[:, CONV_CH:2 * CONV_CH]
    u_ref[...] = a * jax.nn.sigmoid(gate)
    z_ref[...] = y[:, 2 * CONV_CH:]


def _conv_in(x, mod, g, w_in, layer):
    tile_map = lambda i: (i, 0)
    const2 = lambda i: (0, 0)
    return pl.pallas_call(
        _conv_in_kernel,
        grid=(N_TILES,),
        in_specs=[
            pl.BlockSpec((TILE, D_MODEL), tile_map),
            pl.BlockSpec((None, None, 6, D_MODEL), lambda i: (layer, _mod_row(i), 0, 0)),
            pl.BlockSpec((1, D_MODEL), const2),
            pl.BlockSpec((D_MODEL, 3 * CONV_CH), const2),
        ],
        out_specs=[pl.BlockSpec((TILE, CONV_CH), tile_map), pl.BlockSpec((TILE, POOL_CH), tile_map)],
        out_shape=[jax.ShapeDtypeStruct((T_ALL, CONV_CH), F32), jax.ShapeDtypeStruct((T_ALL, POOL_CH), F32)],
        compiler_params=_cparams(("arbitrary",)),
        name="conv_in",
    )(x, mod, g, w_in)


CONV_ROWS = 32


def _conv_mix_kernel(u_ref, up_ref, un_ref, z_ref, zp_ref, zn_ref, wdw_ref, bdw_ref, lng_ref, lnb_ref,
                     wg_ref, ps_ref, o_ref, ue_ref, ze_ref):
    i = pl.program_id(0)
    j = (i - N_CTX_TILES) % LAT_TILES_PER_SEQ
    is_lat = i >= N_CTX_TILES
    has_prev = jnp.logical_and(is_lat, j > 0)
    has_next = jnp.logical_and(is_lat, j < LAT_TILES_PER_SEQ - 1)
    seq_len = jnp.where(is_lat, DEC_SEQ, SEQ)
    t0 = jnp.where(is_lat, j * TILE, 0)

    ue_ref[0, 0:HALO, :] = jnp.where(has_prev, up_ref[...], 0.0)
    ue_ref[0, HALO:HALO + TILE, :] = u_ref[...]
    ue_ref[0, HALO + TILE:, :] = jnp.where(has_next, un_ref[...], 0.0)
    ze_ref[0:HALO, :] = jnp.where(has_prev, zp_ref[...], 0.0)
    ze_ref[HALO:HALO + TILE, :] = z_ref[...]
    ze_ref[HALO + TILE:, :] = jnp.where(has_next, zn_ref[...], 0.0)

    pad = CONV_WIDTH // 2
    n_sh = HALO - pad + CONV_WIDTH - 1 - (SUBLANES - 1) + TILE
    for b in range(1, SUBLANES):
        ue_ref[b, 0:n_sh, :] = ue_ref[0, b:b + n_sh, :]

    for r in range(TILE // CONV_ROWS):
        r0 = r * CONV_ROWS
        acc = jnp.zeros((CONV_ROWS, CONV_CH), F32) + bdw_ref[...]
        for k in range(CONV_WIDTH):
            a, b = divmod(HALO + k - pad, SUBLANES)
            e0 = r0 + a * SUBLANES
            acc = acc + ue_ref[b, e0:e0 + CONV_ROWS, :] * wdw_ref[k:k + 1, :]
        mu = jnp.mean(acc, axis=-1, keepdims=True)
        d = acc - mu
        var = jnp.mean(d * d, axis=-1, keepdims=True)
        yn = d * lax.rsqrt(var + EPS) * lng_ref[...] + lnb_ref[...]
        o_ref[r0:r0 + CONV_ROWS, 0:CONV_CH] = (yn * jax.nn.sigmoid(yn)).astype(BF16)

    t = t0 + lax.broadcasted_iota(jnp.int32, (TILE, POOL_GROUP_W), 0)
    for gi, w in enumerate(POOL_SIZES):
        lo = w // 2
        hi = w - lo - 1
        cols = slice(gi * POOL_GROUP_W, (gi + 1) * POOL_GROUP_W)
        tot = ze_ref[HALO - lo:HALO - lo + TILE, cols]
        for s in range(-lo + 1, hi + 1):
            tot = tot + ze_ref[HALO + s:HALO + s + TILE, cols]
        cnt = jnp.minimum(t + hi + 1, seq_len) - jnp.maximum(t - lo, 0)
        d = tot / cnt.astype(F32) - ze_ref[HALO:HALO + TILE, cols]
        y = _dot(d.astype(BF16), wg_ref[gi]) * ps_ref[:, cols]
        o_ref[:, CONV_CH + gi * POOL_GROUP_W:CONV_CH + (gi + 1) * POOL_GROUP_W] = y.astype(BF16)


def _conv_mix(u, z, cp):
    tile_map = lambda i: (i, 0)
    const2 = lambda i: (0, 0)
    hb = TILE // HALO
    prev_map = lambda i: (jnp.maximum(i * hb - 1, 0), 0)
    next_map = lambda i: (jnp.minimum((i + 1) * hb, T_ALL // HALO - 1), 0)
    return pl.pallas_call(
        _conv_mix_kernel,
        grid=(N_TILES,),
        in_specs=[
            pl.BlockSpec((TILE, CONV_CH), tile_map),
            pl.BlockSpec((HALO, CONV_CH), prev_map),
            pl.BlockSpec((HALO, CONV_CH), next_map),
            pl.BlockSpec((TILE, POOL_CH), tile_map),
            pl.BlockSpec((HALO, POOL_CH), prev_map),
            pl.BlockSpec((HALO, POOL_CH), next_map),
            pl.BlockSpec((CONV_WIDTH, CONV_CH), const2),
            pl.BlockSpec((1, CONV_CH), const2),
            pl.BlockSpec((1, CONV_CH), const2),
            pl.BlockSpec((1, CONV_CH), const2),
            pl.BlockSpec((len(POOL_SIZES), POOL_GROUP_W, POOL_GROUP_W), lambda i: (0, 0, 0)),
            pl.BlockSpec((1, POOL_CH), const2),
        ],
        out_specs=pl.BlockSpec((TILE, D_MODEL), tile_map),
        out_shape=jax.ShapeDtypeStruct((T_ALL, D_MODEL), BF16),
        scratch_shapes=[pltpu.VMEM((SUBLANES, TILE + 2 * HALO, CONV_CH), F32),
                        pltpu.VMEM((TILE + 2 * HALO, POOL_CH), F32)],
        compiler_params=_cparams(("arbitrary",)),
        name="conv_mix",
    )(u, u, u, z, z, z, cp["w_dw"], cp["b_dw"], cp["ln_g"], cp["ln_b"], cp["w_grp"], cp["p_scale"])


def _post_kernel(*refs, n_o, n_x, final):
    o_refs, x_refs = refs[:n_o], refs[n_o:n_o + n_x]
    mod_ref, g_ref, wo_ref, w1_ref, w2_ref, fg_ref = refs[n_o + n_x:n_o + n_x + 6]
    y_refs = refs[n_o + n_x + 6:]
    x1 = _tok_load(x_refs) + mod_ref[2:3, :] * _dot(_tok_load(o_refs), wo_ref[...])
    h = _rms_mod(x1, g_ref[...], mod_ref[3:4, :], mod_ref[4:5, :])
    a = jnp.maximum(_dot(h.astype(BF16), w1_ref[...]), 0.0)
    x2 = x1 + mod_ref[5:6, :] * _dot((a * a).astype(BF16), w2_ref[...])
    if final:
        ms = jnp.mean(x2 * x2, axis=-1, keepdims=True)
        x2 = x2 * lax.rsqrt(ms + EPS) * fg_ref[...]
    _tok_store(y_refs, x2)


def _post(o, x, mod, g, w_out, w1, w2, final_g, layer, final):
    const2 = lambda i: (0, 0)
    o_ops, o_specs = _tok_specs(o, D_MODEL)
    x_ops, x_specs = _tok_specs(x, D_MODEL)
    if final:
        y = (jax.ShapeDtypeStruct((T_CTX, D_MODEL), F32), jax.ShapeDtypeStruct((T_LAT, D_MODEL), F32))
    else:
        y = jax.ShapeDtypeStruct((T_ALL, D_MODEL), F32)
    _, y_specs = _tok_specs(y, D_MODEL)
    out = pl.pallas_call(
        functools.partial(_post_kernel, n_o=len(o_ops), n_x=len(x_ops), final=final),
        grid=(N_TILES,),
        in_specs=o_specs + x_specs + [
            pl.BlockSpec((None, None, 6, D_MODEL), lambda i: (layer, _mod_row(i), 0, 0)),
            pl.BlockSpec((1, D_MODEL), const2),
            pl.BlockSpec((D_MODEL, D_MODEL), const2, pipeline_mode=pl.Buffered(1)),
            pl.BlockSpec((D_MODEL, D_FF), const2, pipeline_mode=pl.Buffered(1)),
            pl.BlockSpec((D_FF, D_MODEL), const2, pipeline_mode=pl.Buffered(1)),
            pl.BlockSpec((1, D_MODEL), const2),
        ],
        out_specs=y_specs if final else y_specs[0],
        out_shape=y,
        compiler_params=_cparams(("arbitrary",)),
        name="post_final" if final else "post",
    )(*o_ops, *x_ops, mod, g, w_out, w1, w2, final_g)
    return out


def _rope_tables():
    n = DEC_SEQ
    rows = n // GRID_W
    row = jnp.repeat(jnp.arange(rows), GRID_W).astype(F32)
    col = jnp.tile(jnp.arange(GRID_W), rows).astype(F32)

    def angles(dim):
        quarter = dim // 4
        inv_freq = ROPE_BASE ** (-jnp.arange(quarter, dtype=F32) / quarter)
        return jnp.concatenate([row[:, None] * inv_freq, col[:, None] * inv_freq], axis=-1)

    ang_a = angles(HEAD_DIM)
    cos_a, sin_a = jnp.cos(ang_a), jnp.sin(ang_a)
    z32 = jnp.zeros((n, 32), F32)
    c_a = jnp.concatenate([cos_a] * 4, axis=-1)
    s1_a = jnp.concatenate([-sin_a, z32] * 2, axis=-1)
    s2_a = jnp.concatenate([z32, sin_a] * 2, axis=-1)
    ang_b = angles(QK_ROPE)
    cos_b, sin_b = jnp.cos(ang_b), jnp.sin(ang_b)
    one64, z64 = jnp.ones((n, 64), F32), jnp.zeros((n, 64), F32)
    one32 = jnp.ones((n, 32), F32)
    z16 = jnp.zeros((n, 16), F32)
    c_b = jnp.concatenate([one64, cos_b, cos_b, one32], axis=-1)
    s1_b = jnp.concatenate([z64, -sin_b, z16, z32], axis=-1)
    s2_b = jnp.concatenate([z64, z16, sin_b, z32], axis=-1)
    lat = jnp.stack([c_a, s1_a, s2_a, c_b, s1_b, s2_b])
    ident = jnp.stack([jnp.ones((TILE, LANES), F32), jnp.zeros((TILE, LANES), F32),
                       jnp.zeros((TILE, LANES), F32)] * 2)
    return jnp.concatenate([ident, lat], axis=1)


def _prep_even(attn_w_in, mla_q_norm, mla_kv_norm, mla_w_qb, mla_w_kvb, attn_w_out, i):
    w = attn_w_in[i]
    o = np.cumsum((0, 512, 128, 128, Q_LORA, KV_LORA, QK_ROPE))
    zeros = lambda n: jnp.zeros((D_MODEL, n), F32)
    w_in = jnp.concatenate([
        w[:, o[0]:o[3]],
        w[:, o[3]:o[4]], zeros(Q_LORA_PAD - Q_LORA),
        w[:, o[4]:o[5]],
        zeros(64), w[:, o[5]:o[6]], zeros(32),
    ], axis=1).astype(BF16)
    q_norm = jnp.pad(mla_q_norm[i], (0, Q_LORA_PAD - Q_LORA)).reshape(1, Q_LORA_PAD)
    wqb = mla_w_qb[i].reshape(Q_LORA, B_HEADS, QK_NOPE + QK_ROPE)
    wqb = jnp.pad(wqb, ((0, Q_LORA_PAD - Q_LORA), (0, 0), (0, LANES - QK_NOPE - QK_ROPE)))
    wkvb = mla_w_kvb[i].reshape(KV_LORA, B_HEADS, QK_NOPE + V_DIM)
    w_kk = jnp.pad(wkvb[:, :, :QK_NOPE], ((0, 0), (0, 0), (0, LANES - QK_NOPE)))
    w_v = wkvb[:, :, QK_NOPE:].reshape(KV_LORA, B_HEADS // 2, 2, V_DIM)
    z_v = jnp.zeros((KV_LORA, B_HEADS // 2, 2 * V_DIM), F32)
    w_kv = jnp.concatenate([w_v[:, :, 0], z_v, w_v[:, :, 1]], axis=-1)
    return {
        "w_in": w_in,
        "q_norm": q_norm,
        "kv_norm": mla_kv_norm[i].reshape(1, KV_LORA),
        "w_qb": wqb.reshape(Q_LORA_PAD, B_HEADS * LANES).astype(BF16),
        "w_kk": w_kk.reshape(KV_LORA, B_HEADS * LANES).astype(BF16),
        "w_kv": w_kv.reshape(KV_LORA, B_HEADS * LANES).astype(BF16),
        "w_out": attn_w_out[i].astype(BF16),
    }


def kernel(x_prompt, x_sample, cache_win_k, cache_win_v, cache_mla_ckv, cache_mla_krope, c, c_ctx, w_mod, b_mod,
           norm_g, attn_w_in, attn_sink, mla_q_norm, mla_kv_norm, mla_w_qb, mla_w_kvb, attn_w_out, conv_w_in,
           conv_dw, conv_dw_b, conv_ln_g, conv_ln_b, pool_w, pool_scale, conv_w_out, mlp_w1, mlp_w2, final_g):
    x = (x_prompt.reshape(T_CTX, D_MODEL), x_sample.reshape(T_LAT, D_MODEL))

    cond = jnp.concatenate([c_ctx[None, :], c, jnp.zeros((N_COND - 1 - DEC_BATCH, D_MODEL), F32)], axis=0)
    mod = _mod_table(cond, w_mod, b_mod).reshape(DEPTH, N_COND, 6, D_MODEL)

    rope = _rope_tables()
    even = [_prep_even(attn_w_in, mla_q_norm, mla_kv_norm, mla_w_qb, mla_w_kvb, attn_w_out, i)
            for i in range(N_EVEN)]
    cache_kr128 = jnp.pad(cache_mla_krope, ((0, 0), (0, 0), (0, 0), (64, 32)))
    kdc, vdc, kmc, vmc = _ctx_kv(
        cache_win_k.reshape(DEC_BATCH, N_EVEN, PAST_LEN, LANES),
        cache_win_v.reshape(DEC_BATCH, N_EVEN, PAST_LEN, LANES),
        cache_mla_ckv, cache_kr128,
        jnp.stack([e["w_kk"] for e in even]), jnp.stack([e["w_kv"] for e in even]))
    final_g2 = final_g.reshape(1, D_MODEL)

    ks, vs, ckvs, krs = [], [], [], []
    for l in range(DEPTH):
        g_mix = norm_g[l, 0].reshape(1, D_MODEL)
        g_mlp = norm_g[l, 1].reshape(1, D_MODEL)
        if l % 2 == 0:
            i = l // 2
            wp = even[i]
            qa, kd, vd, qm, km, vm, ka32, va32, ckv32, kr32 = _attn_in(x, mod, g_mix, wp, rope, l)
            ks.append(ka32.reshape(BATCH, SEQ, A_KV_HEADS, HEAD_DIM))
            vs.append(va32.reshape(BATCH, SEQ, A_KV_HEADS, HEAD_DIM))
            ckvs.append(ckv32.reshape(BATCH, SEQ, KV_LORA))
            krs.append(kr32[:, 64:64 + QK_ROPE].reshape(BATCH, SEQ, QK_ROPE))
            o = _attention(attn_sink[i] * LOG2E, qa, kd, vd, qm, km, vm, kdc, vdc, kmc, vmc, i)
            w_out = wp["w_out"]
        else:
            jj = l // 2
            u, z = _conv_in(x, mod, g_mix, conv_w_in[jj].astype(BF16), l)
            cp = {
                "w_dw": conv_dw[jj],
                "b_dw": conv_dw_b[jj].reshape(1, CONV_CH),
                "ln_g": conv_ln_g[jj].reshape(1, CONV_CH),
                "ln_b": conv_ln_b[jj].reshape(1, CONV_CH),
                "w_grp": pool_w[jj].astype(BF16),
                "p_scale": pool_scale[jj].reshape(1, POOL_CH),
            }
            o = _conv_mix(u, z, cp)
            w_out = conv_w_out[jj].astype(BF16)
        x = _post(o, x, mod, g_mlp, w_out, mlp_w1[l].astype(BF16), mlp_w2[l].astype(BF16), final_g2, l,
                  final=(l == DEPTH - 1))

    y_prompt = x[0].reshape(BATCH, SEQ, D_MODEL)
    y_sample = x[1].reshape(DEC_BATCH, DEC_SEQ, D_MODEL)
    return (y_prompt, y_sample, jnp.stack(ks, axis=1), jnp.stack(vs, axis=1),
            jnp.stack(ckvs, axis=1), jnp.stack(krs, axis=1))
```

```python
import functools
import math

import numpy as np
import jax
import jax.numpy as jnp
from jax import lax
from jax.experimental import pallas as pl
from jax.experimental.pallas import tpu as pltpu

F32 = jnp.float32
BF16 = jnp.bfloat16

D_MODEL = 1024
BATCH = 16
SEQ = 256
DEPTH = 4
DEC_BATCH = 8
DEC_SEQ = 2048
PAST_LEN = 256
GRID_W = 64
N_EVEN = (DEPTH + 1) // 2
N_ODD = DEPTH // 2
A_HEADS = 8
A_KV_HEADS = 2
HEAD_DIM = 64
WINDOW = 128
B_HEADS = 8
Q_LORA = 192
KV_LORA = 128
QK_NOPE = 64
QK_ROPE = 32
V_DIM = 64
MLA_SCALE = (QK_NOPE + QK_ROPE) ** -0.5
CONV_CH = D_MODEL // 2
CONV_WIDTH = 31
POOL_CH = D_MODEL // 2
POOL_SIZES = (2, 4, 8, 16)
POOL_GROUP_W = POOL_CH // len(POOL_SIZES)
D_FF = 4 * D_MODEL
ROPE_BASE = 10000.0
EPS = 1e-6
NEG_INF = -1e30
LOG2E = math.log2(math.e)

LANES = 128
SUBLANES = 8
VMEM_LIMIT_BYTES = 56 * 1024 * 1024

TILE = 256
T_CTX = BATCH * SEQ
T_LAT = DEC_BATCH * DEC_SEQ
T_ALL = T_CTX + T_LAT
N_CTX_TILES = T_CTX // TILE
N_TILES = T_ALL // TILE
LAT_TILES_PER_SEQ = DEC_SEQ // TILE
TOK = 512
N_CTX_TOK = T_CTX // TOK
N_TOK = T_ALL // TOK
LAT_TOK_PER_SEQ = DEC_SEQ // TOK
N_COND = 16
BQ = 256
HALO = 16
Q_LORA_PAD = 256
ATTN_IN_COLS = 512 + 128 + 128 + Q_LORA_PAD + 128 + 128


def _cparams(sem):
    return pltpu.CompilerParams(dimension_semantics=sem, vmem_limit_bytes=VMEM_LIMIT_BYTES)


def _mod_row(i):
    return jnp.where(i < N_CTX_TOK, 0, 1 + (i - N_CTX_TOK) // LAT_TOK_PER_SEQ)


def _pos_block(i):
    return jnp.where(i < N_CTX_TOK, 0, 1 + (i - N_CTX_TOK) % LAT_TOK_PER_SEQ)


def _tok_specs(a, width):
    if isinstance(a, tuple):
        return list(a), [pl.BlockSpec((TOK, width), lambda i: (jnp.minimum(i, N_CTX_TOK - 1), 0)),
                         pl.BlockSpec((TOK, width), lambda i: (jnp.maximum(i - N_CTX_TOK, 0), 0))]
    return [a], [pl.BlockSpec((TOK, width), lambda i: (i, 0))]


def _tok_load(refs, rows):
    if len(refs) == 1:
        return refs[0][rows, :]
    return jnp.where(pl.program_id(0) < N_CTX_TOK, refs[0][rows, :], refs[1][rows, :])


def _tok_store(refs, parts):
    def put(ref):
        for rows, val in parts:
            ref[rows, :] = val

    if len(refs) == 1:
        put(refs[0])
        return
    i = pl.program_id(0)

    @pl.when(i < N_CTX_TOK)
    def _():
        put(refs[0])

    @pl.when(i >= N_CTX_TOK)
    def _():
        put(refs[1])


def _sub_rows():
    return [slice(s * TILE, (s + 1) * TILE) for s in range(TOK // TILE)]


def _dot(a, b):
    return jnp.dot(a, b, preferred_element_type=F32)


def _dot_nt(a, b):
    return lax.dot_general(a, b, (((1,), (1,)), ((), ())), preferred_element_type=F32)


def _rms_mod(x, g, shift, scale):
    ms = jnp.mean(x * x, axis=-1, keepdims=True)
    return x * lax.rsqrt(ms + EPS) * (g * (1.0 + scale)) + shift


def _rope(x, c, s):
    return x * c + pltpu.roll(x, 64, 1) * s


def _lane_lt64(shape):
    return lax.broadcasted_iota(jnp.int32, shape, len(shape) - 1) < 64


def _lane_even32(shape):
    return (lax.broadcasted_iota(jnp.int32, shape, len(shape) - 1) & 63) < 32


def _store_kv_dup(kd_ref, vd_ref, rows, k, v):
    even32 = _lane_even32(k.shape)
    lt64 = _lane_lt64(v.shape)
    kd_ref[rows, 0:LANES] = jnp.where(even32, k, pltpu.roll(k, 32, 1)).astype(BF16)
    kd_ref[rows, LANES:2 * LANES] = jnp.where(even32, pltpu.roll(k, LANES - 32, 1), k).astype(BF16)
    vd_ref[rows, 0:LANES] = jnp.where(lt64, v, 1.0).astype(BF16)
    vd_ref[rows, LANES:2 * LANES] = jnp.where(lt64, pltpu.roll(v, 64, 1), 1.0).astype(BF16)


def _ones_pattern():
    lane = lax.broadcasted_iota(jnp.int32, (1, B_HEADS * LANES), 1) & (2 * LANES - 1)
    return jnp.logical_and(lane >= 64, lane < 192).astype(F32)


def _mod_kernel(cond_ref, w_ref, b_ref, o_ref):
    c = cond_ref[...]
    s = c * jax.nn.sigmoid(c)
    o_ref[...] = _dot(s.astype(BF16), w_ref[...].astype(BF16)) + b_ref[...]


def _mod_table(cond, w_mod, b_mod):
    nb = 6 * D_MODEL // 1024
    return pl.pallas_call(
        _mod_kernel,
        grid=(DEPTH, nb),
        in_specs=[
            pl.BlockSpec((N_COND, D_MODEL), lambda l, n: (0, 0)),
            pl.BlockSpec((None, D_MODEL, 1024), lambda l, n: (l, 0, n)),
            pl.BlockSpec((None, 1, 1024), lambda l, n: (l, 0, n)),
        ],
        out_specs=pl.BlockSpec((None, N_COND, 1024), lambda l, n: (l, 0, n)),
        out_shape=jax.ShapeDtypeStruct((DEPTH, N_COND, 6 * D_MODEL), F32),
        compiler_params=_cparams(("arbitrary", "arbitrary")),
        name="mod_table",
    )(cond, w_mod, b_mod.reshape(DEPTH, 1, 6 * D_MODEL))


def _attn_in_kernel(*refs, n_x):
    x_refs = refs[:n_x]
    (mod_ref, g_ref, w_ref, qn_ref, kvn_ref, wqb_ref, wkk_ref, wkv_ref, rope_ref,
     qa_ref, kd_ref, vd_ref, qm_ref, km_ref, vm_ref, ka32_ref, va32_ref, ckv32_ref, kr32_ref) = refs[n_x:]
    col = lambda n: slice(n * LANES, (n + 1) * LANES)

    def project(rows):
        h = _rms_mod(_tok_load(x_refs, rows), g_ref[...], mod_ref[0:1, :], mod_ref[1:2, :])
        return _dot(h.astype(BF16), w_ref[...])

    def derive(rows, y):
        ca, sa = rope_ref[0, rows, :], rope_ref[1, rows, :]
        cb, sb = rope_ref[2, rows, :], rope_ref[3, rows, :]

        for c in range(4):
            q = _rope(y[:, col(c)], ca, sa)
            qa_ref[rows, col(c)] = (q * (HEAD_DIM ** -0.5 * LOG2E)).astype(BF16)

        ka = _rope(y[:, 512:640], ca, sa)
        va = y[:, 640:768]
        _store_kv_dup(kd_ref, vd_ref, rows, ka, va)

        cq = y[:, 768:768 + Q_LORA_PAD]
        cqn = cq * lax.rsqrt(jnp.sum(cq * cq, axis=-1, keepdims=True) * (1.0 / Q_LORA) + EPS) * qn_ref[...]
        qm = _dot(cqn.astype(BF16), wqb_ref[...])
        for hh in range(B_HEADS):
            q = _rope(qm[:, col(hh)], cb, sb)
            qm_ref[rows, col(hh)] = (q * (MLA_SCALE * LOG2E)).astype(BF16)

        ckv = y[:, 1024:1152]
        ckvn = ckv * lax.rsqrt(jnp.mean(ckv * ckv, axis=-1, keepdims=True) + EPS) * kvn_ref[...]
        kr = y[:, 1152:1280]
        ckvn_b = ckvn.astype(BF16)
        kn = _dot(ckvn_b, wkk_ref[...])
        krr = _rope(kr, cb, sb)
        for hh in range(B_HEADS):
            km_ref[rows, col(hh)] = (kn[:, col(hh)] + krr).astype(BF16)
        vm_ref[rows, :] = (_dot(ckvn_b, wkv_ref[...]) + _ones_pattern()).astype(BF16)
        return ka, va, ckvn, kr

    subs = _sub_rows()
    cache = []
    y = project(subs[0])
    for s, rows in enumerate(subs):
        y_next = project(subs[s + 1]) if s + 1 < len(subs) else None
        cache.append(derive(rows, y))
        y = y_next

    @pl.when(pl.program_id(0) < N_CTX_TOK)
    def _():
        for rows, (ka, va, ckvn, kr) in zip(subs, cache):
            ka32_ref[rows, :] = ka
            va32_ref[rows, :] = va
            ckv32_ref[rows, :] = ckvn
            kr32_ref[rows, :] = kr


def _attn_in(x, mod, g, wp, rope, layer):
    tile_map = lambda i: (i, 0)
    const2 = lambda i: (0, 0)
    ctx_map = lambda i: (jnp.minimum(i, N_CTX_TOK - 1), 0)
    bf = lambda w: jax.ShapeDtypeStruct((T_ALL, w), BF16)
    c32 = jax.ShapeDtypeStruct((T_CTX, LANES), F32)
    x_ops, x_specs = _tok_specs(x, D_MODEL)
    return pl.pallas_call(
        functools.partial(_attn_in_kernel, n_x=len(x_ops)),
        grid=(N_TOK,),
        in_specs=x_specs + [
            pl.BlockSpec((None, None, 6, D_MODEL), lambda i: (layer, _mod_row(i), 0, 0)),
            pl.BlockSpec((1, D_MODEL), const2),
            pl.BlockSpec((D_MODEL, ATTN_IN_COLS), const2),
            pl.BlockSpec((1, Q_LORA_PAD), const2),
            pl.BlockSpec((1, KV_LORA), const2),
            pl.BlockSpec((Q_LORA_PAD, B_HEADS * LANES), const2),
            pl.BlockSpec((KV_LORA, B_HEADS * LANES), const2),
            pl.BlockSpec((KV_LORA, B_HEADS * LANES), const2),
            pl.BlockSpec((4, TOK, LANES), lambda i: (0, _pos_block(i), 0)),
        ],
        out_specs=[
            pl.BlockSpec((TOK, 512), tile_map),
            pl.BlockSpec((TOK, 256), tile_map),
            pl.BlockSpec((TOK, 256), tile_map),
            pl.BlockSpec((TOK, 1024), tile_map),
            pl.BlockSpec((TOK, 1024), tile_map),
            pl.BlockSpec((TOK, 1024), tile_map),
            pl.BlockSpec((TOK, LANES), ctx_map),
            pl.BlockSpec((TOK, LANES), ctx_map),
            pl.BlockSpec((TOK, LANES), ctx_map),
            pl.BlockSpec((TOK, LANES), ctx_map),
        ],
        out_shape=[bf(512), bf(256), bf(256), bf(1024), bf(1024), bf(1024), c32, c32, c32, c32],
        compiler_params=_cparams(("arbitrary",)),
        name="attn_in",
    )(*x_ops, mod, g, wp["w_in"], wp["q_norm"], wp["kv_norm"], wp["w_qb"], wp["w_kk"], wp["w_kv"], rope)


def _ctx_kv_kernel(ck_ref, cv_ref, cckv_ref, ckr_ref, wkk_ref, wkv_ref, kd_ref, vd_ref, km_ref, vm_ref):
    _store_kv_dup(kd_ref, vd_ref, slice(None), ck_ref[...], cv_ref[...])
    cb = cckv_ref[...].astype(BF16)
    kn = _dot(cb, wkk_ref[...])
    kr = ckr_ref[...]
    for hh in range(B_HEADS):
        km_ref[:, hh * LANES:(hh + 1) * LANES] = (kn[:, hh * LANES:(hh + 1) * LANES] + kr).astype(BF16)
    vm_ref[...] = (_dot(cb, wkv_ref[...]) + _ones_pattern()).astype(BF16)


def _ctx_kv(cache_k, cache_v, cache_ckv, cache_kr128, w_kk, w_kv):
    cache_map = lambda i, b: (b, i, 0, 0)
    w_map = lambda i, b: (i, 0, 0)
    out_map = lambda i, b: (i, b, 0, 0)
    o = lambda w: jax.ShapeDtypeStruct((N_EVEN, DEC_BATCH, PAST_LEN, w), BF16)
    return pl.pallas_call(
        _ctx_kv_kernel,
        grid=(N_EVEN, DEC_BATCH),
        in_specs=[
            pl.BlockSpec((None, None, PAST_LEN, LANES), cache_map),
            pl.BlockSpec((None, None, PAST_LEN, LANES), cache_map),
            pl.BlockSpec((None, None, PAST_LEN, KV_LORA), cache_map),
            pl.BlockSpec((None, None, PAST_LEN, LANES), cache_map),
            pl.BlockSpec((None, KV_LORA, B_HEADS * LANES), w_map),
            pl.BlockSpec((None, KV_LORA, B_HEADS * LANES), w_map),
        ],
        out_specs=[
            pl.BlockSpec((None, None, PAST_LEN, 256), out_map),
            pl.BlockSpec((None, None, PAST_LEN, 256), out_map),
            pl.BlockSpec((None, None, PAST_LEN, 1024), out_map),
            pl.BlockSpec((None, None, PAST_LEN, 1024), out_map),
        ],
        out_shape=[o(256), o(256), o(1024), o(1024)],
        compiler_params=_cparams(("arbitrary", "arbitrary")),
        name="ctx_kv",
    )(cache_k, cache_v, cache_ckv, cache_kr128, w_kk, w_kv)


def _head_scores(q, ks, valids, sink):
    ss = []
    for k, valid in zip(ks, valids):
        s = _dot_nt(q, k)
        if valid is not None:
            s = jnp.where(valid, s, NEG_INF)
        ss.append(s)
    m = ss[0].max(axis=-1, keepdims=True)
    for s in ss[1:]:
        m = jnp.maximum(m, s.max(axis=-1, keepdims=True))
    if sink is not None:
        m = jnp.maximum(m, sink)
    return ss, m


def _head_pv(ss, m, vs):
    o = None
    for s, v in zip(ss, vs):
        os_ = _dot(jnp.exp2(s - m).astype(BF16), v)
        o = os_ if o is None else o + os_
    return o


def _attn_heads(qa_ref, qm_ref, a_segs, m_segs, sink_ref, o_ref, rows, depth):
    lt64 = _lane_lt64((rows, LANES))
    lane = lax.broadcasted_iota(jnp.int32, (1, LANES), 1)
    keep = (((lane & 63) < 32).astype(BF16), ((lane & 63) >= 32).astype(BF16))
    col = lambda n: slice(n * LANES, (n + 1) * LANES)
    swap = lambda r: pltpu.roll(r, 64, 1)

    jobs = []
    for hh in range(A_HEADS):
        c, half, kh = hh // 2, hh % 2, hh // 4
        jobs.append(dict(
            out=c, half=half,
            q=lambda c=c, half=half: qa_ref[:, col(c)] * keep[half],
            ks=[lambda kd=kd, rs=rs, kh=kh: kd[rs, col(kh)] for kd, _, rs, _ in a_segs],
            vs=[lambda vd=vd, rs=rs, kh=kh: vd[rs, col(kh)] for _, vd, rs, _ in a_segs],
            valids=[valid for _, _, _, valid in a_segs],
            sink=lambda hh=hh: sink_ref[hh]))
    for hh in range(B_HEADS):
        jobs.append(dict(
            out=4 + hh // 2, half=hh % 2,
            q=lambda hh=hh: qm_ref[:, col(hh)],
            ks=[lambda km=km, hh=hh: km[:, col(hh)] for km, _ in m_segs],
            vs=[lambda vm=vm, hh=hh: vm[:, col(hh)] for _, vm in m_segs],
            valids=[None for _ in m_segs],
            sink=None))

    def scores(job):
        sink = None if job["sink"] is None else job["sink"]()
        job["ss"], job["m"] = _head_scores(job["q"](), [k() for k in job["ks"]], job["valids"], sink)
        job["e"] = None if sink is None else jnp.exp2(sink - job["m"])

    def values(job):
        job["r"] = _head_pv(job.pop("ss"), job.pop("m"), [v() for v in job["vs"]])

    def finish(job, done):
        r = job.pop("r")
        if job["sink"] is not None and job["half"] == 1:
            o = swap(r) / (r + job["e"])
        else:
            o = r / (swap(r) if job["e"] is None else swap(r) + job["e"])
        other = done.pop(job["out"], None)
        if other is None:
            done[job["out"]] = o
        else:
            lo, hi = (other, o) if job["half"] == 1 else (o, other)
            o_ref[:, col(job["out"])] = jnp.where(lt64, lo, hi).astype(BF16)

    done = {}
    d1, d2 = depth[0], depth[0] + depth[1]
    for t in range(len(jobs) + d2):
        if t < len(jobs):
            scores(jobs[t])
        if d1 <= t < len(jobs) + d1:
            values(jobs[t - d1])
        if t >= d2:
            finish(jobs[t - d2], done)


def _attn_lat_kernel(sink_ref, qa_ref, qm_ref, kd_ref, vd_ref, km_ref, vm_ref,
                     kdc_ref, vdc_ref, kmc_ref, vmc_ref, o_ref):
    j = pl.program_id(1)
    q0 = j * BQ
    nloc = BQ + 2 * WINDOW
    start = pl.multiple_of(jnp.clip(q0 - WINDOW, 0, DEC_SEQ - nloc), WINDOW)
    qpos = q0 + lax.broadcasted_iota(jnp.int32, (BQ, nloc), 0)
    kpos = start + lax.broadcasted_iota(jnp.int32, (BQ, nloc), 1)
    valid = jnp.abs(qpos - kpos) <= WINDOW
    everything = slice(None)
    a_segs = [(kd_ref, vd_ref, pl.ds(start, nloc), valid), (kdc_ref, vdc_ref, everything, None)]
    m_segs = [(km_ref, vm_ref), (kmc_ref, vmc_ref)]
    _attn_heads(qa_ref, qm_ref, a_segs, m_segs, sink_ref, o_ref, BQ, depth=(2, 0))


def _attn_ctx_kernel(sink_ref, qa_ref, qm_ref, kd_ref, vd_ref, km_ref, vm_ref, o_ref):
    a_segs = [(kd_ref, vd_ref, slice(None), None)]
    m_segs = [(km_ref, vm_ref)]
    _attn_heads(qa_ref, qm_ref, a_segs, m_segs, sink_ref, o_ref, SEQ, depth=(1, 0))


def _attention(sink2, qa, kd, vd, qm, km, vm, kdc, vdc, kmc, vmc, layer_i):
    smem = pl.BlockSpec(memory_space=pltpu.SMEM)
    nq = DEC_SEQ // BQ
    q_off = T_CTX // BQ
    s_off = T_CTX // DEC_SEQ
    q_map = lambda b, j: (q_off + b * nq + j, 0)
    kv_map = lambda b, j: (s_off + b, 0)
    c_map = lambda b, j: (layer_i, b, 0, 0)
    o_lat = pl.pallas_call(
        _attn_lat_kernel,
        grid=(DEC_BATCH, nq),
        in_specs=[
            smem,
            pl.BlockSpec((BQ, 512), q_map),
            pl.BlockSpec((BQ, 1024), q_map),
            pl.BlockSpec((DEC_SEQ, 256), kv_map),
            pl.BlockSpec((DEC_SEQ, 256), kv_map),
            pl.BlockSpec((DEC_SEQ, 1024), kv_map),
            pl.BlockSpec((DEC_SEQ, 1024), kv_map),
            pl.BlockSpec((None, None, PAST_LEN, 256), c_map),
            pl.BlockSpec((None, None, PAST_LEN, 256), c_map),
            pl.BlockSpec((None, None, PAST_LEN, 1024), c_map),
            pl.BlockSpec((None, None, PAST_LEN, 1024), c_map),
        ],
        out_specs=pl.BlockSpec((BQ, 1024), lambda b, j: (b * nq + j, 0)),
        out_shape=jax.ShapeDtypeStruct((T_LAT, 1024), BF16),
        compiler_params=_cparams(("arbitrary", "arbitrary")),
        name="attn_latent",
    )(sink2, qa, qm, kd, vd, km, vm, kdc, vdc, kmc, vmc)
    b_map = lambda b: (b, 0)
    o_ctx = pl.pallas_call(
        _attn_ctx_kernel,
        grid=(BATCH,),
        in_specs=[
            smem,
            pl.BlockSpec((SEQ, 512), b_map),
            pl.BlockSpec((SEQ, 1024), b_map),
            pl.BlockSpec((SEQ, 256), b_map),
            pl.BlockSpec((SEQ, 256), b_map),
            pl.BlockSpec((SEQ, 1024), b_map),
            pl.BlockSpec((SEQ, 1024), b_map),
        ],
        out_specs=pl.BlockSpec((SEQ, 1024), b_map),
        out_shape=jax.ShapeDtypeStruct((T_CTX, 1024), BF16),
        compiler_params=_cparams(("arbitrary",)),
        name="attn_context",
    )(sink2, qa, qm, kd, vd, km, vm)
    return (o_ctx, o_lat)


def _conv_in_kernel(x_ref, mod_ref, g_ref, w_ref, u_ref, z_ref):
    for rows in _sub_rows():
        h = _rms_mod(x_ref[rows, :], g_ref[...], mod_ref[0:1, :], mod_ref[1:2, :])
        y = _dot(h.astype(BF16), w_ref[...])
        a = y[:, 0:CONV_CH]
        gate = y[:, CONV_CH:2 * CONV_CH]
        u_ref[rows, :] = a * jax.nn.sigmoid(gate)
        z_ref[rows, :] = y[:, 2 * CONV_CH:]


def _conv_in(x, mod, g, w_in, layer):
    tile_map = lambda i: (i, 0)
    const2 = lambda i: (0, 0)
    return pl.pallas_call(
        _conv_in_kernel,
        grid=(N_TOK,),
        in_specs=[
            pl.BlockSpec((TOK, D_MODEL), tile_map),
            pl.BlockSpec((None, None, 6, D_MODEL), lambda i: (layer, _mod_row(i), 0, 0)),
            pl.BlockSpec((1, D_MODEL), const2),
            pl.BlockSpec((D_MODEL, 3 * CONV_CH), const2),
        ],
        out_specs=[pl.BlockSpec((TOK, CONV_CH), tile_map), pl.BlockSpec((TOK, POOL_CH), tile_map)],
        out_shape=[jax.ShapeDtypeStruct((T_ALL, CONV_CH), F32), jax.ShapeDtypeStruct((T_ALL, POOL_CH), F32)],
        compiler_params=_cparams(("arbitrary",)),
        name="conv_in",
    )(x, mod, g, w_in)


CONV_ROWS = 32


def _conv_mix_kernel(u_ref, up_ref, un_ref, z_ref, zp_ref, zn_ref, wdw_ref, bdw_ref, lng_ref, lnb_ref,
                     wg_ref, ps_ref, o_ref, ue_ref, ze_ref):
    i = pl.program_id(0)
    j = (i - N_CTX_TILES) % LAT_TILES_PER_SEQ
    is_lat = i >= N_CTX_TILES
    has_prev = jnp.logical_and(is_lat, j > 0)
    has_next = jnp.logical_and(is_lat, j < LAT_TILES_PER_SEQ - 1)
    seq_len = jnp.where(is_lat, DEC_SEQ, SEQ)
    t0 = jnp.where(is_lat, j * TILE, 0)

    ue_ref[0, 0:HALO, :] = jnp.where(has_prev, up_ref[...], 0.0)
    ue_ref[0, HALO:HALO + TILE, :] = u_ref[...]
    ue_ref[0, HALO + TILE:, :] = jnp.where(has_next, un_ref[...], 0.0)
    ze_ref[0:HALO, :] = jnp.where(has_prev, zp_ref[...], 0.0)
    ze_ref[HALO:HALO + TILE, :] = z_ref[...]
    ze_ref[HALO + TILE:, :] = jnp.where(has_next, zn_ref[...], 0.0)

    pad = CONV_WIDTH // 2
    n_sh = HALO - pad + CONV_WIDTH - 1 - (SUBLANES - 1) + TILE
    for b in range(1, SUBLANES):
        ue_ref[b, 0:n_sh, :] = ue_ref[0, b:b + n_sh, :]

    for r in range(TILE // CONV_ROWS):
        r0 = r * CONV_ROWS
        acc = jnp.zeros((CONV_ROWS, CONV_CH), F32) + bdw_ref[...]
        for k in range(CONV_WIDTH):
            a, b = divmod(HALO + k - pad, SUBLANES)
            e0 = r0 + a * SUBLANES
            acc = acc + ue_ref[b, e0:e0 + CONV_ROWS, :] * wdw_ref[k:k + 1, :]
        mu = jnp.mean(acc, axis=-1, keepdims=True)
        d = acc - mu
        var = jnp.mean(d * d, axis=-1, keepdims=True)
        yn = d * lax.rsqrt(var + EPS) * lng_ref[...] + lnb_ref[...]
        o_ref[r0:r0 + CONV_ROWS, 0:CONV_CH] = (yn * jax.nn.sigmoid(yn)).astype(BF16)

    t = t0 + lax.broadcasted_iota(jnp.int32, (TILE, POOL_GROUP_W), 0)
    for gi, w in enumerate(POOL_SIZES):
        lo = w // 2
        hi = w - lo - 1
        cols = slice(gi * POOL_GROUP_W, (gi + 1) * POOL_GROUP_W)
        tot = ze_ref[HALO - lo:HALO - lo + TILE, cols]
        for s in range(-lo + 1, hi + 1):
            tot = tot + ze_ref[HALO + s:HALO + s + TILE, cols]
        cnt = jnp.minimum(t + hi + 1, seq_len) - jnp.maximum(t - lo, 0)
        d = tot / cnt.astype(F32) - ze_ref[HALO:HALO + TILE, cols]
        y = _dot(d.astype(BF16), wg_ref[gi]) * ps_ref[:, cols]
        o_ref[:, CONV_CH + gi * POOL_GROUP_W:CONV_CH + (gi + 1) * POOL_GROUP_W] = y.astype(BF16)


def _conv_mix(u, z, cp):
    tile_map = lambda i: (i, 0)
    const2 = lambda i: (0, 0)
    hb = TILE // HALO
    prev_map = lambda i: (jnp.maximum(i * hb - 1, 0), 0)
    next_map = lambda i: (jnp.minimum((i + 1) * hb, T_ALL // HALO - 1), 0)
    return pl.pallas_call(
        _conv_mix_kernel,
        grid=(N_TILES,),
        in_specs=[
            pl.BlockSpec((TILE, CONV_CH), tile_map),
            pl.BlockSpec((HALO, CONV_CH), prev_map),
            pl.BlockSpec((HALO, CONV_CH), next_map),
            pl.BlockSpec((TILE, POOL_CH), tile_map),
            pl.BlockSpec((HALO, POOL_CH), prev_map),
            pl.BlockSpec((HALO, POOL_CH), next_map),
            pl.BlockSpec((CONV_WIDTH, CONV_CH), const2),
            pl.BlockSpec((1, CONV_CH), const2),
            pl.BlockSpec((1, CONV_CH), const2),
            pl.BlockSpec((1, CONV_CH), const2),
            pl.BlockSpec((len(POOL_SIZES), POOL_GROUP_W, POOL_GROUP_W), lambda i: (0, 0, 0)),
            pl.BlockSpec((1, POOL_CH), const2),
        ],
        out_specs=pl.BlockSpec((TILE, D_MODEL), tile_map),
        out_shape=jax.ShapeDtypeStruct((T_ALL, D_MODEL), BF16),
        scratch_shapes=[pltpu.VMEM((SUBLANES, TILE + 2 * HALO, CONV_CH), F32),
                        pltpu.VMEM((TILE + 2 * HALO, POOL_CH), F32)],
        compiler_params=_cparams(("arbitrary",)),
        name="conv_mix",
    )(u, u, u, z, z, z, cp["w_dw"], cp["b_dw"], cp["ln_g"], cp["ln_b"], cp["w_grp"], cp["p_scale"])


def _post_kernel(*refs, n_o, n_x, final):
    o_refs, x_refs = refs[:n_o], refs[n_o:n_o + n_x]
    mod_ref, g_ref, wo_ref, w1_ref, w2_ref, fg_ref = refs[n_o + n_x:n_o + n_x + 6]
    y_refs = refs[n_o + n_x + 6:]
    outs = []
    for rows in _sub_rows():
        x1 = _tok_load(x_refs, rows) + mod_ref[2:3, :] * _dot(_tok_load(o_refs, rows), wo_ref[...])
        h = _rms_mod(x1, g_ref[...], mod_ref[3:4, :], mod_ref[4:5, :])
        a = jnp.maximum(_dot(h.astype(BF16), w1_ref[...]), 0.0)
        x2 = x1 + mod_ref[5:6, :] * _dot((a * a).astype(BF16), w2_ref[...])
        if final:
            ms = jnp.mean(x2 * x2, axis=-1, keepdims=True)
            x2 = x2 * lax.rsqrt(ms + EPS) * fg_ref[...]
        outs.append((rows, x2))
    _tok_store(y_refs, outs)


def _post(o, x, mod, g, w_out, w1, w2, final_g, layer, final):
    const2 = lambda i: (0, 0)
    o_ops, o_specs = _tok_specs(o, D_MODEL)
    x_ops, x_specs = _tok_specs(x, D_MODEL)
    if final:
        y = (jax.ShapeDtypeStruct((T_CTX, D_MODEL), F32), jax.ShapeDtypeStruct((T_LAT, D_MODEL), F32))
    else:
        y = jax.ShapeDtypeStruct((T_ALL, D_MODEL), F32)
    _, y_specs = _tok_specs(y, D_MODEL)
    out = pl.pallas_call(
        functools.partial(_post_kernel, n_o=len(o_ops), n_x=len(x_ops), final=final),
        grid=(N_TOK,),
        in_specs=o_specs + x_specs + [
            pl.BlockSpec((None, None, 6, D_MODEL), lambda i: (layer, _mod_row(i), 0, 0)),
            pl.BlockSpec((1, D_MODEL), const2),
            pl.BlockSpec((D_MODEL, D_MODEL), const2, pipeline_mode=pl.Buffered(1)),
            pl.BlockSpec((D_MODEL, D_FF), const2, pipeline_mode=pl.Buffered(1)),
            pl.BlockSpec((D_FF, D_MODEL), const2, pipeline_mode=pl.Buffered(1)),
            pl.BlockSpec((1, D_MODEL), const2),
        ],
        out_specs=y_specs if final else y_specs[0],
        out_shape=y,
        compiler_params=_cparams(("arbitrary",)),
        name="post_final" if final else "post",
    )(*o_ops, *x_ops, mod, g, w_out, w1, w2, final_g)
    return out


def _pair_lanes(w):
    lead = w.shape[:-1]
    w = w.reshape(lead + (w.shape[-1] // LANES, 2, 2, HEAD_DIM // 2))
    return jnp.swapaxes(w, -3, -2).reshape(lead + (-1,))


def _mla_lanes(nope, rope):
    shape = (nope if nope is not None else rope).shape[:-1]
    nope = jnp.zeros(shape + (QK_NOPE,), F32) if nope is None else nope
    rope = jnp.zeros(shape + (QK_ROPE,), F32) if rope is None else rope
    h = QK_ROPE // 2
    return jnp.concatenate([rope[..., :h], nope[..., :64 - h], rope[..., h:], nope[..., 64 - h:],
                            jnp.zeros(shape + (LANES - QK_NOPE - QK_ROPE,), F32)], axis=-1)


def _rope_tables():
    n = DEC_SEQ
    rows = n // GRID_W
    row = jnp.repeat(jnp.arange(rows), GRID_W).astype(F32)
    col = jnp.tile(jnp.arange(GRID_W), rows).astype(F32)

    def angles(dim):
        quarter = dim // 4
        inv_freq = ROPE_BASE ** (-jnp.arange(quarter, dtype=F32) / quarter)
        return jnp.concatenate([row[:, None] * inv_freq, col[:, None] * inv_freq], axis=-1)

    ang_a = angles(HEAD_DIM)
    cos_a, sin_a = jnp.cos(ang_a), jnp.sin(ang_a)
    c_a = jnp.concatenate([cos_a] * 4, axis=-1)
    s_a = jnp.concatenate([-sin_a, -sin_a, sin_a, sin_a], axis=-1)
    ang_b = angles(QK_ROPE)
    cos_b, sin_b = jnp.cos(ang_b), jnp.sin(ang_b)
    c_b = _mla_lanes(jnp.ones((n, QK_NOPE), F32), jnp.concatenate([cos_b, cos_b], axis=-1))
    s_b = _mla_lanes(None, jnp.concatenate([-sin_b, sin_b], axis=-1))
    lat = jnp.stack([c_a, s_a, c_b, s_b])
    ident = jnp.stack([jnp.ones((TOK, LANES), F32), jnp.zeros((TOK, LANES), F32)] * 2)
    return jnp.concatenate([ident, lat], axis=1)


def _prep_even(attn_w_in, mla_q_norm, mla_kv_norm, mla_w_qb, mla_w_kvb, attn_w_out, i):
    w = attn_w_in[i]
    o = np.cumsum((0, 512, 128, 128, Q_LORA, KV_LORA, QK_ROPE))
    zeros = lambda n: jnp.zeros((D_MODEL, n), F32)
    w_in = jnp.concatenate([
        _pair_lanes(w[:, o[0]:o[1]]), _pair_lanes(w[:, o[1]:o[2]]), w[:, o[2]:o[3]],
        w[:, o[3]:o[4]], zeros(Q_LORA_PAD - Q_LORA),
        w[:, o[4]:o[5]],
        _mla_lanes(None, w[:, o[5]:o[6]]),
    ], axis=1).astype(BF16)
    q_norm = jnp.pad(mla_q_norm[i], (0, Q_LORA_PAD - Q_LORA)).reshape(1, Q_LORA_PAD)
    wqb = mla_w_qb[i].reshape(Q_LORA, B_HEADS, QK_NOPE + QK_ROPE)
    wqb = jnp.pad(_mla_lanes(wqb[..., :QK_NOPE], wqb[..., QK_NOPE:]), ((0, Q_LORA_PAD - Q_LORA), (0, 0), (0, 0)))
    wkvb = mla_w_kvb[i].reshape(KV_LORA, B_HEADS, QK_NOPE + V_DIM)
    w_kk = _mla_lanes(wkvb[:, :, :QK_NOPE], None)
    w_v = wkvb[:, :, QK_NOPE:].reshape(KV_LORA, B_HEADS // 2, 2, V_DIM)
    z_v = jnp.zeros((KV_LORA, B_HEADS // 2, 2 * V_DIM), F32)
    w_kv = jnp.concatenate([w_v[:, :, 0], z_v, w_v[:, :, 1]], axis=-1)
    return {
        "w_in": w_in,
        "q_norm": q_norm,
        "kv_norm": mla_kv_norm[i].reshape(1, KV_LORA),
        "w_qb": wqb.reshape(Q_LORA_PAD, B_HEADS * LANES).astype(BF16),
        "w_kk": w_kk.reshape(KV_LORA, B_HEADS * LANES).astype(BF16),
        "w_kv": w_kv.reshape(KV_LORA, B_HEADS * LANES).astype(BF16),
        "w_out": attn_w_out[i].astype(BF16),
    }


def kernel(x_prompt, x_sample, cache_win_k, cache_win_v, cache_mla_ckv, cache_mla_krope, c, c_ctx, w_mod, b_mod,
           norm_g, attn_w_in, attn_sink, mla_q_norm, mla_kv_norm, mla_w_qb, mla_w_kvb, attn_w_out, conv_w_in,
           conv_dw, conv_dw_b, conv_ln_g, conv_ln_b, pool_w, pool_scale, conv_w_out, mlp_w1, mlp_w2, final_g):
    x = (x_prompt.reshape(T_CTX, D_MODEL), x_sample.reshape(T_LAT, D_MODEL))

    cond = jnp.concatenate([c_ctx[None, :], c, jnp.zeros((N_COND - 1 - DEC_BATCH, D_MODEL), F32)], axis=0)
    mod = _mod_table(cond, w_mod, b_mod).reshape(DEPTH, N_COND, 6, D_MODEL)

    rope = _rope_tables()
    even = [_prep_even(attn_w_in, mla_q_norm, mla_kv_norm, mla_w_qb, mla_w_kvb, attn_w_out, i)
            for i in range(N_EVEN)]
    cache_kr128 = _mla_lanes(None, cache_mla_krope)
    kdc, vdc, kmc, vmc = _ctx_kv(
        _pair_lanes(cache_win_k.reshape(DEC_BATCH, N_EVEN, PAST_LEN, LANES)),
        cache_win_v.reshape(DEC_BATCH, N_EVEN, PAST_LEN, LANES),
        cache_mla_ckv, cache_kr128,
        jnp.stack([e["w_kk"] for e in even]), jnp.stack([e["w_kv"] for e in even]))
    final_g2 = final_g.reshape(1, D_MODEL)

    ks, vs, ckvs, krs = [], [], [], []
    for l in range(DEPTH):
        g_mix = norm_g[l, 0].reshape(1, D_MODEL)
        g_mlp = norm_g[l, 1].reshape(1, D_MODEL)
        if l % 2 == 0:
            i = l // 2
            wp = even[i]
            qa, kd, vd, qm, km, vm, ka32, va32, ckv32, kr32 = _attn_in(x, mod, g_mix, wp, rope, l)
            ka32 = jnp.swapaxes(ka32.reshape(T_CTX, 2, A_KV_HEADS, HEAD_DIM // 2), 1, 2)
            ks.append(ka32.reshape(BATCH, SEQ, A_KV_HEADS, HEAD_DIM))
            vs.append(va32.reshape(BATCH, SEQ, A_KV_HEADS, HEAD_DIM))
            ckvs.append(ckv32.reshape(BATCH, SEQ, KV_LORA))
            kr32 = jnp.concatenate([kr32[:, :QK_ROPE // 2], kr32[:, 64:64 + QK_ROPE // 2]], axis=-1)
            krs.append(kr32.reshape(BATCH, SEQ, QK_ROPE))
            o = _attention(attn_sink[i] * LOG2E, qa, kd, vd, qm, km, vm, kdc, vdc, kmc, vmc, i)
            w_out = wp["w_out"]
        else:
            jj = l // 2
            u, z = _conv_in(x, mod, g_mix, conv_w_in[jj].astype(BF16), l)
            cp = {
                "w_dw": conv_dw[jj],
                "b_dw": conv_dw_b[jj].reshape(1, CONV_CH),
                "ln_g": conv_ln_g[jj].reshape(1, CONV_CH),
                "ln_b": conv_ln_b[jj].reshape(1, CONV_CH),
                "w_grp": pool_w[jj].astype(BF16),
                "p_scale": pool_scale[jj].reshape(1, POOL_CH),
            }
            o = _conv_mix(u, z, cp)
            w_out = conv_w_out[jj].astype(BF16)
        x = _post(o, x, mod, g_mlp, w_out, mlp_w1[l].astype(BF16), mlp_w2[l].astype(BF16), final_g2, l,
                  final=(l == DEPTH - 1))

    y_prompt = x[0].reshape(BATCH, SEQ, D_MODEL)
    y_sample = x[1].reshape(DEC_BATCH, DEC_SEQ, D_MODEL)
    return (y_prompt, y_sample, jnp.stack(ks, axis=1), jnp.stack(vs, axis=1),
            jnp.stack(ckvs, axis=1), jnp.stack(krs, axis=1))
```

```python
import functools
import math

import numpy as np
import jax
import jax.numpy as jnp
from jax import lax
from jax.experimental import pallas as pl
from jax.experimental.pallas import tpu as pltpu

F32 = jnp.float32
BF16 = jnp.bfloat16

D_MODEL = 1024
BATCH = 16
SEQ = 256
DEPTH = 4
DEC_BATCH = 8
DEC_SEQ = 2048
PAST_LEN = 256
GRID_W = 64
N_EVEN = (DEPTH + 1) // 2
N_ODD = DEPTH // 2
A_HEADS = 8
A_KV_HEADS = 2
A_GROUP = A_HEADS // A_KV_HEADS
HEAD_DIM = 64
WINDOW = 128
B_HEADS = 8
Q_LORA = 192
KV_LORA = 128
QK_NOPE = 64
QK_ROPE = 32
V_DIM = 64
MLA_SCALE = (QK_NOPE + QK_ROPE) ** -0.5
CONV_CH = D_MODEL // 2
CONV_WIDTH = 31
POOL_CH = D_MODEL // 2
POOL_SIZES = (2, 4, 8, 16)
POOL_GROUP_W = POOL_CH // len(POOL_SIZES)
D_FF = 4 * D_MODEL
ROPE_BASE = 10000.0
EPS = 1e-6
NEG_INF = -1e30
LOG2E = math.log2(math.e)

LANES = 128
SUBLANES = 8
VMEM_LIMIT_BYTES = 56 * 1024 * 1024

TILE = 256
T_CTX = BATCH * SEQ
T_LAT = DEC_BATCH * DEC_SEQ
T_ALL = T_CTX + T_LAT
N_CTX_TILES = T_CTX // TILE
N_TILES = T_ALL // TILE
LAT_TILES_PER_SEQ = DEC_SEQ // TILE
TOK = 512
N_CTX_TOK = T_CTX // TOK
N_TOK = T_ALL // TOK
LAT_TOK_PER_SEQ = DEC_SEQ // TOK
N_COND = 16
BQ = 256
HALO = 16
Q_LORA_PAD = 256
ATTN_IN_COLS = 512 + 128 + 128 + Q_LORA_PAD + 128 + 128


def _cparams(sem):
    return pltpu.CompilerParams(dimension_semantics=sem, vmem_limit_bytes=VMEM_LIMIT_BYTES)


def _mod_row(i):
    return jnp.where(i < N_CTX_TOK, 0, 1 + (i - N_CTX_TOK) // LAT_TOK_PER_SEQ)


def _pos_block(i):
    return jnp.where(i < N_CTX_TOK, 0, 1 + (i - N_CTX_TOK) % LAT_TOK_PER_SEQ)


def _tok_specs(a, width):
    if isinstance(a, tuple):
        return list(a), [pl.BlockSpec((TOK, width), lambda i: (jnp.minimum(i, N_CTX_TOK - 1), 0)),
                         pl.BlockSpec((TOK, width), lambda i: (jnp.maximum(i - N_CTX_TOK, 0), 0))]
    return [a], [pl.BlockSpec((TOK, width), lambda i: (i, 0))]


def _tok_load(refs, rows):
    if len(refs) == 1:
        return refs[0][rows, :]
    return jnp.where(pl.program_id(0) < N_CTX_TOK, refs[0][rows, :], refs[1][rows, :])


def _tok_store(refs, parts):
    def put(ref):
        for rows, val in parts:
            ref[rows, :] = val

    if len(refs) == 1:
        put(refs[0])
        return
    i = pl.program_id(0)

    @pl.when(i < N_CTX_TOK)
    def _():
        put(refs[0])

    @pl.when(i >= N_CTX_TOK)
    def _():
        put(refs[1])


def _sub_rows():
    return [slice(s * TILE, (s + 1) * TILE) for s in range(TOK // TILE)]


def _dot(a, b):
    return jnp.dot(a, b, preferred_element_type=F32)


def _dot_nt(a, b):
    return lax.dot_general(a, b, (((1,), (1,)), ((), ())), preferred_element_type=F32)


def _rms_mod(x, g, shift, scale):
    ms = jnp.mean(x * x, axis=-1, keepdims=True)
    return x * lax.rsqrt(ms + EPS) * (g * (1.0 + scale)) + shift


def _rope(x, c, s):
    return x * c + pltpu.roll(x, 64, 1) * s


def _lane_lt64(shape):
    return lax.broadcasted_iota(jnp.int32, shape, len(shape) - 1) < 64


def _lane_even32(shape):
    return (lax.broadcasted_iota(jnp.int32, shape, len(shape) - 1) & 63) < 32


def _store_kv_dup(kd_ref, vd_ref, rows, k, v):
    even32 = _lane_even32(k.shape)
    lt64 = _lane_lt64(v.shape)
    kd_ref[rows, 0:LANES] = jnp.where(even32, k, pltpu.roll(k, 32, 1)).astype(BF16)
    kd_ref[rows, LANES:2 * LANES] = jnp.where(even32, pltpu.roll(k, LANES - 32, 1), k).astype(BF16)
    vd_ref[rows, 0:LANES] = jnp.where(lt64, v, 1.0).astype(BF16)
    vd_ref[rows, LANES:2 * LANES] = jnp.where(lt64, pltpu.roll(v, 64, 1), 1.0).astype(BF16)


def _ones_pattern():
    lane = lax.broadcasted_iota(jnp.int32, (1, B_HEADS * LANES), 1) & (2 * LANES - 1)
    return jnp.logical_and(lane >= 64, lane < 192).astype(F32)


def _mod_kernel(cond_ref, w_ref, b_ref, o_ref):
    c = cond_ref[...]
    s = c * jax.nn.sigmoid(c)
    o_ref[...] = _dot(s.astype(BF16), w_ref[...].astype(BF16)) + b_ref[...]


def _mod_table(cond, w_mod, b_mod):
    nb = 6 * D_MODEL // 1024
    return pl.pallas_call(
        _mod_kernel,
        grid=(DEPTH, nb),
        in_specs=[
            pl.BlockSpec((N_COND, D_MODEL), lambda l, n: (0, 0)),
            pl.BlockSpec((None, D_MODEL, 1024), lambda l, n: (l, 0, n)),
            pl.BlockSpec((None, 1, 1024), lambda l, n: (l, 0, n)),
        ],
        out_specs=pl.BlockSpec((None, N_COND, 1024), lambda l, n: (l, 0, n)),
        out_shape=jax.ShapeDtypeStruct((DEPTH, N_COND, 6 * D_MODEL), F32),
        compiler_params=_cparams(("arbitrary", "arbitrary")),
        name="mod_table",
    )(cond, w_mod, b_mod.reshape(DEPTH, 1, 6 * D_MODEL))


def _attn_in_kernel(*refs, n_x):
    x_refs = refs[:n_x]
    (mod_ref, g_ref, w_ref, qn_ref, kvn_ref, wqb_ref, wkk_ref, wkv_ref, rope_ref,
     qa_ref, kd_ref, vd_ref, qm_ref, km_ref, vm_ref, ka32_ref, va32_ref, ckv32_ref, kr32_ref) = refs[n_x:]
    col = lambda n: slice(n * LANES, (n + 1) * LANES)

    def project(rows):
        h = _rms_mod(_tok_load(x_refs, rows), g_ref[...], mod_ref[0:1, :], mod_ref[1:2, :])
        return _dot(h.astype(BF16), w_ref[...])

    def derive(rows, y):
        ca, sa = rope_ref[0, rows, :], rope_ref[1, rows, :]
        cb, sb = rope_ref[2, rows, :], rope_ref[3, rows, :]

        for c in range(4):
            q = _rope(y[:, col(c)], ca, sa)
            qa_ref[rows, col(c)] = (q * (HEAD_DIM ** -0.5 * LOG2E)).astype(BF16)

        ka = _rope(y[:, 512:640], ca, sa)
        va = y[:, 640:768]
        _store_kv_dup(kd_ref, vd_ref, rows, ka, va)

        cq = y[:, 768:768 + Q_LORA_PAD]
        cqn = cq * lax.rsqrt(jnp.sum(cq * cq, axis=-1, keepdims=True) * (1.0 / Q_LORA) + EPS) * qn_ref[...]
        qm = _dot(cqn.astype(BF16), wqb_ref[...])
        for hh in range(B_HEADS):
            q = _rope(qm[:, col(hh)], cb, sb)
            qm_ref[rows, col(hh)] = (q * (MLA_SCALE * LOG2E)).astype(BF16)

        ckv = y[:, 1024:1152]
        ckvn = ckv * lax.rsqrt(jnp.mean(ckv * ckv, axis=-1, keepdims=True) + EPS) * kvn_ref[...]
        kr = y[:, 1152:1280]
        ckvn_b = ckvn.astype(BF16)
        kn = _dot(ckvn_b, wkk_ref[...])
        krr = _rope(kr, cb, sb)
        for hh in range(B_HEADS):
            km_ref[rows, col(hh)] = (kn[:, col(hh)] + krr).astype(BF16)
        vm_ref[rows, :] = (_dot(ckvn_b, wkv_ref[...]) + _ones_pattern()).astype(BF16)
        return ka, va, ckvn, kr

    subs = _sub_rows()
    cache = []
    y = project(subs[0])
    for s, rows in enumerate(subs):
        y_next = project(subs[s + 1]) if s + 1 < len(subs) else None
        cache.append(derive(rows, y))
        y = y_next

    @pl.when(pl.program_id(0) < N_CTX_TOK)
    def _():
        for rows, (ka, va, ckvn, kr) in zip(subs, cache):
            ka32_ref[rows, :] = ka
            va32_ref[rows, :] = va
            ckv32_ref[rows, :] = ckvn
            kr32_ref[rows, :] = kr


def _attn_in(x, mod, g, wp, rope, layer):
    tile_map = lambda i: (i, 0)
    const2 = lambda i: (0, 0)
    ctx_map = lambda i: (jnp.minimum(i, N_CTX_TOK - 1), 0)
    bf = lambda w: jax.ShapeDtypeStruct((T_ALL, w), BF16)
    c32 = jax.ShapeDtypeStruct((T_CTX, LANES), F32)
    x_ops, x_specs = _tok_specs(x, D_MODEL)
    return pl.pallas_call(
        functools.partial(_attn_in_kernel, n_x=len(x_ops)),
        grid=(N_TOK,),
        in_specs=x_specs + [
            pl.BlockSpec((None, None, 6, D_MODEL), lambda i: (layer, _mod_row(i), 0, 0)),
            pl.BlockSpec((1, D_MODEL), const2),
            pl.BlockSpec((D_MODEL, ATTN_IN_COLS), const2),
            pl.BlockSpec((1, Q_LORA_PAD), const2),
            pl.BlockSpec((1, KV_LORA), const2),
            pl.BlockSpec((Q_LORA_PAD, B_HEADS * LANES), const2),
            pl.BlockSpec((KV_LORA, B_HEADS * LANES), const2),
            pl.BlockSpec((KV_LORA, B_HEADS * LANES), const2),
            pl.BlockSpec((4, TOK, LANES), lambda i: (0, _pos_block(i), 0)),
        ],
        out_specs=[
            pl.BlockSpec((TOK, 512), tile_map),
            pl.BlockSpec((TOK, 256), tile_map),
            pl.BlockSpec((TOK, 256), tile_map),
            pl.BlockSpec((TOK, 1024), tile_map),
            pl.BlockSpec((TOK, 1024), tile_map),
            pl.BlockSpec((TOK, 1024), tile_map),
            pl.BlockSpec((TOK, LANES), ctx_map),
            pl.BlockSpec((TOK, LANES), ctx_map),
            pl.BlockSpec((TOK, LANES), ctx_map),
            pl.BlockSpec((TOK, LANES), ctx_map),
        ],
        out_shape=[bf(512), bf(256), bf(256), bf(1024), bf(1024), bf(1024), c32, c32, c32, c32],
        compiler_params=_cparams(("arbitrary",)),
        name="attn_in",
    )(*x_ops, mod, g, wp["w_in"], wp["q_norm"], wp["kv_norm"], wp["w_qb"], wp["w_kk"], wp["w_kv"], rope)


def _ctx_kv_kernel(ck_ref, cv_ref, cckv_ref, ckr_ref, wkk_ref, wkv_ref, kd_ref, vd_ref, km_ref, vm_ref):
    _store_kv_dup(kd_ref, vd_ref, slice(None), ck_ref[...], cv_ref[...])
    cb = cckv_ref[...].astype(BF16)
    kn = _dot(cb, wkk_ref[...])
    kr = ckr_ref[...]
    for hh in range(B_HEADS):
        km_ref[:, hh * LANES:(hh + 1) * LANES] = (kn[:, hh * LANES:(hh + 1) * LANES] + kr).astype(BF16)
    vm_ref[...] = (_dot(cb, wkv_ref[...]) + _ones_pattern()).astype(BF16)


def _ctx_kv(cache_k, cache_v, cache_ckv, cache_kr128, w_kk, w_kv):
    cache_map = lambda i, b: (b, i, 0, 0)
    w_map = lambda i, b: (i, 0, 0)
    out_map = lambda i, b: (i, b, 0, 0)
    o = lambda w: jax.ShapeDtypeStruct((N_EVEN, DEC_BATCH, PAST_LEN, w), BF16)
    return pl.pallas_call(
        _ctx_kv_kernel,
        grid=(N_EVEN, DEC_BATCH),
        in_specs=[
            pl.BlockSpec((None, None, PAST_LEN, LANES), cache_map),
            pl.BlockSpec((None, None, PAST_LEN, LANES), cache_map),
            pl.BlockSpec((None, None, PAST_LEN, KV_LORA), cache_map),
            pl.BlockSpec((None, None, PAST_LEN, LANES), cache_map),
            pl.BlockSpec((None, KV_LORA, B_HEADS * LANES), w_map),
            pl.BlockSpec((None, KV_LORA, B_HEADS * LANES), w_map),
        ],
        out_specs=[
            pl.BlockSpec((None, None, PAST_LEN, 256), out_map),
            pl.BlockSpec((None, None, PAST_LEN, 256), out_map),
            pl.BlockSpec((None, None, PAST_LEN, 1024), out_map),
            pl.BlockSpec((None, None, PAST_LEN, 1024), out_map),
        ],
        out_shape=[o(256), o(256), o(1024), o(1024)],
        compiler_params=_cparams(("arbitrary", "arbitrary")),
        name="ctx_kv",
    )(cache_k, cache_v, cache_ckv, cache_kr128, w_kk, w_kv)


def _attn_heads(qa_ref, qm_ref, a_segs, m_segs, sink_ref, o_ref, rows, depth):
    lt64 = _lane_lt64((rows, LANES))
    lane = lax.broadcasted_iota(jnp.int32, (1, LANES), 1)
    keep = (((lane & 63) < 32).astype(BF16), ((lane & 63) >= 32).astype(BF16))
    col = lambda n: slice(n * LANES, (n + 1) * LANES)
    swap = lambda r: pltpu.roll(r, 64, 1)

    jobs = []
    for kh in range(A_KV_HEADS):
        heads = [dict(out=(kh * A_GROUP + g) // 2, half=g % 2, sink=kh * A_GROUP + g) for g in range(A_GROUP)]
        jobs.append(dict(
            heads=heads,
            q=lambda heads=heads: jnp.concatenate(
                [qa_ref[:, col(h["out"])] * keep[h["half"]] for h in heads], axis=0),
            ks=[lambda kd=kd, rs=rs, kh=kh: kd[rs, col(kh)] for kd, _, rs, _ in a_segs],
            vs=[lambda vd=vd, rs=rs, kh=kh: vd[rs, col(kh)] for _, vd, rs, _ in a_segs],
            valids=[valid for _, _, _, valid in a_segs]))
    for hh in range(B_HEADS):
        jobs.append(dict(
            heads=[dict(out=4 + hh // 2, half=hh % 2, sink=None)],
            q=lambda hh=hh: qm_ref[:, col(hh)],
            ks=[lambda km=km, hh=hh: km[:, col(hh)] for km, _ in m_segs],
            vs=[lambda vm=vm, hh=hh: vm[:, col(hh)] for _, vm in m_segs],
            valids=[None for _ in m_segs]))

    def scores(job):
        full = [_dot_nt(job["q"](), k()) for k in job["ks"]]
        for b, h in enumerate(job["heads"]):
            blk = slice(b * rows, (b + 1) * rows)
            ss = [s[blk] if valid is None else jnp.where(valid, s[blk], NEG_INF)
                  for s, valid in zip(full, job["valids"])]
            m = ss[0].max(axis=-1, keepdims=True)
            for s in ss[1:]:
                m = jnp.maximum(m, s.max(axis=-1, keepdims=True))
            h["e"] = None
            if h["sink"] is not None:
                sink = sink_ref[h["sink"]]
                m = jnp.maximum(m, sink)
                h["e"] = jnp.exp2(sink - m)
            h["ss"], h["m"] = ss, m

    def values(job):
        r = None
        for i, v in enumerate(job["vs"]):
            p = [jnp.exp2(h["ss"][i] - h["m"]).astype(BF16) for h in job["heads"]]
            rs_ = _dot(p[0] if len(p) == 1 else jnp.concatenate(p, axis=0), v())
            r = rs_ if r is None else r + rs_
        for b, h in enumerate(job["heads"]):
            del h["ss"], h["m"]
            h["r"] = r[b * rows:(b + 1) * rows]

    def finish(job, done):
        for h in job["heads"]:
            r = h.pop("r")
            if h["sink"] is not None and h["half"] == 1:
                o = swap(r) / (r + h["e"])
            else:
                o = r / (swap(r) if h["e"] is None else swap(r) + h["e"])
            other = done.pop(h["out"], None)
            if other is None:
                done[h["out"]] = o
            else:
                lo, hi = (other, o) if h["half"] == 1 else (o, other)
                o_ref[:, col(h["out"])] = jnp.where(lt64, lo, hi).astype(BF16)

    done = {}
    d1, d2 = depth[0], depth[0] + depth[1]
    for t in range(len(jobs) + d2):
        if t < len(jobs):
            scores(jobs[t])
        if d1 <= t < len(jobs) + d1:
            values(jobs[t - d1])
        if t >= d2:
            finish(jobs[t - d2], done)


def _attn_lat_kernel(sink_ref, qa_ref, qm_ref, kd_ref, vd_ref, km_ref, vm_ref,
                     kdc_ref, vdc_ref, kmc_ref, vmc_ref, o_ref):
    j = pl.program_id(1)
    q0 = j * BQ
    nloc = BQ + 2 * WINDOW
    start = pl.multiple_of(jnp.clip(q0 - WINDOW, 0, DEC_SEQ - nloc), WINDOW)
    qpos = q0 + lax.broadcasted_iota(jnp.int32, (BQ, nloc), 0)
    kpos = start + lax.broadcasted_iota(jnp.int32, (BQ, nloc), 1)
    valid = jnp.abs(qpos - kpos) <= WINDOW
    everything = slice(None)
    a_segs = [(kd_ref, vd_ref, pl.ds(start, nloc), valid), (kdc_ref, vdc_ref, everything, None)]
    m_segs = [(km_ref, vm_ref), (kmc_ref, vmc_ref)]
    _attn_heads(qa_ref, qm_ref, a_segs, m_segs, sink_ref, o_ref, BQ, depth=(2, 0))


def _attn_ctx_kernel(sink_ref, qa_ref, qm_ref, kd_ref, vd_ref, km_ref, vm_ref, o_ref):
    a_segs = [(kd_ref, vd_ref, slice(None), None)]
    m_segs = [(km_ref, vm_ref)]
    _attn_heads(qa_ref, qm_ref, a_segs, m_segs, sink_ref, o_ref, SEQ, depth=(1, 0))


def _attention(sink2, qa, kd, vd, qm, km, vm, kdc, vdc, kmc, vmc, layer_i):
    smem = pl.BlockSpec(memory_space=pltpu.SMEM)
    nq = DEC_SEQ // BQ
    q_off = T_CTX // BQ
    s_off = T_CTX // DEC_SEQ
    q_map = lambda b, j: (q_off + b * nq + j, 0)
    kv_map = lambda b, j: (s_off + b, 0)
    c_map = lambda b, j: (layer_i, b, 0, 0)
    o_lat = pl.pallas_call(
        _attn_lat_kernel,
        grid=(DEC_BATCH, nq),
        in_specs=[
            smem,
            pl.BlockSpec((BQ, 512), q_map),
            pl.BlockSpec((BQ, 1024), q_map),
            pl.BlockSpec((DEC_SEQ, 256), kv_map),
            pl.BlockSpec((DEC_SEQ, 256), kv_map),
            pl.BlockSpec((DEC_SEQ, 1024), kv_map),
            pl.BlockSpec((DEC_SEQ, 1024), kv_map),
            pl.BlockSpec((None, None, PAST_LEN, 256), c_map),
            pl.BlockSpec((None, None, PAST_LEN, 256), c_map),
            pl.BlockSpec((None, None, PAST_LEN, 1024), c_map),
            pl.BlockSpec((None, None, PAST_LEN, 1024), c_map),
        ],
        out_specs=pl.BlockSpec((BQ, 1024), lambda b, j: (b * nq + j, 0)),
        out_shape=jax.ShapeDtypeStruct((T_LAT, 1024), BF16),
        compiler_params=_cparams(("arbitrary", "arbitrary")),
        name="attn_latent",
    )(sink2, qa, qm, kd, vd, km, vm, kdc, vdc, kmc, vmc)
    b_map = lambda b: (b, 0)
    o_ctx = pl.pallas_call(
        _attn_ctx_kernel,
        grid=(BATCH,),
        in_specs=[
            smem,
            pl.BlockSpec((SEQ, 512), b_map),
            pl.BlockSpec((SEQ, 1024), b_map),
            pl.BlockSpec((SEQ, 256), b_map),
            pl.BlockSpec((SEQ, 256), b_map),
            pl.BlockSpec((SEQ, 1024), b_map),
            pl.BlockSpec((SEQ, 1024), b_map),
        ],
        out_specs=pl.BlockSpec((SEQ, 1024), b_map),
        out_shape=jax.ShapeDtypeStruct((T_CTX, 1024), BF16),
        compiler_params=_cparams(("arbitrary",)),
        name="attn_context",
    )(sink2, qa, qm, kd, vd, km, vm)
    return (o_ctx, o_lat)


def _conv_in_kernel(x_ref, mod_ref, g_ref, w_ref, u_ref, z_ref):
    for rows in _sub_rows():
        h = _rms_mod(x_ref[rows, :], g_ref[...], mod_ref[0:1, :], mod_ref[1:2, :])
        y = _dot(h.astype(BF16), w_ref[...])
        a = y[:, 0:CONV_CH]
        gate = y[:, CONV_CH:2 * CONV_CH]
        u_ref[rows, :] = a * jax.nn.sigmoid(gate)
        z_ref[rows, :] = y[:, 2 * CONV_CH:]


def _conv_in(x, mod, g, w_in, layer):
    tile_map = lambda i: (i, 0)
    const2 = lambda i: (0, 0)
    return pl.pallas_call(
        _conv_in_kernel,
        grid=(N_TOK,),
        in_specs=[
            pl.BlockSpec((TOK, D_MODEL), tile_map),
            pl.BlockSpec((None, None, 6, D_MODEL), lambda i: (layer, _mod_row(i), 0, 0)),
            pl.BlockSpec((1, D_MODEL), const2),
            pl.BlockSpec((D_MODEL, 3 * CONV_CH), const2),
        ],
        out_specs=[pl.BlockSpec((TOK, CONV_CH), tile_map), pl.BlockSpec((TOK, POOL_CH), tile_map)],
        out_shape=[jax.ShapeDtypeStruct((T_ALL, CONV_CH), F32), jax.ShapeDtypeStruct((T_ALL, POOL_CH), F32)],
        compiler_params=_cparams(("arbitrary",)),
        name="conv_in",
    )(x, mod, g, w_in)


CONV_ROWS = 32


def _conv_mix_kernel(u_ref, up_ref, un_ref, z_ref, zp_ref, zn_ref, wdw_ref, bdw_ref, lng_ref, lnb_ref,
                     band_ref, wg_ref, ps_ref, o_ref, ue_ref, ze_ref):
    i = pl.program_id(0)
    j = (i - N_CTX_TILES) % LAT_TILES_PER_SEQ
    is_lat = i >= N_CTX_TILES
    has_prev = jnp.logical_and(is_lat, j > 0)
    has_next = jnp.logical_and(is_lat, j < LAT_TILES_PER_SEQ - 1)
    seq_len = jnp.where(is_lat, DEC_SEQ, SEQ)
    t0 = jnp.where(is_lat, j * TILE, 0)

    ue_ref[0, 0:HALO, :] = jnp.where(has_prev, up_ref[...], 0.0)
    ue_ref[0, HALO:HALO + TILE, :] = u_ref[...]
    ue_ref[0, HALO + TILE:, :] = jnp.where(has_next, un_ref[...], 0.0)
    ze_ref[0:HALO, :] = jnp.where(has_prev, zp_ref[...], 0.0)
    ze_ref[HALO:HALO + TILE, :] = z_ref[...]
    ze_ref[HALO + TILE:, :] = jnp.where(has_next, zn_ref[...], 0.0)

    pad = CONV_WIDTH // 2
    reach = (HALO - pad + CONV_WIDTH - 1) // SUBLANES * SUBLANES
    groups = CONV_ROWS // SUBLANES
    n_chunks = TILE // CONV_ROWS

    def shift_rows(lo, hi):
        for b in range(1, SUBLANES):
            ue_ref[b, lo:hi, :] = ue_ref[0, lo + b:hi + b, :]

    def conv_chunk(r):
        r0 = r * CONV_ROWS
        acc = jnp.zeros((groups, SUBLANES, CONV_CH), F32) + bdw_ref[...]
        for k in range(CONV_WIDTH):
            a, b = divmod(HALO + k - pad, SUBLANES)
            e0 = r0 + a * SUBLANES
            acc = acc + ue_ref[b, e0:e0 + CONV_ROWS, :].reshape(groups, SUBLANES, CONV_CH) * wdw_ref[k]
        acc = acc.reshape(CONV_ROWS, CONV_CH)
        mu = jnp.mean(acc, axis=-1, keepdims=True)
        d = acc - mu
        var = jnp.mean(d * d, axis=-1, keepdims=True)
        yn = d * lax.rsqrt(var + EPS) * lng_ref[...] + lnb_ref[...]
        o_ref[r0:r0 + CONV_ROWS, 0:CONV_CH] = (yn * jax.nn.sigmoid(yn)).astype(BF16)

    t = t0 + lax.broadcasted_iota(jnp.int32, (TILE, POOL_GROUP_W), 0)
    ze = ze_ref[...]
    z_hi = ze.astype(BF16)
    z_lo = (ze - z_hi.astype(F32)).astype(BF16)

    def pool_group(gi):
        w = POOL_SIZES[gi]
        lo = w // 2
        hi = w - lo - 1
        cols = slice(gi * POOL_GROUP_W, (gi + 1) * POOL_GROUP_W)
        tot = _dot(band_ref[gi], z_hi[:, cols]) + _dot(band_ref[gi], z_lo[:, cols])
        cnt = jnp.minimum(t + hi + 1, seq_len) - jnp.maximum(t - lo, 0)
        d = tot / cnt.astype(F32) - ze_ref[HALO:HALO + TILE, cols]
        y = _dot(d.astype(BF16), wg_ref[gi]) * ps_ref[:, cols]
        o_ref[:, CONV_CH + gi * POOL_GROUP_W:CONV_CH + (gi + 1) * POOL_GROUP_W] = y.astype(BF16)

    shift_rows(0, reach + CONV_ROWS)
    for r in range(n_chunks):
        if r + 1 < n_chunks:
            shift_rows(reach + (r + 1) * CONV_ROWS, reach + (r + 2) * CONV_ROWS)
        conv_chunk(r)
        if r % (n_chunks // len(POOL_SIZES)) == 0:
            pool_group(r // (n_chunks // len(POOL_SIZES)))


def _conv_mix(u, z, cp):
    tile_map = lambda i: (i, 0)
    const2 = lambda i: (0, 0)
    hb = TILE // HALO
    prev_map = lambda i: (jnp.maximum(i * hb - 1, 0), 0)
    next_map = lambda i: (jnp.minimum((i + 1) * hb, T_ALL // HALO - 1), 0)
    return pl.pallas_call(
        _conv_mix_kernel,
        grid=(N_TILES,),
        in_specs=[
            pl.BlockSpec((TILE, CONV_CH), tile_map),
            pl.BlockSpec((HALO, CONV_CH), prev_map),
            pl.BlockSpec((HALO, CONV_CH), next_map),
            pl.BlockSpec((TILE, POOL_CH), tile_map),
            pl.BlockSpec((HALO, POOL_CH), prev_map),
            pl.BlockSpec((HALO, POOL_CH), next_map),
            pl.BlockSpec((CONV_WIDTH, SUBLANES, CONV_CH), lambda i: (0, 0, 0)),
            pl.BlockSpec((1, CONV_CH), const2),
            pl.BlockSpec((1, CONV_CH), const2),
            pl.BlockSpec((1, CONV_CH), const2),
            pl.BlockSpec((len(POOL_SIZES), TILE, TILE + 2 * HALO), lambda i: (0, 0, 0)),
            pl.BlockSpec((len(POOL_SIZES), POOL_GROUP_W, POOL_GROUP_W), lambda i: (0, 0, 0)),
            pl.BlockSpec((1, POOL_CH), const2),
        ],
        out_specs=pl.BlockSpec((TILE, D_MODEL), tile_map),
        out_shape=jax.ShapeDtypeStruct((T_ALL, D_MODEL), BF16),
        scratch_shapes=[pltpu.VMEM((SUBLANES, TILE + 2 * HALO, CONV_CH), F32),
                        pltpu.VMEM((TILE + 2 * HALO, POOL_CH), F32)],
        compiler_params=_cparams(("arbitrary",)),
        name="conv_mix",
    )(u, u, u, z, z, z, cp["w_dw"], cp["b_dw"], cp["ln_g"], cp["ln_b"], _pool_bands(), cp["w_grp"], cp["p_scale"])


def _pool_bands():
    t = np.arange(TILE)[:, None]
    e = np.arange(TILE + 2 * HALO)[None, :] - HALO
    bands = []
    for w in POOL_SIZES:
        lo = w // 2
        hi = w - lo - 1
        bands.append((e >= t - lo) & (e <= t + hi))
    return jnp.asarray(np.stack(bands), BF16)


def _post_kernel(*refs, n_o, n_x, final):
    o_refs, x_refs = refs[:n_o], refs[n_o:n_o + n_x]
    mod_ref, g_ref, wo_ref, w1_ref, w2_ref, fg_ref = refs[n_o + n_x:n_o + n_x + 6]
    y_refs = refs[n_o + n_x + 6:]
    outs = []
    for rows in _sub_rows():
        x1 = _tok_load(x_refs, rows) + mod_ref[2:3, :] * _dot(_tok_load(o_refs, rows), wo_ref[...])
        h = _rms_mod(x1, g_ref[...], mod_ref[3:4, :], mod_ref[4:5, :])
        a = jnp.maximum(_dot(h.astype(BF16), w1_ref[...]), 0.0)
        x2 = x1 + mod_ref[5:6, :] * _dot((a * a).astype(BF16), w2_ref[...])
        if final:
            ms = jnp.mean(x2 * x2, axis=-1, keepdims=True)
            x2 = x2 * lax.rsqrt(ms + EPS) * fg_ref[...]
        outs.append((rows, x2))
    _tok_store(y_refs, outs)


def _post(o, x, mod, g, w_out, w1, w2, final_g, layer, final):
    const2 = lambda i: (0, 0)
    o_ops, o_specs = _tok_specs(o, D_MODEL)
    x_ops, x_specs = _tok_specs(x, D_MODEL)
    if final:
        y = (jax.ShapeDtypeStruct((T_CTX, D_MODEL), F32), jax.ShapeDtypeStruct((T_LAT, D_MODEL), F32))
    else:
        y = jax.ShapeDtypeStruct((T_ALL, D_MODEL), F32)
    _, y_specs = _tok_specs(y, D_MODEL)
    out = pl.pallas_call(
        functools.partial(_post_kernel, n_o=len(o_ops), n_x=len(x_ops), final=final),
        grid=(N_TOK,),
        in_specs=o_specs + x_specs + [
            pl.BlockSpec((None, None, 6, D_MODEL), lambda i: (layer, _mod_row(i), 0, 0)),
            pl.BlockSpec((1, D_MODEL), const2),
            pl.BlockSpec((D_MODEL, D_MODEL), const2, pipeline_mode=pl.Buffered(1)),
            pl.BlockSpec((None, D_MODEL, D_FF), lambda i: (layer, 0, 0), pipeline_mode=pl.Buffered(1)),
            pl.BlockSpec((None, D_FF, D_MODEL), lambda i: (layer, 0, 0), pipeline_mode=pl.Buffered(1)),
            pl.BlockSpec((1, D_MODEL), const2),
        ],
        out_specs=y_specs if final else y_specs[0],
        out_shape=y,
        compiler_params=_cparams(("arbitrary",)),
        name="post_final" if final else "post",
    )(*o_ops, *x_ops, mod, g, w_out, w1, w2, final_g)
    return out


def _pair_lanes(w):
    lead = w.shape[:-1]
    w = w.reshape(lead + (w.shape[-1] // LANES, 2, 2, HEAD_DIM // 2))
    return jnp.swapaxes(w, -3, -2).reshape(lead + (-1,))


def _mla_lanes(nope, rope):
    shape = (nope if nope is not None else rope).shape[:-1]
    nope = jnp.zeros(shape + (QK_NOPE,), F32) if nope is None else nope
    rope = jnp.zeros(shape + (QK_ROPE,), F32) if rope is None else rope
    h = QK_ROPE // 2
    return jnp.concatenate([rope[..., :h], nope[..., :64 - h], rope[..., h:], nope[..., 64 - h:],
                            jnp.zeros(shape + (LANES - QK_NOPE - QK_ROPE,), F32)], axis=-1)


def _rope_tables():
    n = DEC_SEQ
    rows = n // GRID_W
    row = jnp.repeat(jnp.arange(rows), GRID_W).astype(F32)
    col = jnp.tile(jnp.arange(GRID_W), rows).astype(F32)

    def angles(dim):
        quarter = dim // 4
        inv_freq = ROPE_BASE ** (-jnp.arange(quarter, dtype=F32) / quarter)
        return jnp.concatenate([row[:, None] * inv_freq, col[:, None] * inv_freq], axis=-1)

    ang_a = angles(HEAD_DIM)
    cos_a, sin_a = jnp.cos(ang_a), jnp.sin(ang_a)
    c_a = jnp.concatenate([cos_a] * 4, axis=-1)
    s_a = jnp.concatenate([-sin_a, -sin_a, sin_a, sin_a], axis=-1)
    ang_b = angles(QK_ROPE)
    cos_b, sin_b = jnp.cos(ang_b), jnp.sin(ang_b)
    c_b = _mla_lanes(jnp.ones((n, QK_NOPE), F32), jnp.concatenate([cos_b, cos_b], axis=-1))
    s_b = _mla_lanes(None, jnp.concatenate([-sin_b, sin_b], axis=-1))
    lat = jnp.stack([c_a, s_a, c_b, s_b])
    ident = jnp.stack([jnp.ones((TOK, LANES), F32), jnp.zeros((TOK, LANES), F32)] * 2)
    return jnp.concatenate([ident, lat], axis=1)


def _prep_even(attn_w_in, mla_q_norm, mla_kv_norm, mla_w_qb, mla_w_kvb, attn_w_out, i):
    w = attn_w_in[i]
    o = np.cumsum((0, 512, 128, 128, Q_LORA, KV_LORA, QK_ROPE))
    zeros = lambda n: jnp.zeros((D_MODEL, n), F32)
    w_in = jnp.concatenate([
        _pair_lanes(w[:, o[0]:o[1]]), _pair_lanes(w[:, o[1]:o[2]]), w[:, o[2]:o[3]],
        w[:, o[3]:o[4]], zeros(Q_LORA_PAD - Q_LORA),
        w[:, o[4]:o[5]],
        _mla_lanes(None, w[:, o[5]:o[6]]),
    ], axis=1).astype(BF16)
    q_norm = jnp.pad(mla_q_norm[i], (0, Q_LORA_PAD - Q_LORA)).reshape(1, Q_LORA_PAD)
    wqb = mla_w_qb[i].reshape(Q_LORA, B_HEADS, QK_NOPE + QK_ROPE)
    wqb = jnp.pad(_mla_lanes(wqb[..., :QK_NOPE], wqb[..., QK_NOPE:]), ((0, Q_LORA_PAD - Q_LORA), (0, 0), (0, 0)))
    wkvb = mla_w_kvb[i].reshape(KV_LORA, B_HEADS, QK_NOPE + V_DIM)
    w_kk = _mla_lanes(wkvb[:, :, :QK_NOPE], None)
    w_v = wkvb[:, :, QK_NOPE:].reshape(KV_LORA, B_HEADS // 2, 2, V_DIM)
    z_v = jnp.zeros((KV_LORA, B_HEADS // 2, 2 * V_DIM), F32)
    w_kv = jnp.concatenate([w_v[:, :, 0], z_v, w_v[:, :, 1]], axis=-1)
    return {
        "w_in": w_in,
        "q_norm": q_norm,
        "kv_norm": mla_kv_norm[i].reshape(1, KV_LORA),
        "w_qb": wqb.reshape(Q_LORA_PAD, B_HEADS * LANES).astype(BF16),
        "w_kk": w_kk.reshape(KV_LORA, B_HEADS * LANES).astype(BF16),
        "w_kv": w_kv.reshape(KV_LORA, B_HEADS * LANES).astype(BF16),
        "w_out": attn_w_out[i].astype(BF16),
    }


def kernel(x_prompt, x_sample, cache_win_k, cache_win_v, cache_mla_ckv, cache_mla_krope, c, c_ctx, w_mod, b_mod,
           norm_g, attn_w_in, attn_sink, mla_q_norm, mla_kv_norm, mla_w_qb, mla_w_kvb, attn_w_out, conv_w_in,
           conv_dw, conv_dw_b, conv_ln_g, conv_ln_b, pool_w, pool_scale, conv_w_out, mlp_w1, mlp_w2, final_g):
    x = (x_prompt.reshape(T_CTX, D_MODEL), x_sample.reshape(T_LAT, D_MODEL))

    cond = jnp.concatenate([c_ctx[None, :], c, jnp.zeros((N_COND - 1 - DEC_BATCH, D_MODEL), F32)], axis=0)
    mod = _mod_table(cond, w_mod, b_mod).reshape(DEPTH, N_COND, 6, D_MODEL)

    rope = _rope_tables()
    even = [_prep_even(attn_w_in, mla_q_norm, mla_kv_norm, mla_w_qb, mla_w_kvb, attn_w_out, i)
            for i in range(N_EVEN)]
    cache_kr128 = _mla_lanes(None, cache_mla_krope)
    kdc, vdc, kmc, vmc = _ctx_kv(
        _pair_lanes(cache_win_k.reshape(DEC_BATCH, N_EVEN, PAST_LEN, LANES)),
        cache_win_v.reshape(DEC_BATCH, N_EVEN, PAST_LEN, LANES),
        cache_mla_ckv, cache_kr128,
        jnp.stack([e["w_kk"] for e in even]), jnp.stack([e["w_kv"] for e in even]))
    final_g2 = final_g.reshape(1, D_MODEL)
    w1_all, w2_all = mlp_w1.astype(BF16), mlp_w2.astype(BF16)

    ks, vs, ckvs, krs = [], [], [], []
    for l in range(DEPTH):
        g_mix = norm_g[l, 0].reshape(1, D_MODEL)
        g_mlp = norm_g[l, 1].reshape(1, D_MODEL)
        if l % 2 == 0:
            i = l // 2
            wp = even[i]
            qa, kd, vd, qm, km, vm, ka32, va32, ckv32, kr32 = _attn_in(x, mod, g_mix, wp, rope, l)
            ka32 = jnp.swapaxes(ka32.reshape(T_CTX, 2, A_KV_HEADS, HEAD_DIM // 2), 1, 2)
            ks.append(ka32.reshape(BATCH, SEQ, A_KV_HEADS, HEAD_DIM))
            vs.append(va32.reshape(BATCH, SEQ, A_KV_HEADS, HEAD_DIM))
            ckvs.append(ckv32.reshape(BATCH, SEQ, KV_LORA))
            kr32 = jnp.concatenate([kr32[:, :QK_ROPE // 2], kr32[:, 64:64 + QK_ROPE // 2]], axis=-1)
            krs.append(kr32.reshape(BATCH, SEQ, QK_ROPE))
            o = _attention(attn_sink[i] * LOG2E, qa, kd, vd, qm, km, vm, kdc, vdc, kmc, vmc, i)
            w_out = wp["w_out"]
        else:
            jj = l // 2
            u, z = _conv_in(x, mod, g_mix, conv_w_in[jj].astype(BF16), l)
            cp = {
                "w_dw": jnp.broadcast_to(conv_dw[jj][:, None, :], (CONV_WIDTH, SUBLANES, CONV_CH)),
                "b_dw": conv_dw_b[jj].reshape(1, CONV_CH),
                "ln_g": conv_ln_g[jj].reshape(1, CONV_CH),
                "ln_b": conv_ln_b[jj].reshape(1, CONV_CH),
                "w_grp": pool_w[jj].astype(BF16),
                "p_scale": pool_scale[jj].reshape(1, POOL_CH),
            }
            o = _conv_mix(u, z, cp)
            w_out = conv_w_out[jj].astype(BF16)
        x = _post(o, x, mod, g_mlp, w_out, w1_all, w2_all, final_g2, l, final=(l == DEPTH - 1))

    y_prompt = x[0].reshape(BATCH, SEQ, D_MODEL)
    y_sample = x[1].reshape(DEC_BATCH, DEC_SEQ, D_MODEL)
    return (y_prompt, y_sample, jnp.stack(ks, axis=1), jnp.stack(vs, axis=1),
            jnp.stack(ckvs, axis=1), jnp.stack(krs, axis=1))
```

```python
import functools
import math

import numpy as np
import jax
import jax.numpy as jnp
from jax import lax
from jax.experimental import pallas as pl
from jax.experimental.pallas import tpu as pltpu

F32 = jnp.float32
BF16 = jnp.bfloat16

D_MODEL = 1024
BATCH = 16
SEQ = 256
DEPTH = 4
DEC_BATCH = 8
DEC_SEQ = 2048
PAST_LEN = 256
GRID_W = 64
N_EVEN = (DEPTH + 1) // 2
N_ODD = DEPTH // 2
A_HEADS = 8
A_KV_HEADS = 2
A_GROUP = A_HEADS // A_KV_HEADS
HEAD_DIM = 64
WINDOW = 128
B_HEADS = 8
Q_LORA = 192
KV_LORA = 128
QK_NOPE = 64
QK_ROPE = 32
V_DIM = 64
MLA_SCALE = (QK_NOPE + QK_ROPE) ** -0.5
CONV_CH = D_MODEL // 2
CONV_WIDTH = 31
POOL_CH = D_MODEL // 2
POOL_SIZES = (2, 4, 8, 16)
POOL_GROUP_W = POOL_CH // len(POOL_SIZES)
D_FF = 4 * D_MODEL
ROPE_BASE = 10000.0
EPS = 1e-6
NEG_INF = -1e30
LOG2E = math.log2(math.e)

LANES = 128
SUBLANES = 8
VMEM_LIMIT_BYTES = 56 * 1024 * 1024

TILE = 256
T_CTX = BATCH * SEQ
T_LAT = DEC_BATCH * DEC_SEQ
T_ALL = T_CTX + T_LAT
N_CTX_TILES = T_CTX // TILE
N_TILES = T_ALL // TILE
LAT_TILES_PER_SEQ = DEC_SEQ // TILE
TOK = 512
N_CTX_TOK = T_CTX // TOK
N_TOK = T_ALL // TOK
LAT_TOK_PER_SEQ = DEC_SEQ // TOK
N_COND = 16
BQ = 256
VT_ROWS = 80
JOB_ORDER = (0, 1, 2, 3, 4, 5, 6, 7, 8, 9)
HALO = 16
Q_LORA_PAD = 256
ATTN_IN_COLS = 512 + 128 + 128 + Q_LORA_PAD + 128 + 128


def _cparams(sem):
    return pltpu.CompilerParams(dimension_semantics=sem, vmem_limit_bytes=VMEM_LIMIT_BYTES)


def _mod_row(i):
    return jnp.where(i < N_CTX_TOK, 0, 1 + (i - N_CTX_TOK) // LAT_TOK_PER_SEQ)


def _pos_block(i):
    return jnp.where(i < N_CTX_TOK, 0, 1 + (i - N_CTX_TOK) % LAT_TOK_PER_SEQ)


def _tok_specs(a, width):
    if isinstance(a, tuple):
        return list(a), [pl.BlockSpec((TOK, width), lambda i: (jnp.minimum(i, N_CTX_TOK - 1), 0)),
                         pl.BlockSpec((TOK, width), lambda i: (jnp.maximum(i - N_CTX_TOK, 0), 0))]
    return [a], [pl.BlockSpec((TOK, width), lambda i: (i, 0))]


def _tok_load(refs, rows):
    if len(refs) == 1:
        return refs[0][rows, :]
    return jnp.where(pl.program_id(0) < N_CTX_TOK, refs[0][rows, :], refs[1][rows, :])


def _tok_store(refs, parts):
    def put(ref):
        for rows, val in parts:
            ref[rows, :] = val

    if len(refs) == 1:
        put(refs[0])
        return
    i = pl.program_id(0)

    @pl.when(i < N_CTX_TOK)
    def _():
        put(refs[0])

    @pl.when(i >= N_CTX_TOK)
    def _():
        put(refs[1])


def _sub_rows():
    return [slice(s * TILE, (s + 1) * TILE) for s in range(TOK // TILE)]


def _dot(a, b):
    return jnp.dot(a, b, preferred_element_type=F32)


def _dot_nt(a, b):
    return lax.dot_general(a, b, (((1,), (1,)), ((), ())), preferred_element_type=F32)


def _rms_mod(x, g, shift, scale):
    ms = jnp.mean(x * x, axis=-1, keepdims=True)
    return x * lax.rsqrt(ms + EPS) * (g * (1.0 + scale)) + shift


def _rope(x, c, s):
    return x * c + pltpu.roll(x, 64, 1) * s


def _lane_lt64(shape):
    return lax.broadcasted_iota(jnp.int32, shape, len(shape) - 1) < 64


def _lane_even32(shape):
    return (lax.broadcasted_iota(jnp.int32, shape, len(shape) - 1) & 63) < 32


def _store_kv_dup(kd_ref, vd_ref, rows, k, v):
    even32 = _lane_even32(k.shape)
    lt64 = _lane_lt64(v.shape)
    kd_ref[rows, 0:LANES] = jnp.where(even32, k, pltpu.roll(k, 32, 1)).astype(BF16)
    kd_ref[rows, LANES:2 * LANES] = jnp.where(even32, pltpu.roll(k, LANES - 32, 1), k).astype(BF16)
    vd_ref[rows, 0:LANES] = jnp.where(lt64, v, 1.0).astype(BF16)
    vd_ref[rows, LANES:2 * LANES] = jnp.where(lt64, pltpu.roll(v, 64, 1), 1.0).astype(BF16)


def _store_vt(vt_ref, cols, ckvn, wkvt_ref):
    vt = _dot(wkvt_ref[...], ckvn.T.astype(BF16))
    ones = jnp.ones((VT_ROWS - V_DIM, ckvn.shape[0]), BF16)
    for hh in range(B_HEADS):
        vt_ref[hh * VT_ROWS:hh * VT_ROWS + V_DIM, cols] = vt[hh * V_DIM:(hh + 1) * V_DIM].astype(BF16)
        vt_ref[hh * VT_ROWS + V_DIM:(hh + 1) * VT_ROWS, cols] = ones


def _mod_kernel(cond_ref, w_ref, b_ref, o_ref):
    c = cond_ref[...]
    s = c * jax.nn.sigmoid(c)
    o_ref[...] = _dot(s.astype(BF16), w_ref[...].astype(BF16)) + b_ref[...]


def _mod_table(cond, w_mod, b_mod):
    nb = 6 * D_MODEL // 1024
    return pl.pallas_call(
        _mod_kernel,
        grid=(DEPTH, nb),
        in_specs=[
            pl.BlockSpec((N_COND, D_MODEL), lambda l, n: (0, 0)),
            pl.BlockSpec((None, D_MODEL, 1024), lambda l, n: (l, 0, n)),
            pl.BlockSpec((None, 1, 1024), lambda l, n: (l, 0, n)),
        ],
        out_specs=pl.BlockSpec((None, N_COND, 1024), lambda l, n: (l, 0, n)),
        out_shape=jax.ShapeDtypeStruct((DEPTH, N_COND, 6 * D_MODEL), F32),
        compiler_params=_cparams(("arbitrary", "arbitrary")),
        name="mod_table",
    )(cond, w_mod, b_mod.reshape(DEPTH, 1, 6 * D_MODEL))


def _attn_in_kernel(*refs, n_x):
    x_refs = refs[:n_x]
    (mod_ref, g_ref, w_ref, qn_ref, kvn_ref, wqb_ref, wkk_ref, wkv_ref, rope_ref,
     qa_ref, kd_ref, vd_ref, qm_ref, km_ref, vm_ref, ka32_ref, va32_ref, ckv32_ref, kr32_ref) = refs[n_x:]
    col = lambda n: slice(n * LANES, (n + 1) * LANES)

    def project(rows):
        h = _rms_mod(_tok_load(x_refs, rows), g_ref[...], mod_ref[0:1, :], mod_ref[1:2, :])
        return _dot(h.astype(BF16), w_ref[...])

    def derive(rows, y):
        ca, sa = rope_ref[0, rows, :], rope_ref[1, rows, :]
        cb, sb = rope_ref[2, rows, :], rope_ref[3, rows, :]

        for c in range(4):
            q = _rope(y[:, col(c)], ca, sa)
            qa_ref[rows, col(c)] = (q * (HEAD_DIM ** -0.5 * LOG2E)).astype(BF16)

        ka = _rope(y[:, 512:640], ca, sa)
        va = y[:, 640:768]
        _store_kv_dup(kd_ref, vd_ref, rows, ka, va)

        cq = y[:, 768:768 + Q_LORA_PAD]
        cqn = cq * lax.rsqrt(jnp.sum(cq * cq, axis=-1, keepdims=True) * (1.0 / Q_LORA) + EPS) * qn_ref[...]
        qm = _dot(cqn.astype(BF16), wqb_ref[...])
        for hh in range(B_HEADS):
            q = _rope(qm[:, col(hh)], cb, sb)
            qm_ref[rows, col(hh)] = (q * (MLA_SCALE * LOG2E)).astype(BF16)

        ckv = y[:, 1024:1152]
        ckvn = ckv * lax.rsqrt(jnp.mean(ckv * ckv, axis=-1, keepdims=True) + EPS) * kvn_ref[...]
        kr = y[:, 1152:1280]
        ckvn_b = ckvn.astype(BF16)
        kn = _dot(ckvn_b, wkk_ref[...])
        krr = _rope(kr, cb, sb)
        for hh in range(B_HEADS):
            km_ref[rows, col(hh)] = (kn[:, col(hh)] + krr).astype(BF16)
        _store_vt(vm_ref, rows, ckvn, wkv_ref)
        return ka, va, ckvn, kr

    subs = _sub_rows()
    cache = []
    y = project(subs[0])
    for s, rows in enumerate(subs):
        y_next = project(subs[s + 1]) if s + 1 < len(subs) else None
        cache.append(derive(rows, y))
        y = y_next

    @pl.when(pl.program_id(0) < N_CTX_TOK)
    def _():
        for rows, (ka, va, ckvn, kr) in zip(subs, cache):
            ka32_ref[rows, :] = ka
            va32_ref[rows, :] = va
            ckv32_ref[rows, :] = ckvn
            kr32_ref[rows, :] = kr


def _attn_in(x, mod, g, wp, rope, layer):
    tile_map = lambda i: (i, 0)
    const2 = lambda i: (0, 0)
    ctx_map = lambda i: (jnp.minimum(i, N_CTX_TOK - 1), 0)
    bf = lambda w: jax.ShapeDtypeStruct((T_ALL, w), BF16)
    c32 = jax.ShapeDtypeStruct((T_CTX, LANES), F32)
    x_ops, x_specs = _tok_specs(x, D_MODEL)
    return pl.pallas_call(
        functools.partial(_attn_in_kernel, n_x=len(x_ops)),
        grid=(N_TOK,),
        in_specs=x_specs + [
            pl.BlockSpec((None, None, 6, D_MODEL), lambda i: (layer, _mod_row(i), 0, 0)),
            pl.BlockSpec((1, D_MODEL), const2),
            pl.BlockSpec((D_MODEL, ATTN_IN_COLS), const2),
            pl.BlockSpec((1, Q_LORA_PAD), const2),
            pl.BlockSpec((1, KV_LORA), const2),
            pl.BlockSpec((Q_LORA_PAD, B_HEADS * LANES), const2),
            pl.BlockSpec((KV_LORA, B_HEADS * LANES), const2),
            pl.BlockSpec((B_HEADS * V_DIM, KV_LORA), const2),
            pl.BlockSpec((4, TOK, LANES), lambda i: (0, _pos_block(i), 0)),
        ],
        out_specs=[
            pl.BlockSpec((TOK, 512), tile_map),
            pl.BlockSpec((TOK, 256), tile_map),
            pl.BlockSpec((TOK, 256), tile_map),
            pl.BlockSpec((TOK, 1024), tile_map),
            pl.BlockSpec((TOK, 1024), tile_map),
            pl.BlockSpec((B_HEADS * VT_ROWS, TOK), lambda i: (0, i)),
            pl.BlockSpec((TOK, LANES), ctx_map),
            pl.BlockSpec((TOK, LANES), ctx_map),
            pl.BlockSpec((TOK, LANES), ctx_map),
            pl.BlockSpec((TOK, LANES), ctx_map),
        ],
        out_shape=[bf(512), bf(256), bf(256), bf(1024), bf(1024),
                   jax.ShapeDtypeStruct((B_HEADS * VT_ROWS, T_ALL), BF16), c32, c32, c32, c32],
        compiler_params=_cparams(("arbitrary",)),
        name="attn_in",
    )(*x_ops, mod, g, wp["w_in"], wp["q_norm"], wp["kv_norm"], wp["w_qb"], wp["w_kk"], wp["w_kv"], rope)


def _ctx_kv_kernel(ck_ref, cv_ref, cckv_ref, ckr_ref, wkk_ref, wkv_ref, kd_ref, vd_ref, km_ref, vm_ref):
    _store_kv_dup(kd_ref, vd_ref, slice(None), ck_ref[...], cv_ref[...])
    cb = cckv_ref[...].astype(BF16)
    kn = _dot(cb, wkk_ref[...])
    kr = ckr_ref[...]
    for hh in range(B_HEADS):
        km_ref[:, hh * LANES:(hh + 1) * LANES] = (kn[:, hh * LANES:(hh + 1) * LANES] + kr).astype(BF16)
    _store_vt(vm_ref, slice(None), cckv_ref[...], wkv_ref)


def _ctx_kv(cache_k, cache_v, cache_ckv, cache_kr128, w_kk, w_kv):
    cache_map = lambda i, b: (b, i, 0, 0)
    w_map = lambda i, b: (i, 0, 0)
    out_map = lambda i, b: (i, b, 0, 0)
    o = lambda w: jax.ShapeDtypeStruct((N_EVEN, DEC_BATCH, PAST_LEN, w), BF16)
    return pl.pallas_call(
        _ctx_kv_kernel,
        grid=(N_EVEN, DEC_BATCH),
        in_specs=[
            pl.BlockSpec((None, None, PAST_LEN, LANES), cache_map),
            pl.BlockSpec((None, None, PAST_LEN, LANES), cache_map),
            pl.BlockSpec((None, None, PAST_LEN, KV_LORA), cache_map),
            pl.BlockSpec((None, None, PAST_LEN, LANES), cache_map),
            pl.BlockSpec((None, KV_LORA, B_HEADS * LANES), w_map),
            pl.BlockSpec((None, B_HEADS * V_DIM, KV_LORA), w_map),
        ],
        out_specs=[
            pl.BlockSpec((None, None, PAST_LEN, 256), out_map),
            pl.BlockSpec((None, None, PAST_LEN, 256), out_map),
            pl.BlockSpec((None, None, PAST_LEN, 1024), out_map),
            pl.BlockSpec((None, None, B_HEADS * VT_ROWS, PAST_LEN), out_map),
        ],
        out_shape=[o(256), o(256), o(1024),
                   jax.ShapeDtypeStruct((N_EVEN, DEC_BATCH, B_HEADS * VT_ROWS, PAST_LEN), BF16)],
        compiler_params=_cparams(("arbitrary", "arbitrary")),
        name="ctx_kv",
    )(cache_k, cache_v, cache_ckv, cache_kr128, w_kk, w_kv)


def _attn_heads(qa_ref, qm_ref, a_segs, m_segs, sink_ref, o_ref, rows, depth):
    lt64 = _lane_lt64((rows, LANES))
    lane = lax.broadcasted_iota(jnp.int32, (1, LANES), 1)
    keep = (((lane & 63) < 32).astype(BF16), ((lane & 63) >= 32).astype(BF16))
    col = lambda n: slice(n * LANES, (n + 1) * LANES)
    swap = lambda r: pltpu.roll(r, 64, 1)

    jobs = []
    for kh in range(A_KV_HEADS):
        heads = [dict(out=(kh * A_GROUP + g) // 2, half=g % 2, sink=kh * A_GROUP + g) for g in range(A_GROUP)]
        jobs.append(dict(
            heads=heads,
            q=lambda heads=heads: jnp.concatenate(
                [qa_ref[:, col(h["out"])] * keep[h["half"]] for h in heads], axis=0),
            ks=[lambda kd=kd, rs=rs, kh=kh: kd[rs, col(kh)] for kd, _, rs, _ in a_segs],
            vs=[lambda vd=vd, rs=rs, kh=kh: vd[rs, col(kh)] for _, vd, rs, _ in a_segs],
            valids=[valid for _, _, _, valid in a_segs]))
    for hh in range(B_HEADS):
        jobs.append(dict(
            keys_on_rows=True,
            heads=[dict(out=4 + hh // 2, half=hh % 2, sink=None)],
            q=lambda hh=hh: qm_ref[:, col(hh)],
            ks=[lambda km=km, hh=hh: km[:, col(hh)] for km, _ in m_segs],
            vs=[lambda vmt=vmt, hh=hh: vmt[hh * VT_ROWS:(hh + 1) * VT_ROWS, :] for _, vmt in m_segs]))
    jobs = [jobs[i] for i in JOB_ORDER]

    def scores(job):
        if job.get("keys_on_rows"):
            h = job["heads"][0]
            q = job["q"]()
            h["ss"] = [_dot_nt(k(), q) for k in job["ks"]]
            m = h["ss"][0].max(axis=0, keepdims=True)
            for s in h["ss"][1:]:
                m = jnp.maximum(m, s.max(axis=0, keepdims=True))
            h["m"] = m
            return
        full = [_dot_nt(job["q"](), k()) for k in job["ks"]]
        for b, h in enumerate(job["heads"]):
            blk = slice(b * rows, (b + 1) * rows)
            ss = [s[blk] if valid is None else jnp.where(valid, s[blk], NEG_INF)
                  for s, valid in zip(full, job["valids"])]
            m = ss[0].max(axis=-1, keepdims=True)
            for s in ss[1:]:
                m = jnp.maximum(m, s.max(axis=-1, keepdims=True))
            h["e"] = None
            if h["sink"] is not None:
                sink = sink_ref[h["sink"]]
                m = jnp.maximum(m, sink)
                h["e"] = jnp.exp2(sink - m)
            h["ss"], h["m"] = ss, m

    def values(job):
        if job.get("keys_on_rows"):
            h = job["heads"][0]
            r = None
            for s, vt in zip(h.pop("ss"), job["vs"]):
                rs_ = _dot(vt(), jnp.exp2(s - h["m"]).astype(BF16))
                r = rs_ if r is None else r + rs_
            del h["m"]
            h["r"] = r
            return
        r = None
        for i, v in enumerate(job["vs"]):
            p = [jnp.exp2(h["ss"][i] - h["m"]).astype(BF16) for h in job["heads"]]
            rs_ = _dot(p[0] if len(p) == 1 else jnp.concatenate(p, axis=0), v())
            r = rs_ if r is None else r + rs_
        for b, h in enumerate(job["heads"]):
            del h["ss"], h["m"]
            h["r"] = r[b * rows:(b + 1) * rows]

    def finish(job, done):
        if job.get("keys_on_rows"):
            h = job["heads"][0]
            r = h.pop("r")
            o_t = r[0:V_DIM] / r[V_DIM:V_DIM + 1]
            other = done.pop(h["out"], None)
            if other is None:
                done[h["out"]] = o_t
            else:
                lo, hi = (other, o_t) if h["half"] == 1 else (o_t, other)
                o_ref[:, col(h["out"])] = jnp.concatenate([lo, hi], axis=0).T.astype(BF16)
            return
        for h in job["heads"]:
            r = h.pop("r")
            if h["sink"] is not None and h["half"] == 1:
                o = swap(r) / (r + h["e"])
            else:
                o = r / (swap(r) if h["e"] is None else swap(r) + h["e"])
            other = done.pop(h["out"], None)
            if other is None:
                done[h["out"]] = o
            else:
                lo, hi = (other, o) if h["half"] == 1 else (o, other)
                o_ref[:, col(h["out"])] = jnp.where(lt64, lo, hi).astype(BF16)

    done = {}
    d1, d2 = depth[0], depth[0] + depth[1]
    for t in range(len(jobs) + d2):
        if t < len(jobs):
            scores(jobs[t])
        if d1 <= t < len(jobs) + d1:
            values(jobs[t - d1])
        if t >= d2:
            finish(jobs[t - d2], done)


def _attn_lat_kernel(sink_ref, qa_ref, qm_ref, kd_ref, vd_ref, km_ref, vm_ref,
                     kdc_ref, vdc_ref, kmc_ref, vmc_ref, o_ref):
    j = pl.program_id(1)
    q0 = j * BQ
    nloc = BQ + 2 * WINDOW
    start = pl.multiple_of(jnp.clip(q0 - WINDOW, 0, DEC_SEQ - nloc), WINDOW)
    qpos = q0 + lax.broadcasted_iota(jnp.int32, (BQ, nloc), 0)
    kpos = start + lax.broadcasted_iota(jnp.int32, (BQ, nloc), 1)
    valid = jnp.abs(qpos - kpos) <= WINDOW
    everything = slice(None)
    a_segs = [(kd_ref, vd_ref, pl.ds(start, nloc), valid), (kdc_ref, vdc_ref, everything, None)]
    m_segs = [(km_ref, vm_ref), (kmc_ref, vmc_ref)]
    _attn_heads(qa_ref, qm_ref, a_segs, m_segs, sink_ref, o_ref, BQ, depth=(2, 0))


def _attn_ctx_kernel(sink_ref, qa_ref, qm_ref, kd_ref, vd_ref, km_ref, vm_ref, o_ref):
    a_segs = [(kd_ref, vd_ref, slice(None), None)]
    m_segs = [(km_ref, vm_ref)]
    _attn_heads(qa_ref, qm_ref, a_segs, m_segs, sink_ref, o_ref, SEQ, depth=(1, 0))


def _attention(sink2, qa, kd, vd, qm, km, vm, kdc, vdc, kmc, vmc, layer_i):
    smem = pl.BlockSpec(memory_space=pltpu.SMEM)
    nq = DEC_SEQ // BQ
    q_off = T_CTX // BQ
    s_off = T_CTX // DEC_SEQ
    q_map = lambda b, j: (q_off + b * nq + j, 0)
    kv_map = lambda b, j: (s_off + b, 0)
    c_map = lambda b, j: (layer_i, b, 0, 0)
    o_lat = pl.pallas_call(
        _attn_lat_kernel,
        grid=(DEC_BATCH, nq),
        in_specs=[
            smem,
            pl.BlockSpec((BQ, 512), q_map),
            pl.BlockSpec((BQ, 1024), q_map),
            pl.BlockSpec((DEC_SEQ, 256), kv_map),
            pl.BlockSpec((DEC_SEQ, 256), kv_map),
            pl.BlockSpec((DEC_SEQ, 1024), kv_map),
            pl.BlockSpec((B_HEADS * VT_ROWS, DEC_SEQ), lambda b, j: (0, s_off + b)),
            pl.BlockSpec((None, None, PAST_LEN, 256), c_map),
            pl.BlockSpec((None, None, PAST_LEN, 256), c_map),
            pl.BlockSpec((None, None, PAST_LEN, 1024), c_map),
            pl.BlockSpec((None, None, B_HEADS * VT_ROWS, PAST_LEN), c_map),
        ],
        out_specs=pl.BlockSpec((BQ, 1024), lambda b, j: (b * nq + j, 0)),
        out_shape=jax.ShapeDtypeStruct((T_LAT, 1024), BF16),
        compiler_params=_cparams(("arbitrary", "arbitrary")),
        name="attn_latent",
    )(sink2, qa, qm, kd, vd, km, vm, kdc, vdc, kmc, vmc)
    b_map = lambda b: (b, 0)
    o_ctx = pl.pallas_call(
        _attn_ctx_kernel,
        grid=(BATCH,),
        in_specs=[
            smem,
            pl.BlockSpec((SEQ, 512), b_map),
            pl.BlockSpec((SEQ, 1024), b_map),
            pl.BlockSpec((SEQ, 256), b_map),
            pl.BlockSpec((SEQ, 256), b_map),
            pl.BlockSpec((SEQ, 1024), b_map),
            pl.BlockSpec((B_HEADS * VT_ROWS, SEQ), lambda b: (0, b)),
        ],
        out_specs=pl.BlockSpec((SEQ, 1024), b_map),
        out_shape=jax.ShapeDtypeStruct((T_CTX, 1024), BF16),
        compiler_params=_cparams(("arbitrary",)),
        name="attn_context",
    )(sink2, qa, qm, kd, vd, km, vm)
    return (o_ctx, o_lat)


def _conv_in_kernel(x_ref, mod_ref, g_ref, w_ref, u_ref, z_ref):
    for rows in _sub_rows():
        h = _rms_mod(x_ref[rows, :], g_ref[...], mod_ref[0:1, :], mod_ref[1:2, :])
        y = _dot(h.astype(BF16), w_ref[...])
        a = y[:, 0:CONV_CH]
        gate = y[:, CONV_CH:2 * CONV_CH]
        u_ref[rows, :] = a * jax.nn.sigmoid(gate)
        z_ref[rows, :] = y[:, 2 * CONV_CH:]


def _conv_in(x, mod, g, w_in, layer):
    tile_map = lambda i: (i, 0)
    const2 = lambda i: (0, 0)
    return pl.pallas_call(
        _conv_in_kernel,
        grid=(N_TOK,),
        in_specs=[
            pl.BlockSpec((TOK, D_MODEL), tile_map),
            pl.BlockSpec((None, None, 6, D_MODEL), lambda i: (layer, _mod_row(i), 0, 0)),
            pl.BlockSpec((1, D_MODEL), const2),
            pl.BlockSpec((D_MODEL, 3 * CONV_CH), const2),
        ],
        out_specs=[pl.BlockSpec((TOK, CONV_CH), tile_map), pl.BlockSpec((TOK, POOL_CH), tile_map)],
        out_shape=[jax.ShapeDtypeStruct((T_ALL, CONV_CH), F32), jax.ShapeDtypeStruct((T_ALL, POOL_CH), F32)],
        compiler_params=_cparams(("arbitrary",)),
        name="conv_in",
    )(x, mod, g, w_in)


CONV_ROWS = 32


def _conv_mix_kernel(u_ref, up_ref, un_ref, z_ref, zp_ref, zn_ref, wdw_ref, bdw_ref, lng_ref, lnb_ref,
                     band_ref, wg_ref, ps_ref, o_ref, ue_ref, ze_ref):
    i = pl.program_id(0)
    j = (i - N_CTX_TILES) % LAT_TILES_PER_SEQ
    is_lat = i >= N_CTX_TILES
    has_prev = jnp.logical_and(is_lat, j > 0)
    has_next = jnp.logical_and(is_lat, j < LAT_TILES_PER_SEQ - 1)
    seq_len = jnp.where(is_lat, DEC_SEQ, SEQ)
    t0 = jnp.where(is_lat, j * TILE, 0)

    ue_ref[0, 0:HALO, :] = jnp.where(has_prev, up_ref[...], 0.0)
    ue_ref[0, HALO:HALO + TILE, :] = u_ref[...]
    ue_ref[0, HALO + TILE:, :] = jnp.where(has_next, un_ref[...], 0.0)
    ze_ref[0:HALO, :] = jnp.where(has_prev, zp_ref[...], 0.0)
    ze_ref[HALO:HALO + TILE, :] = z_ref[...]
    ze_ref[HALO + TILE:, :] = jnp.where(has_next, zn_ref[...], 0.0)

    pad = CONV_WIDTH // 2
    reach = (HALO - pad + CONV_WIDTH - 1) // SUBLANES * SUBLANES
    groups = CONV_ROWS // SUBLANES
    n_chunks = TILE // CONV_ROWS

    def shift_rows(lo, hi):
        for b in range(1, SUBLANES):
            ue_ref[b, lo:hi, :] = ue_ref[0, lo + b:hi + b, :]

    def conv_chunk(r):
        r0 = r * CONV_ROWS
        acc = jnp.zeros((groups, SUBLANES, CONV_CH), F32) + bdw_ref[...]
        for k in range(CONV_WIDTH):
            a, b = divmod(HALO + k - pad, SUBLANES)
            e0 = r0 + a * SUBLANES
            acc = acc + ue_ref[b, e0:e0 + CONV_ROWS, :].reshape(groups, SUBLANES, CONV_CH) * wdw_ref[k]
        acc = acc.reshape(CONV_ROWS, CONV_CH)
        mu = jnp.mean(acc, axis=-1, keepdims=True)
        d = acc - mu
        var = jnp.mean(d * d, axis=-1, keepdims=True)
        yn = d * lax.rsqrt(var + EPS) * lng_ref[...] + lnb_ref[...]
        o_ref[r0:r0 + CONV_ROWS, 0:CONV_CH] = (yn * jax.nn.sigmoid(yn)).astype(BF16)

    t = t0 + lax.broadcasted_iota(jnp.int32, (TILE, POOL_GROUP_W), 0)
    ze = ze_ref[...]
    z_hi = ze.astype(BF16)
    z_lo = (ze - z_hi.astype(F32)).astype(BF16)

    def pool_group(gi):
        w = POOL_SIZES[gi]
        lo = w // 2
        hi = w - lo - 1
        cols = slice(gi * POOL_GROUP_W, (gi + 1) * POOL_GROUP_W)
        tot = _dot(band_ref[gi], z_hi[:, cols]) + _dot(band_ref[gi], z_lo[:, cols])
        cnt = jnp.minimum(t + hi + 1, seq_len) - jnp.maximum(t - lo, 0)
        d = tot / cnt.astype(F32) - ze_ref[HALO:HALO + TILE, cols]
        y = _dot(d.astype(BF16), wg_ref[gi]) * ps_ref[:, cols]
        o_ref[:, CONV_CH + gi * POOL_GROUP_W:CONV_CH + (gi + 1) * POOL_GROUP_W] = y.astype(BF16)

    shift_rows(0, reach + CONV_ROWS)
    for r in range(n_chunks):
        if r + 1 < n_chunks:
            shift_rows(reach + (r + 1) * CONV_ROWS, reach + (r + 2) * CONV_ROWS)
        conv_chunk(r)
        if r % (n_chunks // len(POOL_SIZES)) == 0:
            pool_group(r // (n_chunks // len(POOL_SIZES)))


def _conv_mix(u, z, cp):
    tile_map = lambda i: (i, 0)
    const2 = lambda i: (0, 0)
    hb = TILE // HALO
    prev_map = lambda i: (jnp.maximum(i * hb - 1, 0), 0)
    next_map = lambda i: (jnp.minimum((i + 1) * hb, T_ALL // HALO - 1), 0)
    return pl.pallas_call(
        _conv_mix_kernel,
        grid=(N_TILES,),
        in_specs=[
            pl.BlockSpec((TILE, CONV_CH), tile_map),
            pl.BlockSpec((HALO, CONV_CH), prev_map),
            pl.BlockSpec((HALO, CONV_CH), next_map),
            pl.BlockSpec((TILE, POOL_CH), tile_map),
            pl.BlockSpec((HALO, POOL_CH), prev_map),
            pl.BlockSpec((HALO, POOL_CH), next_map),
            pl.BlockSpec((CONV_WIDTH, SUBLANES, CONV_CH), lambda i: (0, 0, 0)),
            pl.BlockSpec((1, CONV_CH), const2),
            pl.BlockSpec((1, CONV_CH), const2),
            pl.BlockSpec((1, CONV_CH), const2),
            pl.BlockSpec((len(POOL_SIZES), TILE, TILE + 2 * HALO), lambda i: (0, 0, 0)),
            pl.BlockSpec((len(POOL_SIZES), POOL_GROUP_W, POOL_GROUP_W), lambda i: (0, 0, 0)),
            pl.BlockSpec((1, POOL_CH), const2),
        ],
        out_specs=pl.BlockSpec((TILE, D_MODEL), tile_map),
        out_shape=jax.ShapeDtypeStruct((T_ALL, D_MODEL), BF16),
        scratch_shapes=[pltpu.VMEM((SUBLANES, TILE + 2 * HALO, CONV_CH), F32),
                        pltpu.VMEM((TILE + 2 * HALO, POOL_CH), F32)],
        compiler_params=_cparams(("arbitrary",)),
        name="conv_mix",
    )(u, u, u, z, z, z, cp["w_dw"], cp["b_dw"], cp["ln_g"], cp["ln_b"], _pool_bands(), cp["w_grp"], cp["p_scale"])


def _pool_bands():
    t = np.arange(TILE)[:, None]
    e = np.arange(TILE + 2 * HALO)[None, :] - HALO
    bands = []
    for w in POOL_SIZES:
        lo = w // 2
        hi = w - lo - 1
        bands.append((e >= t - lo) & (e <= t + hi))
    return jnp.asarray(np.stack(bands), BF16)


def _post_kernel(*refs, n_o, n_x, final):
    o_refs, x_refs = refs[:n_o], refs[n_o:n_o + n_x]
    mod_ref, g_ref, wo_ref, w1_ref, w2_ref, fg_ref = refs[n_o + n_x:n_o + n_x + 6]
    y_refs = refs[n_o + n_x + 6:]
    outs = []
    for rows in _sub_rows():
        x1 = _tok_load(x_refs, rows) + mod_ref[2:3, :] * _dot(_tok_load(o_refs, rows), wo_ref[...])
        h = _rms_mod(x1, g_ref[...], mod_ref[3:4, :], mod_ref[4:5, :])
        a = jnp.maximum(_dot(h.astype(BF16), w1_ref[...]), 0.0)
        x2 = x1 + mod_ref[5:6, :] * _dot((a * a).astype(BF16), w2_ref[...])
        if final:
            ms = jnp.mean(x2 * x2, axis=-1, keepdims=True)
            x2 = x2 * lax.rsqrt(ms + EPS) * fg_ref[...]
        outs.append((rows, x2))
    _tok_store(y_refs, outs)


def _post(o, x, mod, g, w_out, w1, w2, final_g, layer, final):
    const2 = lambda i: (0, 0)
    o_ops, o_specs = _tok_specs(o, D_MODEL)
    x_ops, x_specs = _tok_specs(x, D_MODEL)
    if final:
        y = (jax.ShapeDtypeStruct((T_CTX, D_MODEL), F32), jax.ShapeDtypeStruct((T_LAT, D_MODEL), F32))
    else:
        y = jax.ShapeDtypeStruct((T_ALL, D_MODEL), F32)
    _, y_specs = _tok_specs(y, D_MODEL)
    out = pl.pallas_call(
        functools.partial(_post_kernel, n_o=len(o_ops), n_x=len(x_ops), final=final),
        grid=(N_TOK,),
        in_specs=o_specs + x_specs + [
            pl.BlockSpec((None, None, 6, D_MODEL), lambda i: (layer, _mod_row(i), 0, 0)),
            pl.BlockSpec((1, D_MODEL), const2),
            pl.BlockSpec((D_MODEL, D_MODEL), const2, pipeline_mode=pl.Buffered(1)),
            pl.BlockSpec((None, D_MODEL, D_FF), lambda i: (layer, 0, 0), pipeline_mode=pl.Buffered(1)),
            pl.BlockSpec((None, D_FF, D_MODEL), lambda i: (layer, 0, 0), pipeline_mode=pl.Buffered(1)),
            pl.BlockSpec((1, D_MODEL), const2),
        ],
        out_specs=y_specs if final else y_specs[0],
        out_shape=y,
        compiler_params=_cparams(("arbitrary",)),
        name="post_final" if final else "post",
    )(*o_ops, *x_ops, mod, g, w_out, w1, w2, final_g)
    return out


def _pair_lanes(w):
    lead = w.shape[:-1]
    w = w.reshape(lead + (w.shape[-1] // LANES, 2, 2, HEAD_DIM // 2))
    return jnp.swapaxes(w, -3, -2).reshape(lead + (-1,))


def _mla_lanes(nope, rope):
    shape = (nope if nope is not None else rope).shape[:-1]
    nope = jnp.zeros(shape + (QK_NOPE,), F32) if nope is None else nope
    rope = jnp.zeros(shape + (QK_ROPE,), F32) if rope is None else rope
    h = QK_ROPE // 2
    return jnp.concatenate([rope[..., :h], nope[..., :64 - h], rope[..., h:], nope[..., 64 - h:],
                            jnp.zeros(shape + (LANES - QK_NOPE - QK_ROPE,), F32)], axis=-1)


def _rope_tables():
    n = DEC_SEQ
    rows = n // GRID_W
    row = jnp.repeat(jnp.arange(rows), GRID_W).astype(F32)
    col = jnp.tile(jnp.arange(GRID_W), rows).astype(F32)

    def angles(dim):
        quarter = dim // 4
        inv_freq = ROPE_BASE ** (-jnp.arange(quarter, dtype=F32) / quarter)
        return jnp.concatenate([row[:, None] * inv_freq, col[:, None] * inv_freq], axis=-1)

    ang_a = angles(HEAD_DIM)
    cos_a, sin_a = jnp.cos(ang_a), jnp.sin(ang_a)
    c_a = jnp.concatenate([cos_a] * 4, axis=-1)
    s_a = jnp.concatenate([-sin_a, -sin_a, sin_a, sin_a], axis=-1)
    ang_b = angles(QK_ROPE)
    cos_b, sin_b = jnp.cos(ang_b), jnp.sin(ang_b)
    c_b = _mla_lanes(jnp.ones((n, QK_NOPE), F32), jnp.concatenate([cos_b, cos_b], axis=-1))
    s_b = _mla_lanes(None, jnp.concatenate([-sin_b, sin_b], axis=-1))
    lat = jnp.stack([c_a, s_a, c_b, s_b])
    ident = jnp.stack([jnp.ones((TOK, LANES), F32), jnp.zeros((TOK, LANES), F32)] * 2)
    return jnp.concatenate([ident, lat], axis=1)


def _prep_even(attn_w_in, mla_q_norm, mla_kv_norm, mla_w_qb, mla_w_kvb, attn_w_out, i):
    w = attn_w_in[i]
    o = np.cumsum((0, 512, 128, 128, Q_LORA, KV_LORA, QK_ROPE))
    zeros = lambda n: jnp.zeros((D_MODEL, n), F32)
    w_in = jnp.concatenate([
        _pair_lanes(w[:, o[0]:o[1]]), _pair_lanes(w[:, o[1]:o[2]]), w[:, o[2]:o[3]],
        w[:, o[3]:o[4]], zeros(Q_LORA_PAD - Q_LORA),
        w[:, o[4]:o[5]],
        _mla_lanes(None, w[:, o[5]:o[6]]),
    ], axis=1).astype(BF16)
    q_norm = jnp.pad(mla_q_norm[i], (0, Q_LORA_PAD - Q_LORA)).reshape(1, Q_LORA_PAD)
    wqb = mla_w_qb[i].reshape(Q_LORA, B_HEADS, QK_NOPE + QK_ROPE)
    wqb = jnp.pad(_mla_lanes(wqb[..., :QK_NOPE], wqb[..., QK_NOPE:]), ((0, Q_LORA_PAD - Q_LORA), (0, 0), (0, 0)))
    wkvb = mla_w_kvb[i].reshape(KV_LORA, B_HEADS, QK_NOPE + V_DIM)
    w_kk = _mla_lanes(wkvb[:, :, :QK_NOPE], None)
    w_kv = wkvb[:, :, QK_NOPE:]
    return {
        "w_in": w_in,
        "q_norm": q_norm,
        "kv_norm": mla_kv_norm[i].reshape(1, KV_LORA),
        "w_qb": wqb.reshape(Q_LORA_PAD, B_HEADS * LANES).astype(BF16),
        "w_kk": w_kk.reshape(KV_LORA, B_HEADS * LANES).astype(BF16),
        "w_kv": w_kv.reshape(KV_LORA, B_HEADS * V_DIM).T.astype(BF16),
        "w_out": attn_w_out[i].astype(BF16),
    }


def kernel(x_prompt, x_sample, cache_win_k, cache_win_v, cache_mla_ckv, cache_mla_krope, c, c_ctx, w_mod, b_mod,
           norm_g, attn_w_in, attn_sink, mla_q_norm, mla_kv_norm, mla_w_qb, mla_w_kvb, attn_w_out, conv_w_in,
           conv_dw, conv_dw_b, conv_ln_g, conv_ln_b, pool_w, pool_scale, conv_w_out, mlp_w1, mlp_w2, final_g):
    x = (x_prompt.reshape(T_CTX, D_MODEL), x_sample.reshape(T_LAT, D_MODEL))

    cond = jnp.concatenate([c_ctx[None, :], c, jnp.zeros((N_COND - 1 - DEC_BATCH, D_MODEL), F32)], axis=0)
    mod = _mod_table(cond, w_mod, b_mod).reshape(DEPTH, N_COND, 6, D_MODEL)

    rope = _rope_tables()
    even = [_prep_even(attn_w_in, mla_q_norm, mla_kv_norm, mla_w_qb, mla_w_kvb, attn_w_out, i)
            for i in range(N_EVEN)]
    cache_kr128 = _mla_lanes(None, cache_mla_krope)
    kdc, vdc, kmc, vmc = _ctx_kv(
        _pair_lanes(cache_win_k.reshape(DEC_BATCH, N_EVEN, PAST_LEN, LANES)),
        cache_win_v.reshape(DEC_BATCH, N_EVEN, PAST_LEN, LANES),
        cache_mla_ckv, cache_kr128,
        jnp.stack([e["w_kk"] for e in even]), jnp.stack([e["w_kv"] for e in even]))
    final_g2 = final_g.reshape(1, D_MODEL)
    w1_all, w2_all = mlp_w1.astype(BF16), mlp_w2.astype(BF16)

    ks, vs, ckvs, krs = [], [], [], []
    for l in range(DEPTH):
        g_mix = norm_g[l, 0].reshape(1, D_MODEL)
        g_mlp = norm_g[l, 1].reshape(1, D_MODEL)
        if l % 2 == 0:
            i = l // 2
            wp = even[i]
            qa, kd, vd, qm, km, vm, ka32, va32, ckv32, kr32 = _attn_in(x, mod, g_mix, wp, rope, l)
            ka32 = jnp.swapaxes(ka32.reshape(T_CTX, 2, A_KV_HEADS, HEAD_DIM // 2), 1, 2)
            ks.append(ka32.reshape(BATCH, SEQ, A_KV_HEADS, HEAD_DIM))
            vs.append(va32.reshape(BATCH, SEQ, A_KV_HEADS, HEAD_DIM))
            ckvs.append(ckv32.reshape(BATCH, SEQ, KV_LORA))
            kr32 = jnp.concatenate([kr32[:, :QK_ROPE // 2], kr32[:, 64:64 + QK_ROPE // 2]], axis=-1)
            krs.append(kr32.reshape(BATCH, SEQ, QK_ROPE))
            o = _attention(attn_sink[i] * LOG2E, qa, kd, vd, qm, km, vm, kdc, vdc, kmc, vmc, i)
            w_out = wp["w_out"]
        else:
            jj = l // 2
            u, z = _conv_in(x, mod, g_mix, conv_w_in[jj].astype(BF16), l)
            cp = {
                "w_dw": jnp.broadcast_to(conv_dw[jj][:, None, :], (CONV_WIDTH, SUBLANES, CONV_CH)),
                "b_dw": conv_dw_b[jj].reshape(1, CONV_CH),
                "ln_g": conv_ln_g[jj].reshape(1, CONV_CH),
                "ln_b": conv_ln_b[jj].reshape(1, CONV_CH),
                "w_grp": pool_w[jj].astype(BF16),
                "p_scale": pool_scale[jj].reshape(1, POOL_CH),
            }
            o = _conv_mix(u, z, cp)
            w_out = conv_w_out[jj].astype(BF16)
        x = _post(o, x, mod, g_mlp, w_out, w1_all, w2_all, final_g2, l, final=(l == DEPTH - 1))

    y_prompt = x[0].reshape(BATCH, SEQ, D_MODEL)
    y_sample = x[1].reshape(DEC_BATCH, DEC_SEQ, D_MODEL)
    return (y_prompt, y_sample, jnp.stack(ks, axis=1), jnp.stack(vs, axis=1),
            jnp.stack(ckvs, axis=1), jnp.stack(krs, axis=1))
```

```python
import functools
import math

import numpy as np
import jax
import jax.numpy as jnp
from jax import lax
from jax.experimental import pallas as pl
from jax.experimental.pallas import tpu as pltpu

F32 = jnp.float32
BF16 = jnp.bfloat16

D_MODEL = 1024
BATCH = 16
SEQ = 256
DEPTH = 4
DEC_BATCH = 8
DEC_SEQ = 2048
PAST_LEN = 256
GRID_W = 64
N_EVEN = (DEPTH + 1) // 2
N_ODD = DEPTH // 2
A_HEADS = 8
A_KV_HEADS = 2
A_GROUP = A_HEADS // A_KV_HEADS
HEAD_DIM = 64
WINDOW = 128
B_HEADS = 8
Q_LORA = 192
KV_LORA = 128
QK_NOPE = 64
QK_ROPE = 32
V_DIM = 64
MLA_SCALE = (QK_NOPE + QK_ROPE) ** -0.5
CONV_CH = D_MODEL // 2
CONV_WIDTH = 31
POOL_CH = D_MODEL // 2
POOL_SIZES = (2, 4, 8, 16)
POOL_GROUP_W = POOL_CH // len(POOL_SIZES)
D_FF = 4 * D_MODEL
ROPE_BASE = 10000.0
EPS = 1e-6
NEG_INF = -1e30
LOG2E = math.log2(math.e)

LANES = 128
SUBLANES = 8
VMEM_LIMIT_BYTES = 56 * 1024 * 1024

TILE = 256
T_CTX = BATCH * SEQ
T_LAT = DEC_BATCH * DEC_SEQ
T_ALL = T_CTX + T_LAT
N_CTX_TILES = T_CTX // TILE
N_TILES = T_ALL // TILE
LAT_TILES_PER_SEQ = DEC_SEQ // TILE
TOK = 512
N_CTX_TOK = T_CTX // TOK
N_TOK = T_ALL // TOK
LAT_TOK_PER_SEQ = DEC_SEQ // TOK
N_COND = 16
BQ = 256
VT_ROWS = 80
HALO = 16
Q_LORA_PAD = 256
ATTN_IN_COLS = 512 + 128 + 128 + Q_LORA_PAD + 128 + 128


def _cparams(sem):
    return pltpu.CompilerParams(dimension_semantics=sem, vmem_limit_bytes=VMEM_LIMIT_BYTES)


def _mod_row(i):
    return jnp.where(i < N_CTX_TOK, 0, 1 + (i - N_CTX_TOK) // LAT_TOK_PER_SEQ)


def _pos_block(i):
    return jnp.where(i < N_CTX_TOK, 0, 1 + (i - N_CTX_TOK) % LAT_TOK_PER_SEQ)


def _tok_specs(a, width):
    if isinstance(a, tuple):
        return list(a), [pl.BlockSpec((TOK, width), lambda i: (jnp.minimum(i, N_CTX_TOK - 1), 0)),
                         pl.BlockSpec((TOK, width), lambda i: (jnp.maximum(i - N_CTX_TOK, 0), 0))]
    return [a], [pl.BlockSpec((TOK, width), lambda i: (i, 0))]


def _tok_load(refs, rows):
    if len(refs) == 1:
        return refs[0][rows, :]
    return jnp.where(pl.program_id(0) < N_CTX_TOK, refs[0][rows, :], refs[1][rows, :])


def _tok_store(refs, parts):
    def put(ref):
        for rows, val in parts:
            ref[rows, :] = val

    if len(refs) == 1:
        put(refs[0])
        return
    i = pl.program_id(0)

    @pl.when(i < N_CTX_TOK)
    def _():
        put(refs[0])

    @pl.when(i >= N_CTX_TOK)
    def _():
        put(refs[1])


def _sub_rows():
    return [slice(s * TILE, (s + 1) * TILE) for s in range(TOK // TILE)]


def _dot(a, b):
    return jnp.dot(a, b, preferred_element_type=F32)


def _dot_nt(a, b):
    return lax.dot_general(a, b, (((1,), (1,)), ((), ())), preferred_element_type=F32)


def _rms_mod(x, g, shift, scale):
    ms = jnp.mean(x * x, axis=-1, keepdims=True)
    return x * lax.rsqrt(ms + EPS) * (g * (1.0 + scale)) + shift


def _rope(x, c, s):
    return x * c + pltpu.roll(x, 64, 1) * s


def _lane_lt64(shape):
    return lax.broadcasted_iota(jnp.int32, shape, len(shape) - 1) < 64


def _lane_even32(shape):
    return (lax.broadcasted_iota(jnp.int32, shape, len(shape) - 1) & 63) < 32


def _store_kv_dup(kd_ref, vd_ref, rows, k, v):
    even32 = _lane_even32(k.shape)
    lt64 = _lane_lt64(v.shape)
    kd_ref[rows, 0:LANES] = jnp.where(even32, k, pltpu.roll(k, 32, 1)).astype(BF16)
    kd_ref[rows, LANES:2 * LANES] = jnp.where(even32, pltpu.roll(k, LANES - 32, 1), k).astype(BF16)
    vd_ref[rows, 0:LANES] = jnp.where(lt64, v, 1.0).astype(BF16)
    vd_ref[rows, LANES:2 * LANES] = jnp.where(lt64, pltpu.roll(v, 64, 1), 1.0).astype(BF16)


def _store_vt(vt_ref, cols, ckvn, wkvt_ref):
    vt = _dot(wkvt_ref[...], ckvn.T.astype(BF16))
    ones = jnp.ones((VT_ROWS - V_DIM, ckvn.shape[0]), BF16)
    for hh in range(B_HEADS):
        vt_ref[hh * VT_ROWS:hh * VT_ROWS + V_DIM, cols] = vt[hh * V_DIM:(hh + 1) * V_DIM].astype(BF16)
        vt_ref[hh * VT_ROWS + V_DIM:(hh + 1) * VT_ROWS, cols] = ones


def _mod_kernel(cond_ref, w_ref, b_ref, o_ref):
    c = cond_ref[...]
    s = c * jax.nn.sigmoid(c)
    o_ref[...] = _dot(s.astype(BF16), w_ref[...].astype(BF16)) + b_ref[...]


def _mod_table(cond, w_mod, b_mod):
    nb = 6 * D_MODEL // 1024
    return pl.pallas_call(
        _mod_kernel,
        grid=(DEPTH, nb),
        in_specs=[
            pl.BlockSpec((N_COND, D_MODEL), lambda l, n: (0, 0)),
            pl.BlockSpec((None, D_MODEL, 1024), lambda l, n: (l, 0, n)),
            pl.BlockSpec((None, 1, 1024), lambda l, n: (l, 0, n)),
        ],
        out_specs=pl.BlockSpec((None, N_COND, 1024), lambda l, n: (l, 0, n)),
        out_shape=jax.ShapeDtypeStruct((DEPTH, N_COND, 6 * D_MODEL), F32),
        compiler_params=_cparams(("arbitrary", "arbitrary")),
        name="mod_table",
    )(cond, w_mod, b_mod.reshape(DEPTH, 1, 6 * D_MODEL))


def _attn_in_kernel(*refs, n_x):
    x_refs = refs[:n_x]
    (mod_ref, g_ref, w_ref, qn_ref, kvn_ref, wqb_ref, wkk_ref, wkv_ref, rope_ref,
     qa_ref, kd_ref, vd_ref, qm_ref, km_ref, vm_ref, ka32_ref, va32_ref, ckv32_ref, kr32_ref) = refs[n_x:]
    col = lambda n: slice(n * LANES, (n + 1) * LANES)

    def project(rows):
        h = _rms_mod(_tok_load(x_refs, rows), g_ref[...], mod_ref[0:1, :], mod_ref[1:2, :])
        return _dot(h.astype(BF16), w_ref[...])

    def derive(rows, y):
        ca, sa = rope_ref[0, rows, :], rope_ref[1, rows, :]
        cb, sb = rope_ref[2, rows, :], rope_ref[3, rows, :]

        for c in range(4):
            q = _rope(y[:, col(c)], ca, sa)
            qa_ref[rows, col(c)] = (q * (HEAD_DIM ** -0.5 * LOG2E)).astype(BF16)

        ka = _rope(y[:, 512:640], ca, sa)
        va = y[:, 640:768]
        _store_kv_dup(kd_ref, vd_ref, rows, ka, va)

        cq = y[:, 768:768 + Q_LORA_PAD]
        cqn = cq * lax.rsqrt(jnp.sum(cq * cq, axis=-1, keepdims=True) * (1.0 / Q_LORA) + EPS) * qn_ref[...]
        qm = _dot(cqn.astype(BF16), wqb_ref[...])
        for hh in range(B_HEADS):
            q = _rope(qm[:, col(hh)], cb, sb)
            qm_ref[rows, col(hh)] = (q * (MLA_SCALE * LOG2E)).astype(BF16)

        ckv = y[:, 1024:1152]
        ckvn = ckv * lax.rsqrt(jnp.mean(ckv * ckv, axis=-1, keepdims=True) + EPS) * kvn_ref[...]
        kr = y[:, 1152:1280]
        ckvn_b = ckvn.astype(BF16)
        kn = _dot(ckvn_b, wkk_ref[...])
        krr = _rope(kr, cb, sb)
        for hh in range(B_HEADS):
            km_ref[rows, col(hh)] = (kn[:, col(hh)] + krr).astype(BF16)
        _store_vt(vm_ref, rows, ckvn, wkv_ref)
        return ka, va, ckvn, kr

    subs = _sub_rows()
    cache = []
    y = project(subs[0])
    for s, rows in enumerate(subs):
        y_next = project(subs[s + 1]) if s + 1 < len(subs) else None
        cache.append(derive(rows, y))
        y = y_next

    @pl.when(pl.program_id(0) < N_CTX_TOK)
    def _():
        for rows, (ka, va, ckvn, kr) in zip(subs, cache):
            ka32_ref[rows, :] = ka
            va32_ref[rows, :] = va
            ckv32_ref[rows, :] = ckvn
            kr32_ref[rows, :] = kr


def _attn_in(x, mod, g, wp, rope, layer):
    tile_map = lambda i: (i, 0)
    const2 = lambda i: (0, 0)
    ctx_map = lambda i: (jnp.minimum(i, N_CTX_TOK - 1), 0)
    bf = lambda w: jax.ShapeDtypeStruct((T_ALL, w), BF16)
    c32 = jax.ShapeDtypeStruct((T_CTX, LANES), F32)
    x_ops, x_specs = _tok_specs(x, D_MODEL)
    return pl.pallas_call(
        functools.partial(_attn_in_kernel, n_x=len(x_ops)),
        grid=(N_TOK,),
        in_specs=x_specs + [
            pl.BlockSpec((None, None, 6, D_MODEL), lambda i: (layer, _mod_row(i), 0, 0)),
            pl.BlockSpec((1, D_MODEL), const2),
            pl.BlockSpec((D_MODEL, ATTN_IN_COLS), const2),
            pl.BlockSpec((1, Q_LORA_PAD), const2),
            pl.BlockSpec((1, KV_LORA), const2),
            pl.BlockSpec((Q_LORA_PAD, B_HEADS * LANES), const2),
            pl.BlockSpec((KV_LORA, B_HEADS * LANES), const2),
            pl.BlockSpec((B_HEADS * V_DIM, KV_LORA), const2),
            pl.BlockSpec((4, TOK, LANES), lambda i: (0, _pos_block(i), 0)),
        ],
        out_specs=[
            pl.BlockSpec((TOK, 512), tile_map),
            pl.BlockSpec((TOK, 256), tile_map),
            pl.BlockSpec((TOK, 256), tile_map),
            pl.BlockSpec((TOK, 1024), tile_map),
            pl.BlockSpec((TOK, 1024), tile_map),
            pl.BlockSpec((B_HEADS * VT_ROWS, TOK), lambda i: (0, i)),
            pl.BlockSpec((TOK, LANES), ctx_map),
            pl.BlockSpec((TOK, LANES), ctx_map),
            pl.BlockSpec((TOK, LANES), ctx_map),
            pl.BlockSpec((TOK, LANES), ctx_map),
        ],
        out_shape=[bf(512), bf(256), bf(256), bf(1024), bf(1024),
                   jax.ShapeDtypeStruct((B_HEADS * VT_ROWS, T_ALL), BF16), c32, c32, c32, c32],
        compiler_params=_cparams(("arbitrary",)),
        name="attn_in",
    )(*x_ops, mod, g, wp["w_in"], wp["q_norm"], wp["kv_norm"], wp["w_qb"], wp["w_kk"], wp["w_kv"], rope)


def _ctx_kv_kernel(ck_ref, cv_ref, cckv_ref, ckr_ref, wkk_ref, wkv_ref, kd_ref, vd_ref, km_ref, vm_ref):
    _store_kv_dup(kd_ref, vd_ref, slice(None), ck_ref[...], cv_ref[...])
    cb = cckv_ref[...].astype(BF16)
    kn = _dot(cb, wkk_ref[...])
    kr = ckr_ref[...]
    for hh in range(B_HEADS):
        km_ref[:, hh * LANES:(hh + 1) * LANES] = (kn[:, hh * LANES:(hh + 1) * LANES] + kr).astype(BF16)
    _store_vt(vm_ref, slice(None), cckv_ref[...], wkv_ref)


def _ctx_kv(cache_k, cache_v, cache_ckv, cache_kr128, w_kk, w_kv):
    cache_map = lambda i, b: (b, i, 0, 0)
    w_map = lambda i, b: (i, 0, 0)
    out_map = lambda i, b: (i, b, 0, 0)
    o = lambda w: jax.ShapeDtypeStruct((N_EVEN, DEC_BATCH, PAST_LEN, w), BF16)
    return pl.pallas_call(
        _ctx_kv_kernel,
        grid=(N_EVEN, DEC_BATCH),
        in_specs=[
            pl.BlockSpec((None, None, PAST_LEN, LANES), cache_map),
            pl.BlockSpec((None, None, PAST_LEN, LANES), cache_map),
            pl.BlockSpec((None, None, PAST_LEN, KV_LORA), cache_map),
            pl.BlockSpec((None, None, PAST_LEN, LANES), cache_map),
            pl.BlockSpec((None, KV_LORA, B_HEADS * LANES), w_map),
            pl.BlockSpec((None, B_HEADS * V_DIM, KV_LORA), w_map),
        ],
        out_specs=[
            pl.BlockSpec((None, None, PAST_LEN, 256), out_map),
            pl.BlockSpec((None, None, PAST_LEN, 256), out_map),
            pl.BlockSpec((None, None, PAST_LEN, 1024), out_map),
            pl.BlockSpec((None, None, B_HEADS * VT_ROWS, PAST_LEN), out_map),
        ],
        out_shape=[o(256), o(256), o(1024),
                   jax.ShapeDtypeStruct((N_EVEN, DEC_BATCH, B_HEADS * VT_ROWS, PAST_LEN), BF16)],
        compiler_params=_cparams(("arbitrary", "arbitrary")),
        name="ctx_kv",
    )(cache_k, cache_v, cache_ckv, cache_kr128, w_kk, w_kv)


def _attn_heads(qa_ref, qm_ref, a_segs, m_segs, sink_ref, o_ref, rows, depth):
    lt64 = _lane_lt64((rows, LANES))
    lane = lax.broadcasted_iota(jnp.int32, (1, LANES), 1)
    keep = (((lane & 63) < 32).astype(BF16), ((lane & 63) >= 32).astype(BF16))
    col = lambda n: slice(n * LANES, (n + 1) * LANES)
    swap = lambda r: pltpu.roll(r, 64, 1)

    jobs = []
    for kh in range(A_KV_HEADS):
        heads = [dict(out=(kh * A_GROUP + g) // 2, half=g % 2, sink=kh * A_GROUP + g) for g in range(A_GROUP)]
        jobs.append(dict(
            heads=heads,
            q=lambda heads=heads: jnp.concatenate(
                [qa_ref[:, col(h["out"])] * keep[h["half"]] for h in heads], axis=0),
            ks=[lambda kd=kd, rs=rs, kh=kh: kd[rs, col(kh)] for kd, _, rs, _ in a_segs],
            vs=[lambda vd=vd, rs=rs, kh=kh: vd[rs, col(kh)] for _, vd, rs, _ in a_segs],
            valids=[valid for _, _, _, valid in a_segs]))
    for hh in range(B_HEADS):
        jobs.append(dict(
            keys_on_rows=True,
            heads=[dict(out=4 + hh // 2, half=hh % 2)],
            q=lambda hh=hh: qm_ref[:, col(hh)],
            ks=[lambda km=km, hh=hh: km[:, col(hh)] for km, _ in m_segs],
            vs=[lambda vmt=vmt, hh=hh: vmt[hh * VT_ROWS:(hh + 1) * VT_ROWS, :] for _, vmt in m_segs]))

    def scores(job):
        if job.get("keys_on_rows"):
            h = job["heads"][0]
            q = job["q"]()
            h["ss"] = [_dot_nt(k(), q) for k in job["ks"]]
            m = h["ss"][0].max(axis=0, keepdims=True)
            for s in h["ss"][1:]:
                m = jnp.maximum(m, s.max(axis=0, keepdims=True))
            h["m"] = m
            return
        full = [_dot_nt(job["q"](), k()) for k in job["ks"]]
        for b, h in enumerate(job["heads"]):
            blk = slice(b * rows, (b + 1) * rows)
            ss = [s[blk] if valid is None else jnp.where(valid, s[blk], NEG_INF)
                  for s, valid in zip(full, job["valids"])]
            m = ss[0].max(axis=-1, keepdims=True)
            for s in ss[1:]:
                m = jnp.maximum(m, s.max(axis=-1, keepdims=True))
            sink = sink_ref[h["sink"]]
            m = jnp.maximum(m, sink)
            h["e"] = jnp.exp2(sink - m)
            h["ss"], h["m"] = ss, m

    def values(job):
        if job.get("keys_on_rows"):
            h = job["heads"][0]
            r = None
            for s, vt in zip(h.pop("ss"), job["vs"]):
                rs_ = _dot(vt(), jnp.exp2(s - h["m"]).astype(BF16))
                r = rs_ if r is None else r + rs_
            del h["m"]
            h["r"] = r
            return
        r = None
        for i, v in enumerate(job["vs"]):
            p = [jnp.exp2(h["ss"][i] - h["m"]).astype(BF16) for h in job["heads"]]
            rs_ = _dot(jnp.concatenate(p, axis=0), v())
            r = rs_ if r is None else r + rs_
        for b, h in enumerate(job["heads"]):
            del h["ss"], h["m"]
            h["r"] = r[b * rows:(b + 1) * rows]

    def finish(job, done):
        if job.get("keys_on_rows"):
            h = job["heads"][0]
            r = h.pop("r")
            o_t = r[0:V_DIM] / r[V_DIM:V_DIM + 1]
            other = done.pop(h["out"], None)
            if other is None:
                done[h["out"]] = o_t
            else:
                lo, hi = (other, o_t) if h["half"] == 1 else (o_t, other)
                o_ref[:, col(h["out"])] = jnp.concatenate([lo, hi], axis=0).T.astype(BF16)
            return
        for h in job["heads"]:
            r = h.pop("r")
            if h["half"] == 1:
                o = swap(r) / (r + h["e"])
            else:
                o = r / (swap(r) + h["e"])
            other = done.pop(h["out"], None)
            if other is None:
                done[h["out"]] = o
            else:
                lo, hi = (other, o) if h["half"] == 1 else (o, other)
                o_ref[:, col(h["out"])] = jnp.where(lt64, lo, hi).astype(BF16)

    done = {}
    for t in range(len(jobs) + depth):
        if t < len(jobs):
            scores(jobs[t])
        if t >= depth:
            values(jobs[t - depth])
            finish(jobs[t - depth], done)


def _attn_lat_kernel(sink_ref, qa_ref, qm_ref, kd_ref, vd_ref, km_ref, vm_ref,
                     kdc_ref, vdc_ref, kmc_ref, vmc_ref, o_ref):
    j = pl.program_id(1)
    q0 = j * BQ
    nloc = BQ + 2 * WINDOW
    start = pl.multiple_of(jnp.clip(q0 - WINDOW, 0, DEC_SEQ - nloc), WINDOW)
    qpos = q0 + lax.broadcasted_iota(jnp.int32, (BQ, nloc), 0)
    kpos = start + lax.broadcasted_iota(jnp.int32, (BQ, nloc), 1)
    valid = jnp.abs(qpos - kpos) <= WINDOW
    everything = slice(None)
    a_segs = [(kd_ref, vd_ref, pl.ds(start, nloc), valid), (kdc_ref, vdc_ref, everything, None)]
    m_segs = [(km_ref, vm_ref), (kmc_ref, vmc_ref)]
    _attn_heads(qa_ref, qm_ref, a_segs, m_segs, sink_ref, o_ref, BQ, depth=2)


def _attn_ctx_kernel(sink_ref, qa_ref, qm_ref, kd_ref, vd_ref, km_ref, vm_ref, o_ref):
    a_segs = [(kd_ref, vd_ref, slice(None), None)]
    m_segs = [(km_ref, vm_ref)]
    _attn_heads(qa_ref, qm_ref, a_segs, m_segs, sink_ref, o_ref, SEQ, depth=1)


def _attention(sink2, qa, kd, vd, qm, km, vm, kdc, vdc, kmc, vmc, layer_i):
    smem = pl.BlockSpec(memory_space=pltpu.SMEM)
    nq = DEC_SEQ // BQ
    q_off = T_CTX // BQ
    s_off = T_CTX // DEC_SEQ
    q_map = lambda b, j: (q_off + b * nq + j, 0)
    kv_map = lambda b, j: (s_off + b, 0)
    c_map = lambda b, j: (layer_i, b, 0, 0)
    o_lat = pl.pallas_call(
        _attn_lat_kernel,
        grid=(DEC_BATCH, nq),
        in_specs=[
            smem,
            pl.BlockSpec((BQ, 512), q_map),
            pl.BlockSpec((BQ, 1024), q_map),
            pl.BlockSpec((DEC_SEQ, 256), kv_map),
            pl.BlockSpec((DEC_SEQ, 256), kv_map),
            pl.BlockSpec((DEC_SEQ, 1024), kv_map),
            pl.BlockSpec((B_HEADS * VT_ROWS, DEC_SEQ), lambda b, j: (0, s_off + b)),
            pl.BlockSpec((None, None, PAST_LEN, 256), c_map),
            pl.BlockSpec((None, None, PAST_LEN, 256), c_map),
            pl.BlockSpec((None, None, PAST_LEN, 1024), c_map),
            pl.BlockSpec((None, None, B_HEADS * VT_ROWS, PAST_LEN), c_map),
        ],
        out_specs=pl.BlockSpec((BQ, 1024), lambda b, j: (b * nq + j, 0)),
        out_shape=jax.ShapeDtypeStruct((T_LAT, 1024), BF16),
        compiler_params=_cparams(("arbitrary", "arbitrary")),
        name="attn_latent",
    )(sink2, qa, qm, kd, vd, km, vm, kdc, vdc, kmc, vmc)
    b_map = lambda b: (b, 0)
    o_ctx = pl.pallas_call(
        _attn_ctx_kernel,
        grid=(BATCH,),
        in_specs=[
            smem,
            pl.BlockSpec((SEQ, 512), b_map),
            pl.BlockSpec((SEQ, 1024), b_map),
            pl.BlockSpec((SEQ, 256), b_map),
            pl.BlockSpec((SEQ, 256), b_map),
            pl.BlockSpec((SEQ, 1024), b_map),
            pl.BlockSpec((B_HEADS * VT_ROWS, SEQ), lambda b: (0, b)),
        ],
        out_specs=pl.BlockSpec((SEQ, 1024), b_map),
        out_shape=jax.ShapeDtypeStruct((T_CTX, 1024), BF16),
        compiler_params=_cparams(("arbitrary",)),
        name="attn_context",
    )(sink2, qa, qm, kd, vd, km, vm)
    return (o_ctx, o_lat)


def _conv_in_kernel(x_ref, mod_ref, g_ref, w_ref, u_ref, z_ref):
    for rows in _sub_rows():
        h = _rms_mod(x_ref[rows, :], g_ref[...], mod_ref[0:1, :], mod_ref[1:2, :])
        y = _dot(h.astype(BF16), w_ref[...])
        a = y[:, 0:CONV_CH]
        gate = y[:, CONV_CH:2 * CONV_CH]
        u_ref[rows, :] = a * jax.nn.sigmoid(gate)
        z_ref[rows, :] = y[:, 2 * CONV_CH:]


def _conv_in(x, mod, g, w_in, layer):
    tile_map = lambda i: (i, 0)
    const2 = lambda i: (0, 0)
    return pl.pallas_call(
        _conv_in_kernel,
        grid=(N_TOK,),
        in_specs=[
            pl.BlockSpec((TOK, D_MODEL), tile_map),
            pl.BlockSpec((None, None, 6, D_MODEL), lambda i: (layer, _mod_row(i), 0, 0)),
            pl.BlockSpec((1, D_MODEL), const2),
            pl.BlockSpec((D_MODEL, 3 * CONV_CH), const2),
        ],
        out_specs=[pl.BlockSpec((TOK, CONV_CH), tile_map), pl.BlockSpec((TOK, POOL_CH), tile_map)],
        out_shape=[jax.ShapeDtypeStruct((T_ALL, CONV_CH), F32), jax.ShapeDtypeStruct((T_ALL, POOL_CH), F32)],
        compiler_params=_cparams(("arbitrary",)),
        name="conv_in",
    )(x, mod, g, w_in)


CONV_ROWS = 32


def _conv_mix_kernel(u_ref, up_ref, un_ref, z_ref, zp_ref, zn_ref, wdw_ref, bdw_ref, lng_ref, lnb_ref,
                     band_ref, wg_ref, ps_ref, o_ref, ue_ref, ze_ref):
    i = pl.program_id(0)
    j = (i - N_CTX_TILES) % LAT_TILES_PER_SEQ
    is_lat = i >= N_CTX_TILES
    has_prev = jnp.logical_and(is_lat, j > 0)
    has_next = jnp.logical_and(is_lat, j < LAT_TILES_PER_SEQ - 1)
    seq_len = jnp.where(is_lat, DEC_SEQ, SEQ)
    t0 = jnp.where(is_lat, j * TILE, 0)

    ue_ref[0, 0:HALO, :] = jnp.where(has_prev, up_ref[...], 0.0)
    ue_ref[0, HALO:HALO + TILE, :] = u_ref[...]
    ue_ref[0, HALO + TILE:, :] = jnp.where(has_next, un_ref[...], 0.0)
    ze_ref[0:HALO, :] = jnp.where(has_prev, zp_ref[...], 0.0)
    ze_ref[HALO:HALO + TILE, :] = z_ref[...]
    ze_ref[HALO + TILE:, :] = jnp.where(has_next, zn_ref[...], 0.0)

    pad = CONV_WIDTH // 2
    reach = (HALO - pad + CONV_WIDTH - 1) // SUBLANES * SUBLANES
    groups = CONV_ROWS // SUBLANES
    n_chunks = TILE // CONV_ROWS

    def shift_rows(lo, hi):
        for b in range(1, SUBLANES):
            ue_ref[b, lo:hi, :] = ue_ref[0, lo + b:hi + b, :]

    def conv_chunk(r):
        r0 = r * CONV_ROWS
        acc = jnp.zeros((groups, SUBLANES, CONV_CH), F32) + bdw_ref[...]
        for k in range(CONV_WIDTH):
            a, b = divmod(HALO + k - pad, SUBLANES)
            e0 = r0 + a * SUBLANES
            acc = acc + ue_ref[b, e0:e0 + CONV_ROWS, :].reshape(groups, SUBLANES, CONV_CH) * wdw_ref[k]
        acc = acc.reshape(CONV_ROWS, CONV_CH)
        mu = jnp.mean(acc, axis=-1, keepdims=True)
        d = acc - mu
        var = jnp.mean(d * d, axis=-1, keepdims=True)
        yn = d * lax.rsqrt(var + EPS) * lng_ref[...] + lnb_ref[...]
        o_ref[r0:r0 + CONV_ROWS, 0:CONV_CH] = (yn * jax.nn.sigmoid(yn)).astype(BF16)

    t = t0 + lax.broadcasted_iota(jnp.int32, (TILE, POOL_GROUP_W), 0)
    ze = ze_ref[...]
    z_hi = ze.astype(BF16)
    z_lo = (ze - z_hi.astype(F32)).astype(BF16)

    def pool_group(gi):
        w = POOL_SIZES[gi]
        lo = w // 2
        hi = w - lo - 1
        cols = slice(gi * POOL_GROUP_W, (gi + 1) * POOL_GROUP_W)
        tot = _dot(band_ref[gi], z_hi[:, cols]) + _dot(band_ref[gi], z_lo[:, cols])
        cnt = jnp.minimum(t + hi + 1, seq_len) - jnp.maximum(t - lo, 0)
        d = tot / cnt.astype(F32) - ze_ref[HALO:HALO + TILE, cols]
        y = _dot(d.astype(BF16), wg_ref[gi]) * ps_ref[:, cols]
        o_ref[:, CONV_CH + gi * POOL_GROUP_W:CONV_CH + (gi + 1) * POOL_GROUP_W] = y.astype(BF16)

    shift_rows(0, reach + CONV_ROWS)
    for r in range(n_chunks):
        if r + 1 < n_chunks:
            shift_rows(reach + (r + 1) * CONV_ROWS, reach + (r + 2) * CONV_ROWS)
        conv_chunk(r)
        if r % (n_chunks // len(POOL_SIZES)) == 0:
            pool_group(r // (n_chunks // len(POOL_SIZES)))


def _conv_mix(u, z, cp):
    tile_map = lambda i: (i, 0)
    const2 = lambda i: (0, 0)
    hb = TILE // HALO
    prev_map = lambda i: (jnp.maximum(i * hb - 1, 0), 0)
    next_map = lambda i: (jnp.minimum((i + 1) * hb, T_ALL // HALO - 1), 0)
    return pl.pallas_call(
        _conv_mix_kernel,
        grid=(N_TILES,),
        in_specs=[
            pl.BlockSpec((TILE, CONV_CH), tile_map),
            pl.BlockSpec((HALO, CONV_CH), prev_map),
            pl.BlockSpec((HALO, CONV_CH), next_map),
            pl.BlockSpec((TILE, POOL_CH), tile_map),
            pl.BlockSpec((HALO, POOL_CH), prev_map),
            pl.BlockSpec((HALO, POOL_CH), next_map),
            pl.BlockSpec((CONV_WIDTH, SUBLANES, CONV_CH), lambda i: (0, 0, 0)),
            pl.BlockSpec((1, CONV_CH), const2),
            pl.BlockSpec((1, CONV_CH), const2),
            pl.BlockSpec((1, CONV_CH), const2),
            pl.BlockSpec((len(POOL_SIZES), TILE, TILE + 2 * HALO), lambda i: (0, 0, 0)),
            pl.BlockSpec((len(POOL_SIZES), POOL_GROUP_W, POOL_GROUP_W), lambda i: (0, 0, 0)),
            pl.BlockSpec((1, POOL_CH), const2),
        ],
        out_specs=pl.BlockSpec((TILE, D_MODEL), tile_map),
        out_shape=jax.ShapeDtypeStruct((T_ALL, D_MODEL), BF16),
        scratch_shapes=[pltpu.VMEM((SUBLANES, TILE + 2 * HALO, CONV_CH), F32),
                        pltpu.VMEM((TILE + 2 * HALO, POOL_CH), F32)],
        compiler_params=_cparams(("arbitrary",)),
        name="conv_mix",
    )(u, u, u, z, z, z, cp["w_dw"], cp["b_dw"], cp["ln_g"], cp["ln_b"], _pool_bands(), cp["w_grp"], cp["p_scale"])


def _pool_bands():
    t = np.arange(TILE)[:, None]
    e = np.arange(TILE + 2 * HALO)[None, :] - HALO
    bands = []
    for w in POOL_SIZES:
        lo = w // 2
        hi = w - lo - 1
        bands.append((e >= t - lo) & (e <= t + hi))
    return jnp.asarray(np.stack(bands), BF16)


def _post_kernel(*refs, n_o, n_x, final):
    o_refs, x_refs = refs[:n_o], refs[n_o:n_o + n_x]
    mod_ref, g_ref, wo_ref, w1_ref, w2_ref, fg_ref = refs[n_o + n_x:n_o + n_x + 6]
    y_refs = refs[n_o + n_x + 6:]
    subs = _sub_rows()
    x1s, hs, acts, outs = [], [], [], []
    for rows in subs:
        x1 = _tok_load(x_refs, rows) + mod_ref[2:3, :] * _dot(_tok_load(o_refs, rows), wo_ref[...])
        x1s.append(x1)
        hs.append(_rms_mod(x1, g_ref[...], mod_ref[3:4, :], mod_ref[4:5, :]).astype(BF16))
    for h in hs:
        a = jnp.maximum(_dot(h, w1_ref[...]), 0.0)
        acts.append((a * a).astype(BF16))
    for rows, x1, a in zip(subs, x1s, acts):
        x2 = x1 + mod_ref[5:6, :] * _dot(a, w2_ref[...])
        if final:
            ms = jnp.mean(x2 * x2, axis=-1, keepdims=True)
            x2 = x2 * lax.rsqrt(ms + EPS) * fg_ref[...]
        outs.append((rows, x2))
    _tok_store(y_refs, outs)


def _post(o, x, mod, g, w_out, w1, w2, final_g, layer, final):
    const2 = lambda i: (0, 0)
    o_ops, o_specs = _tok_specs(o, D_MODEL)
    x_ops, x_specs = _tok_specs(x, D_MODEL)
    if final:
        y = (jax.ShapeDtypeStruct((T_CTX, D_MODEL), F32), jax.ShapeDtypeStruct((T_LAT, D_MODEL), F32))
    else:
        y = jax.ShapeDtypeStruct((T_ALL, D_MODEL), F32)
    _, y_specs = _tok_specs(y, D_MODEL)
    out = pl.pallas_call(
        functools.partial(_post_kernel, n_o=len(o_ops), n_x=len(x_ops), final=final),
        grid=(N_TOK,),
        in_specs=o_specs + x_specs + [
            pl.BlockSpec((None, None, 6, D_MODEL), lambda i: (layer, _mod_row(i), 0, 0)),
            pl.BlockSpec((1, D_MODEL), const2),
            pl.BlockSpec((D_MODEL, D_MODEL), const2, pipeline_mode=pl.Buffered(1)),
            pl.BlockSpec((None, D_MODEL, D_FF), lambda i: (layer, 0, 0), pipeline_mode=pl.Buffered(1)),
            pl.BlockSpec((None, D_FF, D_MODEL), lambda i: (layer, 0, 0), pipeline_mode=pl.Buffered(1)),
            pl.BlockSpec((1, D_MODEL), const2),
        ],
        out_specs=y_specs if final else y_specs[0],
        out_shape=y,
        compiler_params=_cparams(("arbitrary",)),
        name="post_final" if final else "post",
    )(*o_ops, *x_ops, mod, g, w_out, w1, w2, final_g)
    return out


def _pair_lanes(w):
    lead = w.shape[:-1]
    w = w.reshape(lead + (w.shape[-1] // LANES, 2, 2, HEAD_DIM // 2))
    return jnp.swapaxes(w, -3, -2).reshape(lead + (-1,))


def _mla_lanes(nope, rope):
    shape = (nope if nope is not None else rope).shape[:-1]
    nope = jnp.zeros(shape + (QK_NOPE,), F32) if nope is None else nope
    rope = jnp.zeros(shape + (QK_ROPE,), F32) if rope is None else rope
    h = QK_ROPE // 2
    return jnp.concatenate([rope[..., :h], nope[..., :64 - h], rope[..., h:], nope[..., 64 - h:],
                            jnp.zeros(shape + (LANES - QK_NOPE - QK_ROPE,), F32)], axis=-1)


def _rope_tables():
    n = DEC_SEQ
    rows = n // GRID_W
    row = jnp.repeat(jnp.arange(rows), GRID_W).astype(F32)
    col = jnp.tile(jnp.arange(GRID_W), rows).astype(F32)

    def angles(dim):
        quarter = dim // 4
        inv_freq = ROPE_BASE ** (-jnp.arange(quarter, dtype=F32) / quarter)
        return jnp.concatenate([row[:, None] * inv_freq, col[:, None] * inv_freq], axis=-1)

    ang_a = angles(HEAD_DIM)
    cos_a, sin_a = jnp.cos(ang_a), jnp.sin(ang_a)
    c_a = jnp.concatenate([cos_a] * 4, axis=-1)
    s_a = jnp.concatenate([-sin_a, -sin_a, sin_a, sin_a], axis=-1)
    ang_b = angles(QK_ROPE)
    cos_b, sin_b = jnp.cos(ang_b), jnp.sin(ang_b)
    c_b = _mla_lanes(jnp.ones((n, QK_NOPE), F32), jnp.concatenate([cos_b, cos_b], axis=-1))
    s_b = _mla_lanes(None, jnp.concatenate([-sin_b, sin_b], axis=-1))
    lat = jnp.stack([c_a, s_a, c_b, s_b])
    ident = jnp.stack([jnp.ones((TOK, LANES), F32), jnp.zeros((TOK, LANES), F32)] * 2)
    return jnp.concatenate([ident, lat], axis=1)


def _prep_even(attn_w_in, mla_q_norm, mla_kv_norm, mla_w_qb, mla_w_kvb, attn_w_out, i):
    w = attn_w_in[i]
    o = np.cumsum((0, 512, 128, 128, Q_LORA, KV_LORA, QK_ROPE))
    zeros = lambda n: jnp.zeros((D_MODEL, n), F32)
    w_in = jnp.concatenate([
        _pair_lanes(w[:, o[0]:o[1]]), _pair_lanes(w[:, o[1]:o[2]]), w[:, o[2]:o[3]],
        w[:, o[3]:o[4]], zeros(Q_LORA_PAD - Q_LORA),
        w[:, o[4]:o[5]],
        _mla_lanes(None, w[:, o[5]:o[6]]),
    ], axis=1).astype(BF16)
    q_norm = jnp.pad(mla_q_norm[i], (0, Q_LORA_PAD - Q_LORA)).reshape(1, Q_LORA_PAD)
    wqb = mla_w_qb[i].reshape(Q_LORA, B_HEADS, QK_NOPE + QK_ROPE)
    wqb = jnp.pad(_mla_lanes(wqb[..., :QK_NOPE], wqb[..., QK_NOPE:]), ((0, Q_LORA_PAD - Q_LORA), (0, 0), (0, 0)))
    wkvb = mla_w_kvb[i].reshape(KV_LORA, B_HEADS, QK_NOPE + V_DIM)
    w_kk = _mla_lanes(wkvb[:, :, :QK_NOPE], None)
    w_kv = wkvb[:, :, QK_NOPE:]
    return {
        "w_in": w_in,
        "q_norm": q_norm,
        "kv_norm": mla_kv_norm[i].reshape(1, KV_LORA),
        "w_qb": wqb.reshape(Q_LORA_PAD, B_HEADS * LANES).astype(BF16),
        "w_kk": w_kk.reshape(KV_LORA, B_HEADS * LANES).astype(BF16),
        "w_kv": w_kv.reshape(KV_LORA, B_HEADS * V_DIM).T.astype(BF16),
        "w_out": attn_w_out[i].astype(BF16),
    }


def kernel(x_prompt, x_sample, cache_win_k, cache_win_v, cache_mla_ckv, cache_mla_krope, c, c_ctx, w_mod, b_mod,
           norm_g, attn_w_in, attn_sink, mla_q_norm, mla_kv_norm, mla_w_qb, mla_w_kvb, attn_w_out, conv_w_in,
           conv_dw, conv_dw_b, conv_ln_g, conv_ln_b, pool_w, pool_scale, conv_w_out, mlp_w1, mlp_w2, final_g):
    x = (x_prompt.reshape(T_CTX, D_MODEL), x_sample.reshape(T_LAT, D_MODEL))

    cond = jnp.concatenate([c_ctx[None, :], c, jnp.zeros((N_COND - 1 - DEC_BATCH, D_MODEL), F32)], axis=0)
    mod = _mod_table(cond, w_mod, b_mod).reshape(DEPTH, N_COND, 6, D_MODEL)

    rope = _rope_tables()
    even = [_prep_even(attn_w_in, mla_q_norm, mla_kv_norm, mla_w_qb, mla_w_kvb, attn_w_out, i)
            for i in range(N_EVEN)]
    cache_kr128 = _mla_lanes(None, cache_mla_krope)
    kdc, vdc, kmc, vmc = _ctx_kv(
        _pair_lanes(cache_win_k.reshape(DEC_BATCH, N_EVEN, PAST_LEN, LANES)),
        cache_win_v.reshape(DEC_BATCH, N_EVEN, PAST_LEN, LANES),
        cache_mla_ckv, cache_kr128,
        jnp.stack([e["w_kk"] for e in even]), jnp.stack([e["w_kv"] for e in even]))
    final_g2 = final_g.reshape(1, D_MODEL)
    w1_all, w2_all = mlp_w1.astype(BF16), mlp_w2.astype(BF16)

    ks, vs, ckvs, krs = [], [], [], []
    for l in range(DEPTH):
        g_mix = norm_g[l, 0].reshape(1, D_MODEL)
        g_mlp = norm_g[l, 1].reshape(1, D_MODEL)
        if l % 2 == 0:
            i = l // 2
            wp = even[i]
            qa, kd, vd, qm, km, vm, ka32, va32, ckv32, kr32 = _attn_in(x, mod, g_mix, wp, rope, l)
            ka32 = jnp.swapaxes(ka32.reshape(T_CTX, 2, A_KV_HEADS, HEAD_DIM // 2), 1, 2)
            ks.append(ka32.reshape(BATCH, SEQ, A_KV_HEADS, HEAD_DIM))
            vs.append(va32.reshape(BATCH, SEQ, A_KV_HEADS, HEAD_DIM))
            ckvs.append(ckv32.reshape(BATCH, SEQ, KV_LORA))
            kr32 = jnp.concatenate([kr32[:, :QK_ROPE // 2], kr32[:, 64:64 + QK_ROPE // 2]], axis=-1)
            krs.append(kr32.reshape(BATCH, SEQ, QK_ROPE))
            o = _attention(attn_sink[i] * LOG2E, qa, kd, vd, qm, km, vm, kdc, vdc, kmc, vmc, i)
            w_out = wp["w_out"]
        else:
            jj = l // 2
            u, z = _conv_in(x, mod, g_mix, conv_w_in[jj].astype(BF16), l)
            cp = {
                "w_dw": jnp.broadcast_to(conv_dw[jj][:, None, :], (CONV_WIDTH, SUBLANES, CONV_CH)),
                "b_dw": conv_dw_b[jj].reshape(1, CONV_CH),
                "ln_g": conv_ln_g[jj].reshape(1, CONV_CH),
                "ln_b": conv_ln_b[jj].reshape(1, CONV_CH),
                "w_grp": pool_w[jj].astype(BF16),
                "p_scale": pool_scale[jj].reshape(1, POOL_CH),
            }
            o = _conv_mix(u, z, cp)
            w_out = conv_w_out[jj].astype(BF16)
        x = _post(o, x, mod, g_mlp, w_out, w1_all, w2_all, final_g2, l, final=(l == DEPTH - 1))

    y_prompt = x[0].reshape(BATCH, SEQ, D_MODEL)
    y_sample = x[1].reshape(DEC_BATCH, DEC_SEQ, D_MODEL)
    return (y_prompt, y_sample, jnp.stack(ks, axis=1), jnp.stack(vs, axis=1),
            jnp.stack(ckvs, axis=1), jnp.stack(krs, axis=1))
```

```python
import functools
import math

import numpy as np
import jax
import jax.numpy as jnp
from jax import lax
from jax.experimental import pallas as pl
from jax.experimental.pallas import tpu as pltpu

F32 = jnp.float32
BF16 = jnp.bfloat16

D_MODEL = 1024
BATCH = 16
SEQ = 256
DEPTH = 4
DEC_BATCH = 8
DEC_SEQ = 2048
PAST_LEN = 256
GRID_W = 64
N_EVEN = (DEPTH + 1) // 2
N_ODD = DEPTH // 2
A_HEADS = 8
A_KV_HEADS = 2
A_GROUP = A_HEADS // A_KV_HEADS
HEAD_DIM = 64
WINDOW = 128
B_HEADS = 8
Q_LORA = 192
KV_LORA = 128
QK_NOPE = 64
QK_ROPE = 32
V_DIM = 64
MLA_SCALE = (QK_NOPE + QK_ROPE) ** -0.5
CONV_CH = D_MODEL // 2
CONV_WIDTH = 31
POOL_CH = D_MODEL // 2
POOL_SIZES = (2, 4, 8, 16)
POOL_GROUP_W = POOL_CH // len(POOL_SIZES)
D_FF = 4 * D_MODEL
ROPE_BASE = 10000.0
EPS = 1e-6
NEG_INF = -1e30
LOG2E = math.log2(math.e)

LANES = 128
SUBLANES = 8
VMEM_LIMIT_BYTES = 56 * 1024 * 1024

TILE = 256
T_CTX = BATCH * SEQ
T_LAT = DEC_BATCH * DEC_SEQ
T_ALL = T_CTX + T_LAT
N_CTX_TILES = T_CTX // TILE
N_TILES = T_ALL // TILE
LAT_TILES_PER_SEQ = DEC_SEQ // TILE
TOK = 512
N_CTX_TOK = T_CTX // TOK
N_TOK = T_ALL // TOK
LAT_TOK_PER_SEQ = DEC_SEQ // TOK
N_COND = 16
BQ = 256
VT_ROWS = 80
HALO = 16
Q_LORA_PAD = 256
ATTN_IN_COLS = 512 + 128 + 128 + Q_LORA_PAD + 128 + 128


def _cparams(sem):
    return pltpu.CompilerParams(dimension_semantics=sem, vmem_limit_bytes=VMEM_LIMIT_BYTES)


def _mod_row(i):
    return jnp.where(i < N_CTX_TOK, 0, 1 + (i - N_CTX_TOK) // LAT_TOK_PER_SEQ)


def _pos_block(i):
    return jnp.where(i < N_CTX_TOK, 0, 1 + (i - N_CTX_TOK) % LAT_TOK_PER_SEQ)


def _tok_specs(a, width):
    if isinstance(a, tuple):
        return list(a), [pl.BlockSpec((TOK, width), lambda i: (jnp.minimum(i, N_CTX_TOK - 1), 0)),
                         pl.BlockSpec((TOK, width), lambda i: (jnp.maximum(i - N_CTX_TOK, 0), 0))]
    return [a], [pl.BlockSpec((TOK, width), lambda i: (i, 0))]


def _tok_load(refs, rows):
    if len(refs) == 1:
        return refs[0][rows, :]
    return jnp.where(pl.program_id(0) < N_CTX_TOK, refs[0][rows, :], refs[1][rows, :])


def _tok_store(refs, parts):
    def put(ref):
        for rows, val in parts:
            ref[rows, :] = val

    if len(refs) == 1:
        put(refs[0])
        return
    i = pl.program_id(0)

    @pl.when(i < N_CTX_TOK)
    def _():
        put(refs[0])

    @pl.when(i >= N_CTX_TOK)
    def _():
        put(refs[1])


def _sub_rows():
    return [slice(s * TILE, (s + 1) * TILE) for s in range(TOK // TILE)]


def _dot(a, b):
    return jnp.dot(a, b, preferred_element_type=F32)


def _dot_nt(a, b):
    return lax.dot_general(a, b, (((1,), (1,)), ((), ())), preferred_element_type=F32)


def _rms_mod(x, g, shift, scale):
    ms = jnp.mean(x * x, axis=-1, keepdims=True)
    return x * lax.rsqrt(ms + EPS) * (g * (1.0 + scale)) + shift


def _rope(x, c, s):
    return x * c + pltpu.roll(x, 64, 1) * s


def _lane_lt64(shape):
    return lax.broadcasted_iota(jnp.int32, shape, len(shape) - 1) < 64


def _lane_even32(shape):
    return (lax.broadcasted_iota(jnp.int32, shape, len(shape) - 1) & 63) < 32


def _store_kv_dup(kd_ref, vd_ref, rows, k, v):
    even32 = _lane_even32(k.shape)
    lt64 = _lane_lt64(v.shape)
    kd_ref[rows, 0:LANES] = jnp.where(even32, k, pltpu.roll(k, 32, 1)).astype(BF16)
    kd_ref[rows, LANES:2 * LANES] = jnp.where(even32, pltpu.roll(k, LANES - 32, 1), k).astype(BF16)
    vd_ref[rows, 0:LANES] = jnp.where(lt64, v, 1.0).astype(BF16)
    vd_ref[rows, LANES:2 * LANES] = jnp.where(lt64, pltpu.roll(v, 64, 1), 1.0).astype(BF16)


def _store_vt(vt_ref, cols, ckvn, wkvt_ref):
    vt = _dot(wkvt_ref[...], ckvn.T.astype(BF16))
    ones = jnp.ones((VT_ROWS - V_DIM, ckvn.shape[0]), BF16)
    for hh in range(B_HEADS):
        vt_ref[hh * VT_ROWS:hh * VT_ROWS + V_DIM, cols] = vt[hh * V_DIM:(hh + 1) * V_DIM].astype(BF16)
        vt_ref[hh * VT_ROWS + V_DIM:(hh + 1) * VT_ROWS, cols] = ones


def _mod_kernel(cond_ref, w_ref, b_ref, o_ref):
    c = cond_ref[...]
    s = c * jax.nn.sigmoid(c)
    o_ref[...] = _dot(s.astype(BF16), w_ref[...].astype(BF16)) + b_ref[...]


def _mod_table(cond, w_mod, b_mod):
    nb = 6 * D_MODEL // 1024
    return pl.pallas_call(
        _mod_kernel,
        grid=(DEPTH, nb),
        in_specs=[
            pl.BlockSpec((N_COND, D_MODEL), lambda l, n: (0, 0)),
            pl.BlockSpec((None, D_MODEL, 1024), lambda l, n: (l, 0, n)),
            pl.BlockSpec((None, 1, 1024), lambda l, n: (l, 0, n)),
        ],
        out_specs=pl.BlockSpec((None, N_COND, 1024), lambda l, n: (l, 0, n)),
        out_shape=jax.ShapeDtypeStruct((DEPTH, N_COND, 6 * D_MODEL), F32),
        compiler_params=_cparams(("arbitrary", "arbitrary")),
        name="mod_table",
    )(cond, w_mod, b_mod.reshape(DEPTH, 1, 6 * D_MODEL))


def _attn_in_kernel(*refs, n_x):
    x_refs = refs[:n_x]
    (mod_ref, g_ref, w_ref, qn_ref, kvn_ref, wqb_ref, wkk_ref, wkv_ref, rope_ref,
     qa_ref, kd_ref, vd_ref, qm_ref, km_ref, vm_ref, ka32_ref, va32_ref, ckv32_ref, kr32_ref) = refs[n_x:]
    col = lambda n: slice(n * LANES, (n + 1) * LANES)

    def project(rows):
        h = _rms_mod(_tok_load(x_refs, rows), g_ref[...], mod_ref[0:1, :], mod_ref[1:2, :])
        return _dot(h.astype(BF16), w_ref[...])

    def derive(rows, y):
        ca, sa = rope_ref[0, rows, :], rope_ref[1, rows, :]
        cb, sb = rope_ref[2, rows, :], rope_ref[3, rows, :]

        for c in range(4):
            q = _rope(y[:, col(c)], ca, sa)
            qa_ref[rows, col(c)] = (q * (HEAD_DIM ** -0.5 * LOG2E)).astype(BF16)

        ka = _rope(y[:, 512:640], ca, sa)
        va = y[:, 640:768]
        _store_kv_dup(kd_ref, vd_ref, rows, ka, va)

        cq = y[:, 768:768 + Q_LORA_PAD]
        cqn = cq * lax.rsqrt(jnp.sum(cq * cq, axis=-1, keepdims=True) * (1.0 / Q_LORA) + EPS) * qn_ref[...]
        qm = _dot(cqn.astype(BF16), wqb_ref[...])
        for hh in range(B_HEADS):
            q = _rope(qm[:, col(hh)], cb, sb)
            qm_ref[rows, col(hh)] = (q * (MLA_SCALE * LOG2E)).astype(BF16)

        ckv = y[:, 1024:1152]
        ckvn = ckv * lax.rsqrt(jnp.mean(ckv * ckv, axis=-1, keepdims=True) + EPS) * kvn_ref[...]
        kr = y[:, 1152:1280]
        ckvn_b = ckvn.astype(BF16)
        kn = _dot(ckvn_b, wkk_ref[...])
        krr = _rope(kr, cb, sb)
        for hh in range(B_HEADS):
            km_ref[rows, col(hh)] = (kn[:, col(hh)] + krr).astype(BF16)
        _store_vt(vm_ref, rows, ckvn, wkv_ref)
        return ka, va, ckvn, kr

    subs = _sub_rows()
    cache = []
    y = project(subs[0])
    for s, rows in enumerate(subs):
        y_next = project(subs[s + 1]) if s + 1 < len(subs) else None
        cache.append(derive(rows, y))
        y = y_next

    @pl.when(pl.program_id(0) < N_CTX_TOK)
    def _():
        for rows, (ka, va, ckvn, kr) in zip(subs, cache):
            ka32_ref[rows, :] = ka
            va32_ref[rows, :] = va
            ckv32_ref[rows, :] = ckvn
            kr32_ref[rows, :] = kr


def _attn_in(x, mod, g, wp, rope, layer):
    tile_map = lambda i: (i, 0)
    const2 = lambda i: (0, 0)
    ctx_map = lambda i: (jnp.minimum(i, N_CTX_TOK - 1), 0)
    bf = lambda w: jax.ShapeDtypeStruct((T_ALL, w), BF16)
    c32 = jax.ShapeDtypeStruct((T_CTX, LANES), F32)
    x_ops, x_specs = _tok_specs(x, D_MODEL)
    return pl.pallas_call(
        functools.partial(_attn_in_kernel, n_x=len(x_ops)),
        grid=(N_TOK,),
        in_specs=x_specs + [
            pl.BlockSpec((None, None, 6, D_MODEL), lambda i: (layer, _mod_row(i), 0, 0)),
            pl.BlockSpec((1, D_MODEL), const2),
            pl.BlockSpec((D_MODEL, ATTN_IN_COLS), const2),
            pl.BlockSpec((1, Q_LORA_PAD), const2),
            pl.BlockSpec((1, KV_LORA), const2),
            pl.BlockSpec((Q_LORA_PAD, B_HEADS * LANES), const2),
            pl.BlockSpec((KV_LORA, B_HEADS * LANES), const2),
            pl.BlockSpec((B_HEADS * V_DIM, KV_LORA), const2),
            pl.BlockSpec((4, TOK, LANES), lambda i: (0, _pos_block(i), 0)),
        ],
        out_specs=[
            pl.BlockSpec((TOK, 512), tile_map),
            pl.BlockSpec((TOK, 256), tile_map),
            pl.BlockSpec((TOK, 256), tile_map),
            pl.BlockSpec((TOK, 1024), tile_map),
            pl.BlockSpec((TOK, 1024), tile_map),
            pl.BlockSpec((B_HEADS * VT_ROWS, TOK), lambda i: (0, i)),
            pl.BlockSpec((TOK, LANES), ctx_map),
            pl.BlockSpec((TOK, LANES), ctx_map),
            pl.BlockSpec((TOK, LANES), ctx_map),
            pl.BlockSpec((TOK, LANES), ctx_map),
        ],
        out_shape=[bf(512), bf(256), bf(256), bf(1024), bf(1024),
                   jax.ShapeDtypeStruct((B_HEADS * VT_ROWS, T_ALL), BF16), c32, c32, c32, c32],
        compiler_params=_cparams(("arbitrary",)),
        name="attn_in",
    )(*x_ops, mod, g, wp["w_in"], wp["q_norm"], wp["kv_norm"], wp["w_qb"], wp["w_kk"], wp["w_kv"], rope)


def _ctx_kv_kernel(ck_ref, cv_ref, cckv_ref, ckr_ref, wkk_ref, wkv_ref, kd_ref, vd_ref, km_ref, vm_ref):
    _store_kv_dup(kd_ref, vd_ref, slice(None), ck_ref[...], cv_ref[...])
    cb = cckv_ref[...].astype(BF16)
    kn = _dot(cb, wkk_ref[...])
    kr = ckr_ref[...]
    for hh in range(B_HEADS):
        km_ref[:, hh * LANES:(hh + 1) * LANES] = (kn[:, hh * LANES:(hh + 1) * LANES] + kr).astype(BF16)
    _store_vt(vm_ref, slice(None), cckv_ref[...], wkv_ref)


def _ctx_kv(cache_k, cache_v, cache_ckv, cache_kr128, w_kk, w_kv):
    cache_map = lambda i, b: (b, i, 0, 0)
    w_map = lambda i, b: (i, 0, 0)
    out_map = lambda i, b: (i, b, 0, 0)
    o = lambda w: jax.ShapeDtypeStruct((N_EVEN, DEC_BATCH, PAST_LEN, w), BF16)
    return pl.pallas_call(
        _ctx_kv_kernel,
        grid=(N_EVEN, DEC_BATCH),
        in_specs=[
            pl.BlockSpec((None, None, PAST_LEN, LANES), cache_map),
            pl.BlockSpec((None, None, PAST_LEN, LANES), cache_map),
            pl.BlockSpec((None, None, PAST_LEN, KV_LORA), cache_map),
            pl.BlockSpec((None, None, PAST_LEN, LANES), cache_map),
            pl.BlockSpec((None, KV_LORA, B_HEADS * LANES), w_map),
            pl.BlockSpec((None, B_HEADS * V_DIM, KV_LORA), w_map),
        ],
        out_specs=[
            pl.BlockSpec((None, None, PAST_LEN, 256), out_map),
            pl.BlockSpec((None, None, PAST_LEN, 256), out_map),
            pl.BlockSpec((None, None, PAST_LEN, 1024), out_map),
            pl.BlockSpec((None, None, B_HEADS * VT_ROWS, PAST_LEN), out_map),
        ],
        out_shape=[o(256), o(256), o(1024),
                   jax.ShapeDtypeStruct((N_EVEN, DEC_BATCH, B_HEADS * VT_ROWS, PAST_LEN), BF16)],
        compiler_params=_cparams(("arbitrary", "arbitrary")),
        name="ctx_kv",
    )(cache_k, cache_v, cache_ckv, cache_kr128, w_kk, w_kv)


def _attn_heads(qa_ref, qm_ref, a_segs, m_segs, sink_ref, o_ref, rows, depth):
    lt64 = _lane_lt64((rows, LANES))
    lane = lax.broadcasted_iota(jnp.int32, (1, LANES), 1)
    keep = (((lane & 63) < 32).astype(BF16), ((lane & 63) >= 32).astype(BF16))
    col = lambda n: slice(n * LANES, (n + 1) * LANES)
    swap = lambda r: pltpu.roll(r, 64, 1)

    jobs = []
    for kh in range(A_KV_HEADS):
        heads = [dict(out=(kh * A_GROUP + g) // 2, half=g % 2, sink=kh * A_GROUP + g) for g in range(A_GROUP)]
        jobs.append(dict(
            heads=heads,
            q=lambda heads=heads: jnp.concatenate(
                [qa_ref[:, col(h["out"])] * keep[h["half"]] for h in heads], axis=0),
            ks=[lambda kd=kd, rs=rs, kh=kh: kd[rs, col(kh)] for kd, _, rs, _ in a_segs],
            vs=[lambda vd=vd, rs=rs, kh=kh: vd[rs, col(kh)] for _, vd, rs, _ in a_segs],
            valids=[valid for _, _, _, valid in a_segs]))
    for hh in range(B_HEADS):
        jobs.append(dict(
            keys_on_rows=True,
            heads=[dict(out=4 + hh // 2, half=hh % 2)],
            q=lambda hh=hh: qm_ref[:, col(hh)],
            ks=[lambda km=km, hh=hh: km[:, col(hh)] for km, _ in m_segs],
            vs=[lambda vmt=vmt, hh=hh: vmt[hh * VT_ROWS:(hh + 1) * VT_ROWS, :] for _, vmt in m_segs]))

    def scores(job):
        if job.get("keys_on_rows"):
            h = job["heads"][0]
            q = job["q"]()
            h["ss"] = [_dot_nt(k(), q) for k in job["ks"]]
            m = h["ss"][0].max(axis=0, keepdims=True)
            for s in h["ss"][1:]:
                m = jnp.maximum(m, s.max(axis=0, keepdims=True))
            h["m"] = m
            return
        full = [_dot_nt(job["q"](), k()) for k in job["ks"]]
        for b, h in enumerate(job["heads"]):
            blk = slice(b * rows, (b + 1) * rows)
            ss = [s[blk] if valid is None else jnp.where(valid, s[blk], NEG_INF)
                  for s, valid in zip(full, job["valids"])]
            m = ss[0].max(axis=-1, keepdims=True)
            for s in ss[1:]:
                m = jnp.maximum(m, s.max(axis=-1, keepdims=True))
            sink = sink_ref[h["sink"]]
            m = jnp.maximum(m, sink)
            h["e"] = jnp.exp2(sink - m)
            h["ss"], h["m"] = ss, m

    def values(job):
        if job.get("keys_on_rows"):
            h = job["heads"][0]
            r = None
            for s, vt in zip(h.pop("ss"), job["vs"]):
                rs_ = _dot(vt(), jnp.exp2(s - h["m"]).astype(BF16))
                r = rs_ if r is None else r + rs_
            del h["m"]
            h["r"] = r
            return
        r = None
        for i, v in enumerate(job["vs"]):
            p = [jnp.exp2(h["ss"][i] - h["m"]).astype(BF16) for h in job["heads"]]
            rs_ = _dot(jnp.concatenate(p, axis=0), v())
            r = rs_ if r is None else r + rs_
        for b, h in enumerate(job["heads"]):
            del h["ss"], h["m"]
            h["r"] = r[b * rows:(b + 1) * rows]

    def finish(job, done):
        if job.get("keys_on_rows"):
            h = job["heads"][0]
            r = h.pop("r")
            o_t = r[0:V_DIM] / r[V_DIM:V_DIM + 1]
            other = done.pop(h["out"], None)
            if other is None:
                done[h["out"]] = o_t
            else:
                lo, hi = (other, o_t) if h["half"] == 1 else (o_t, other)
                o_ref[:, col(h["out"])] = jnp.concatenate([lo, hi], axis=0).T.astype(BF16)
            return
        for h in job["heads"]:
            r = h.pop("r")
            if h["half"] == 1:
                o = swap(r) / (r + h["e"])
            else:
                o = r / (swap(r) + h["e"])
            other = done.pop(h["out"], None)
            if other is None:
                done[h["out"]] = o
            else:
                lo, hi = (other, o) if h["half"] == 1 else (o, other)
                o_ref[:, col(h["out"])] = jnp.where(lt64, lo, hi).astype(BF16)

    done = {}
    for t in range(len(jobs) + depth):
        if t < len(jobs):
            scores(jobs[t])
        if t >= depth:
            values(jobs[t - depth])
            finish(jobs[t - depth], done)


def _attn_lat_kernel(sink_ref, qa_ref, qm_ref, kd_ref, vd_ref, km_ref, vm_ref,
                     kdc_ref, vdc_ref, kmc_ref, vmc_ref, o_ref):
    j = pl.program_id(1)
    q0 = j * BQ
    nloc = BQ + 2 * WINDOW
    start = pl.multiple_of(jnp.clip(q0 - WINDOW, 0, DEC_SEQ - nloc), WINDOW)
    qpos = q0 + lax.broadcasted_iota(jnp.int32, (BQ, nloc), 0)
    kpos = start + lax.broadcasted_iota(jnp.int32, (BQ, nloc), 1)
    valid = jnp.abs(qpos - kpos) <= WINDOW
    everything = slice(None)
    a_segs = [(kd_ref, vd_ref, pl.ds(start, nloc), valid), (kdc_ref, vdc_ref, everything, None)]
    m_segs = [(km_ref, vm_ref), (kmc_ref, vmc_ref)]
    _attn_heads(qa_ref, qm_ref, a_segs, m_segs, sink_ref, o_ref, BQ, depth=2)


def _attn_ctx_kernel(sink_ref, qa_ref, qm_ref, kd_ref, vd_ref, km_ref, vm_ref, o_ref):
    a_segs = [(kd_ref, vd_ref, slice(None), None)]
    m_segs = [(km_ref, vm_ref)]
    _attn_heads(qa_ref, qm_ref, a_segs, m_segs, sink_ref, o_ref, SEQ, depth=1)


def _attention(sink2, qa, kd, vd, qm, km, vm, kdc, vdc, kmc, vmc, layer_i):
    smem = pl.BlockSpec(memory_space=pltpu.SMEM)
    nq = DEC_SEQ // BQ
    q_off = T_CTX // BQ
    s_off = T_CTX // DEC_SEQ
    q_map = lambda b, j: (q_off + b * nq + j, 0)
    kv_map = lambda b, j: (s_off + b, 0)
    c_map = lambda b, j: (layer_i, b, 0, 0)
    o_lat = pl.pallas_call(
        _attn_lat_kernel,
        grid=(DEC_BATCH, nq),
        in_specs=[
            smem,
            pl.BlockSpec((BQ, 512), q_map),
            pl.BlockSpec((BQ, 1024), q_map),
            pl.BlockSpec((DEC_SEQ, 256), kv_map),
            pl.BlockSpec((DEC_SEQ, 256), kv_map),
            pl.BlockSpec((DEC_SEQ, 1024), kv_map),
            pl.BlockSpec((B_HEADS * VT_ROWS, DEC_SEQ), lambda b, j: (0, s_off + b)),
            pl.BlockSpec((None, None, PAST_LEN, 256), c_map),
            pl.BlockSpec((None, None, PAST_LEN, 256), c_map),
            pl.BlockSpec((None, None, PAST_LEN, 1024), c_map),
            pl.BlockSpec((None, None, B_HEADS * VT_ROWS, PAST_LEN), c_map),
        ],
        out_specs=pl.BlockSpec((BQ, 1024), lambda b, j: (b * nq + j, 0)),
        out_shape=jax.ShapeDtypeStruct((T_LAT, 1024), BF16),
        compiler_params=_cparams(("arbitrary", "arbitrary")),
        name="attn_latent",
    )(sink2, qa, qm, kd, vd, km, vm, kdc, vdc, kmc, vmc)
    b_map = lambda b: (b, 0)
    o_ctx = pl.pallas_call(
        _attn_ctx_kernel,
        grid=(BATCH,),
        in_specs=[
            smem,
            pl.BlockSpec((SEQ, 512), b_map),
            pl.BlockSpec((SEQ, 1024), b_map),
            pl.BlockSpec((SEQ, 256), b_map),
            pl.BlockSpec((SEQ, 256), b_map),
            pl.BlockSpec((SEQ, 1024), b_map),
            pl.BlockSpec((B_HEADS * VT_ROWS, SEQ), lambda b: (0, b)),
        ],
        out_specs=pl.BlockSpec((SEQ, 1024), b_map),
        out_shape=jax.ShapeDtypeStruct((T_CTX, 1024), BF16),
        compiler_params=_cparams(("arbitrary",)),
        name="attn_context",
    )(sink2, qa, qm, kd, vd, km, vm)
    return (o_ctx, o_lat)


def _conv_in_kernel(x_ref, mod_ref, g_ref, w_ref, u_ref, z_ref):
    for rows in _sub_rows():
        h = _rms_mod(x_ref[rows, :], g_ref[...], mod_ref[0:1, :], mod_ref[1:2, :])
        y = _dot(h.astype(BF16), w_ref[...])
        a = y[:, 0:CONV_CH]
        gate = y[:, CONV_CH:2 * CONV_CH]
        u_ref[rows, :] = a * jax.nn.sigmoid(gate)
        z_ref[rows, :] = y[:, 2 * CONV_CH:]


def _conv_in(x, mod, g, w_in, layer):
    tile_map = lambda i: (i, 0)
    const2 = lambda i: (0, 0)
    return pl.pallas_call(
        _conv_in_kernel,
        grid=(N_TOK,),
        in_specs=[
            pl.BlockSpec((TOK, D_MODEL), tile_map),
            pl.BlockSpec((None, None, 6, D_MODEL), lambda i: (layer, _mod_row(i), 0, 0)),
            pl.BlockSpec((1, D_MODEL), const2),
            pl.BlockSpec((D_MODEL, 3 * CONV_CH), const2),
        ],
        out_specs=[pl.BlockSpec((TOK, CONV_CH), tile_map), pl.BlockSpec((TOK, POOL_CH), tile_map)],
        out_shape=[jax.ShapeDtypeStruct((T_ALL, CONV_CH), F32), jax.ShapeDtypeStruct((T_ALL, POOL_CH), F32)],
        compiler_params=_cparams(("arbitrary",)),
        name="conv_in",
    )(x, mod, g, w_in)


CONV_ROWS = 32


def _conv_mix_kernel(u_ref, up_ref, un_ref, z_ref, zp_ref, zn_ref, wdw_ref, bdw_ref, lng_ref, lnb_ref,
                     band_ref, wg_ref, ps_ref, o_ref, ue_ref, ze_ref):
    i = pl.program_id(0)
    j = (i - N_CTX_TILES) % LAT_TILES_PER_SEQ
    is_lat = i >= N_CTX_TILES
    has_prev = jnp.logical_and(is_lat, j > 0)
    has_next = jnp.logical_and(is_lat, j < LAT_TILES_PER_SEQ - 1)
    seq_len = jnp.where(is_lat, DEC_SEQ, SEQ)
    t0 = jnp.where(is_lat, j * TILE, 0)

    ue_ref[0, 0:HALO, :] = jnp.where(has_prev, up_ref[...], 0.0)
    ue_ref[0, HALO:HALO + TILE, :] = u_ref[...]
    ue_ref[0, HALO + TILE:, :] = jnp.where(has_next, un_ref[...], 0.0)
    ze_ref[0:HALO, :] = jnp.where(has_prev, zp_ref[...], 0.0)
    ze_ref[HALO:HALO + TILE, :] = z_ref[...]
    ze_ref[HALO + TILE:, :] = jnp.where(has_next, zn_ref[...], 0.0)

    pad = CONV_WIDTH // 2
    reach = (HALO - pad + CONV_WIDTH - 1) // SUBLANES * SUBLANES
    groups = CONV_ROWS // SUBLANES
    n_chunks = TILE // CONV_ROWS

    def shift_rows(lo, hi):
        for b in range(1, SUBLANES):
            ue_ref[b, lo:hi, :] = ue_ref[0, lo + b:hi + b, :]

    def conv_chunk(r):
        r0 = r * CONV_ROWS
        acc = jnp.zeros((groups, SUBLANES, CONV_CH), F32) + bdw_ref[...]
        for k in range(CONV_WIDTH):
            a, b = divmod(HALO + k - pad, SUBLANES)
            e0 = r0 + a * SUBLANES
            acc = acc + ue_ref[b, e0:e0 + CONV_ROWS, :].reshape(groups, SUBLANES, CONV_CH) * wdw_ref[k]
        acc = acc.reshape(CONV_ROWS, CONV_CH)
        mu = jnp.mean(acc, axis=-1, keepdims=True)
        d = acc - mu
        var = jnp.mean(d * d, axis=-1, keepdims=True)
        yn = d * lax.rsqrt(var + EPS) * lng_ref[...] + lnb_ref[...]
        o_ref[r0:r0 + CONV_ROWS, 0:CONV_CH] = (yn * jax.nn.sigmoid(yn)).astype(BF16)

    t = t0 + lax.broadcasted_iota(jnp.int32, (TILE, POOL_GROUP_W), 0)
    ze = ze_ref[...]
    z_hi = ze.astype(BF16)
    z_lo = (ze - z_hi.astype(F32)).astype(BF16)

    def pool_group(gi):
        w = POOL_SIZES[gi]
        lo = w // 2
        hi = w - lo - 1
        cols = slice(gi * POOL_GROUP_W, (gi + 1) * POOL_GROUP_W)
        tot = _dot(band_ref[gi], z_hi[:, cols]) + _dot(band_ref[gi], z_lo[:, cols])
        cnt = jnp.minimum(t + hi + 1, seq_len) - jnp.maximum(t - lo, 0)
        d = tot / cnt.astype(F32) - ze_ref[HALO:HALO + TILE, cols]
        y = _dot(d.astype(BF16), wg_ref[gi]) * ps_ref[:, cols]
        o_ref[:, CONV_CH + gi * POOL_GROUP_W:CONV_CH + (gi + 1) * POOL_GROUP_W] = y.astype(BF16)

    shift_rows(0, reach + CONV_ROWS)
    for r in range(n_chunks):
        if r + 1 < n_chunks:
            shift_rows(reach + (r + 1) * CONV_ROWS, reach + (r + 2) * CONV_ROWS)
        conv_chunk(r)
        if r % (n_chunks // len(POOL_SIZES)) == 0:
            pool_group(r // (n_chunks // len(POOL_SIZES)))


def _conv_mix(u, z, cp):
    tile_map = lambda i: (i, 0)
    const2 = lambda i: (0, 0)
    hb = TILE // HALO
    prev_map = lambda i: (jnp.maximum(i * hb - 1, 0), 0)
    next_map = lambda i: (jnp.minimum((i + 1) * hb, T_ALL // HALO - 1), 0)
    return pl.pallas_call(
        _conv_mix_kernel,
        grid=(N_TILES,),
        in_specs=[
            pl.BlockSpec((TILE, CONV_CH), tile_map),
            pl.BlockSpec((HALO, CONV_CH), prev_map),
            pl.BlockSpec((HALO, CONV_CH), next_map),
            pl.BlockSpec((TILE, POOL_CH), tile_map),
            pl.BlockSpec((HALO, POOL_CH), prev_map),
            pl.BlockSpec((HALO, POOL_CH), next_map),
            pl.BlockSpec((CONV_WIDTH, SUBLANES, CONV_CH), lambda i: (0, 0, 0)),
            pl.BlockSpec((1, CONV_CH), const2),
            pl.BlockSpec((1, CONV_CH), const2),
            pl.BlockSpec((1, CONV_CH), const2),
            pl.BlockSpec((len(POOL_SIZES), TILE, TILE + 2 * HALO), lambda i: (0, 0, 0)),
            pl.BlockSpec((len(POOL_SIZES), POOL_GROUP_W, POOL_GROUP_W), lambda i: (0, 0, 0)),
            pl.BlockSpec((1, POOL_CH), const2),
        ],
        out_specs=pl.BlockSpec((TILE, D_MODEL), tile_map),
        out_shape=jax.ShapeDtypeStruct((T_ALL, D_MODEL), BF16),
        scratch_shapes=[pltpu.VMEM((SUBLANES, TILE + 2 * HALO, CONV_CH), F32),
                        pltpu.VMEM((TILE + 2 * HALO, POOL_CH), F32)],
        compiler_params=_cparams(("arbitrary",)),
        name="conv_mix",
    )(u, u, u, z, z, z, cp["w_dw"], cp["b_dw"], cp["ln_g"], cp["ln_b"], _pool_bands(), cp["w_grp"], cp["p_scale"])


def _conv_layer_kernel(zero_ref, x_ref, xp_ref, xn_ref, mod_ref, g_ref, w_ref, wdw_ref, bdw_ref, lng_ref, lnb_ref,
                       band_ref, wg_ref, ps_ref, o_ref, ub_ref, zb_ref, ue_ref, h_ref):
    i = pl.program_id(0)

    def tile_info(t):
        j = (t - N_CTX_TILES) % LAT_TILES_PER_SEQ
        is_lat = t >= N_CTX_TILES
        has_prev = jnp.logical_and(is_lat, j > 0)
        has_next = jnp.logical_and(is_lat, j < LAT_TILES_PER_SEQ - 1)
        return has_prev, has_next, jnp.where(is_lat, DEC_SEQ, SEQ), jnp.where(is_lat, j * TILE, 0)

    @pl.when(i == 0)
    def _():
        ub_ref[...] = jnp.zeros(ub_ref.shape, F32)
        zb_ref[...] = jnp.zeros(zb_ref.shape, F32)

    ext = TILE + 2 * HALO

    has_prev, has_next, _, _ = tile_info(jnp.minimum(i, N_TILES - 1))
    x_ext = jnp.concatenate([xp_ref[...], x_ref[...], xn_ref[...]], axis=0)
    h_ref[0:ext, :] = _rms_mod(x_ext, g_ref[...], mod_ref[0:1, :], mod_ref[1:2, :]).astype(BF16)
    spare = pl.multiple_of(ext + zero_ref[0] * 16, 16)
    half = CONV_CH // 2
    row = lax.broadcasted_iota(jnp.int32, (ext, half), 0)
    inside = jnp.logical_and(jnp.logical_or(row >= HALO, has_prev), jnp.logical_or(row < HALO + TILE, has_next))
    nxt = {}

    def piece(name, c0, fn=None):
        def run():
            y = _dot(h_ref[0:ext, :], w_ref[:, c0:c0 + half])
            nxt[name] = y if fn is None else fn(y)
        return run

    glu = lambda c: lambda gate: jnp.where(inside, nxt.pop("a%d" % c) * jax.nn.sigmoid(gate), 0.0)
    mask = lambda y: jnp.where(inside, y, 0.0)
    pieces = [piece("a0", 0), piece("u0", CONV_CH, glu(0)), piece("a1", half), piece("u1", CONV_CH + half, glu(1)),
              piece("z0", 2 * CONV_CH, mask), piece("z1", 2 * CONV_CH + half, mask)]

    _, _, seq_len, t0 = tile_info(jnp.maximum(i - 1, 0))
    pad = CONV_WIDTH // 2
    reach = (HALO - pad + CONV_WIDTH - 1) // SUBLANES * SUBLANES
    groups = CONV_ROWS // SUBLANES
    n_chunks = TILE // CONV_ROWS

    def shift_rows(lo, hi):
        for b in range(1, SUBLANES):
            ue_ref[b, lo:hi, :] = ub_ref[lo + b:hi + b, :]

    def conv_chunk(r):
        r0 = r * CONV_ROWS
        acc = jnp.zeros((groups, SUBLANES, CONV_CH), F32) + bdw_ref[...]
        for k in range(CONV_WIDTH):
            a, b = divmod(HALO + k - pad, SUBLANES)
            e0 = r0 + a * SUBLANES
            src = ub_ref[e0:e0 + CONV_ROWS, :] if b == 0 else ue_ref[b, e0:e0 + CONV_ROWS, :]
            acc = acc + src.reshape(groups, SUBLANES, CONV_CH) * wdw_ref[k]
        acc = acc.reshape(CONV_ROWS, CONV_CH)
        mu = jnp.mean(acc, axis=-1, keepdims=True)
        d = acc - mu
        var = jnp.mean(d * d, axis=-1, keepdims=True)
        yn = d * lax.rsqrt(var + EPS) * lng_ref[...] + lnb_ref[...]
        res = (yn * jax.nn.sigmoid(yn)).astype(BF16)
        o_ref[r0:r0 + CONV_ROWS, 0:CONV_CH] = res
        h_ref[pl.ds(spare, 16), 0:LANES] = res[0:16, 0:LANES]

    t = t0 + lax.broadcasted_iota(jnp.int32, (TILE, POOL_GROUP_W), 0)
    ze = zb_ref[...]
    z_hi = ze.astype(BF16)
    z_lo = (ze - z_hi.astype(F32)).astype(BF16)

    def pool_group(gi):
        w = POOL_SIZES[gi]
        lo = w // 2
        hi = w - lo - 1
        cols = slice(gi * POOL_GROUP_W, (gi + 1) * POOL_GROUP_W)
        tot = _dot(band_ref[gi], z_hi[:, cols]) + _dot(band_ref[gi], z_lo[:, cols])
        cnt = jnp.minimum(t + hi + 1, seq_len) - jnp.maximum(t - lo, 0)
        d = tot / cnt.astype(F32) - ze[HALO:HALO + TILE, cols]
        y = _dot(d.astype(BF16), wg_ref[gi]) * ps_ref[:, cols]
        o_ref[:, CONV_CH + gi * POOL_GROUP_W:CONV_CH + (gi + 1) * POOL_GROUP_W] = y.astype(BF16)

    shift_rows(0, reach + CONV_ROWS)
    for r in range(n_chunks):
        if r + 1 < n_chunks:
            shift_rows(reach + (r + 1) * CONV_ROWS, reach + (r + 2) * CONV_ROWS)
        if r < len(pieces):
            pieces[r]()
        conv_chunk(r)
        if r % (n_chunks // len(POOL_SIZES)) == 0:
            pool_group(r // (n_chunks // len(POOL_SIZES)))
    for c in range(2):
        ub_ref[:, c * half:(c + 1) * half] = nxt["u%d" % c]
        zb_ref[:, c * half:(c + 1) * half] = nxt["z%d" % c]


def _conv_layer(x, mod, g, w_in, cp, layer):
    const2 = lambda i: (0, 0)
    hb = TILE // HALO
    last = N_TILES - 1
    tile = lambda i: jnp.minimum(i, last)
    mod_row = lambda i: jnp.where(tile(i) < N_CTX_TILES, 0, 1 + (tile(i) - N_CTX_TILES) // LAT_TILES_PER_SEQ)
    return pl.pallas_call(
        _conv_layer_kernel,
        grid=(N_TILES + 1,),
        in_specs=[
            pl.BlockSpec(memory_space=pltpu.SMEM),
            pl.BlockSpec((TILE, D_MODEL), lambda i: (tile(i), 0)),
            pl.BlockSpec((HALO, D_MODEL), lambda i: (jnp.maximum(tile(i) * hb - 1, 0), 0)),
            pl.BlockSpec((HALO, D_MODEL), lambda i: (jnp.minimum((tile(i) + 1) * hb, T_ALL // HALO - 1), 0)),
            pl.BlockSpec((None, None, 6, D_MODEL), lambda i: (layer, mod_row(i), 0, 0)),
            pl.BlockSpec((1, D_MODEL), const2),
            pl.BlockSpec((D_MODEL, 3 * CONV_CH), const2),
            pl.BlockSpec((CONV_WIDTH, SUBLANES, CONV_CH), lambda i: (0, 0, 0)),
            pl.BlockSpec((1, CONV_CH), const2),
            pl.BlockSpec((1, CONV_CH), const2),
            pl.BlockSpec((1, CONV_CH), const2),
            pl.BlockSpec((len(POOL_SIZES), TILE, TILE + 2 * HALO), lambda i: (0, 0, 0)),
            pl.BlockSpec((len(POOL_SIZES), POOL_GROUP_W, POOL_GROUP_W), lambda i: (0, 0, 0)),
            pl.BlockSpec((1, POOL_CH), const2),
        ],
        out_specs=pl.BlockSpec((TILE, D_MODEL), lambda i: (jnp.maximum(i - 1, 0), 0)),
        out_shape=jax.ShapeDtypeStruct((T_ALL, D_MODEL), BF16),
        scratch_shapes=[pltpu.VMEM((TILE + 2 * HALO, CONV_CH), F32),
                        pltpu.VMEM((TILE + 2 * HALO, POOL_CH), F32),
                        pltpu.VMEM((SUBLANES, TILE + 2 * HALO, CONV_CH), F32),
                        pltpu.VMEM((TILE + 2 * HALO + 16, D_MODEL), BF16)],
        compiler_params=_cparams(("arbitrary",)),
        name="conv_layer",
    )(jnp.zeros((1,), jnp.int32), x, x, x, mod, g, w_in, cp["w_dw"], cp["b_dw"], cp["ln_g"], cp["ln_b"], _pool_bands(), cp["w_grp"], cp["p_scale"])


def _pool_bands():
    t = np.arange(TILE)[:, None]
    e = np.arange(TILE + 2 * HALO)[None, :] - HALO
    bands = []
    for w in POOL_SIZES:
        lo = w // 2
        hi = w - lo - 1
        bands.append((e >= t - lo) & (e <= t + hi))
    return jnp.asarray(np.stack(bands), BF16)


W_CHUNKS = 16


def _load_weights_bf16(src, dst, stage, sems):
    rows = src.shape[0] // W_CHUNKS

    def copy(c):
        return pltpu.make_async_copy(src.at[pl.ds(c * rows, rows), :], stage.at[c % 2], sems.at[c % 2])

    copy(0).start()
    for c in range(W_CHUNKS):
        if c + 1 < W_CHUNKS:
            copy(c + 1).start()
        copy(c).wait()
        dst[c * rows:(c + 1) * rows, :] = stage[c % 2].astype(BF16)


def _post_kernel(*refs, n_o, n_x, final, layer):
    o_refs, x_refs = refs[:n_o], refs[n_o:n_o + n_x]
    mod_ref, g_ref, wo_ref, w1_hbm, w2_hbm, fg_ref = refs[n_o + n_x:n_o + n_x + 6]
    n_y = 2 if final else 1
    y_refs = refs[n_o + n_x + 6:n_o + n_x + 6 + n_y]
    w1_ref, w2_ref, st1_ref, st2_ref, sems = refs[n_o + n_x + 6 + n_y:]

    @pl.when(pl.program_id(0) == 0)
    def _():
        _load_weights_bf16(w1_hbm.at[layer], w1_ref, st1_ref, sems.at[0])
        _load_weights_bf16(w2_hbm.at[layer], w2_ref, st2_ref, sems.at[1])

    subs = _sub_rows()
    x1s, hs, acts, outs = [], [], [], []
    for rows in subs:
        x1 = _tok_load(x_refs, rows) + mod_ref[2:3, :] * _dot(_tok_load(o_refs, rows), wo_ref[...])
        x1s.append(x1)
        hs.append(_rms_mod(x1, g_ref[...], mod_ref[3:4, :], mod_ref[4:5, :]).astype(BF16))
    for h in hs:
        a = jnp.maximum(_dot(h, w1_ref[...]), 0.0)
        acts.append((a * a).astype(BF16))
    for rows, x1, a in zip(subs, x1s, acts):
        x2 = x1 + mod_ref[5:6, :] * _dot(a, w2_ref[...])
        if final:
            ms = jnp.mean(x2 * x2, axis=-1, keepdims=True)
            x2 = x2 * lax.rsqrt(ms + EPS) * fg_ref[...]
        outs.append((rows, x2))
    _tok_store(y_refs, outs)


def _post(o, x, mod, g, w_out, w1, w2, final_g, layer, final):
    const2 = lambda i: (0, 0)
    o_ops, o_specs = _tok_specs(o, D_MODEL)
    x_ops, x_specs = _tok_specs(x, D_MODEL)
    if final:
        y = (jax.ShapeDtypeStruct((T_CTX, D_MODEL), F32), jax.ShapeDtypeStruct((T_LAT, D_MODEL), F32))
    else:
        y = jax.ShapeDtypeStruct((T_ALL, D_MODEL), F32)
    _, y_specs = _tok_specs(y, D_MODEL)
    out = pl.pallas_call(
        functools.partial(_post_kernel, n_o=len(o_ops), n_x=len(x_ops), final=final, layer=layer),
        grid=(N_TOK,),
        in_specs=o_specs + x_specs + [
            pl.BlockSpec((None, None, 6, D_MODEL), lambda i: (layer, _mod_row(i), 0, 0)),
            pl.BlockSpec((1, D_MODEL), const2),
            pl.BlockSpec((D_MODEL, D_MODEL), const2, pipeline_mode=pl.Buffered(1)),
            pl.BlockSpec(memory_space=pl.ANY),
            pl.BlockSpec(memory_space=pl.ANY),
            pl.BlockSpec((1, D_MODEL), const2),
        ],
        out_specs=y_specs if final else y_specs[0],
        out_shape=y,
        scratch_shapes=[pltpu.VMEM((D_MODEL, D_FF), BF16), pltpu.VMEM((D_FF, D_MODEL), BF16),
                        pltpu.VMEM((2, D_MODEL // W_CHUNKS, D_FF), F32), pltpu.VMEM((2, D_FF // W_CHUNKS, D_MODEL), F32),
                        pltpu.SemaphoreType.DMA((2, 2))],
        compiler_params=_cparams(("arbitrary",)),
        name="post_final" if final else "post",
    )(*o_ops, *x_ops, mod, g, w_out, w1, w2, final_g)
    return out


def _pair_lanes(w):
    lead = w.shape[:-1]
    w = w.reshape(lead + (w.shape[-1] // LANES, 2, 2, HEAD_DIM // 2))
    return jnp.swapaxes(w, -3, -2).reshape(lead + (-1,))


def _mla_lanes(nope, rope):
    shape = (nope if nope is not None else rope).shape[:-1]
    nope = jnp.zeros(shape + (QK_NOPE,), F32) if nope is None else nope
    rope = jnp.zeros(shape + (QK_ROPE,), F32) if rope is None else rope
    h = QK_ROPE // 2
    return jnp.concatenate([rope[..., :h], nope[..., :64 - h], rope[..., h:], nope[..., 64 - h:],
                            jnp.zeros(shape + (LANES - QK_NOPE - QK_ROPE,), F32)], axis=-1)


def _rope_tables():
    n = DEC_SEQ
    rows = n // GRID_W
    row = jnp.repeat(jnp.arange(rows), GRID_W).astype(F32)
    col = jnp.tile(jnp.arange(GRID_W), rows).astype(F32)

    def angles(dim):
        quarter = dim // 4
        inv_freq = ROPE_BASE ** (-jnp.arange(quarter, dtype=F32) / quarter)
        return jnp.concatenate([row[:, None] * inv_freq, col[:, None] * inv_freq], axis=-1)

    ang_a = angles(HEAD_DIM)
    cos_a, sin_a = jnp.cos(ang_a), jnp.sin(ang_a)
    c_a = jnp.concatenate([cos_a] * 4, axis=-1)
    s_a = jnp.concatenate([-sin_a, -sin_a, sin_a, sin_a], axis=-1)
    ang_b = angles(QK_ROPE)
    cos_b, sin_b = jnp.cos(ang_b), jnp.sin(ang_b)
    c_b = _mla_lanes(jnp.ones((n, QK_NOPE), F32), jnp.concatenate([cos_b, cos_b], axis=-1))
    s_b = _mla_lanes(None, jnp.concatenate([-sin_b, sin_b], axis=-1))
    lat = jnp.stack([c_a, s_a, c_b, s_b])
    ident = jnp.stack([jnp.ones((TOK, LANES), F32), jnp.zeros((TOK, LANES), F32)] * 2)
    return jnp.concatenate([ident, lat], axis=1)


def _prep_even(attn_w_in, mla_q_norm, mla_kv_norm, mla_w_qb, mla_w_kvb, attn_w_out, i):
    w = attn_w_in[i]
    o = np.cumsum((0, 512, 128, 128, Q_LORA, KV_LORA, QK_ROPE))
    zeros = lambda n: jnp.zeros((D_MODEL, n), F32)
    w_in = jnp.concatenate([
        _pair_lanes(w[:, o[0]:o[1]]), _pair_lanes(w[:, o[1]:o[2]]), w[:, o[2]:o[3]],
        w[:, o[3]:o[4]], zeros(Q_LORA_PAD - Q_LORA),
        w[:, o[4]:o[5]],
        _mla_lanes(None, w[:, o[5]:o[6]]),
    ], axis=1).astype(BF16)
    q_norm = jnp.pad(mla_q_norm[i], (0, Q_LORA_PAD - Q_LORA)).reshape(1, Q_LORA_PAD)
    wqb = mla_w_qb[i].reshape(Q_LORA, B_HEADS, QK_NOPE + QK_ROPE)
    wqb = jnp.pad(_mla_lanes(wqb[..., :QK_NOPE], wqb[..., QK_NOPE:]), ((0, Q_LORA_PAD - Q_LORA), (0, 0), (0, 0)))
    wkvb = mla_w_kvb[i].reshape(KV_LORA, B_HEADS, QK_NOPE + V_DIM)
    w_kk = _mla_lanes(wkvb[:, :, :QK_NOPE], None)
    w_kv = wkvb[:, :, QK_NOPE:]
    return {
        "w_in": w_in,
        "q_norm": q_norm,
        "kv_norm": mla_kv_norm[i].reshape(1, KV_LORA),
        "w_qb": wqb.reshape(Q_LORA_PAD, B_HEADS * LANES).astype(BF16),
        "w_kk": w_kk.reshape(KV_LORA, B_HEADS * LANES).astype(BF16),
        "w_kv": w_kv.reshape(KV_LORA, B_HEADS * V_DIM).T.astype(BF16),
        "w_out": attn_w_out[i].astype(BF16),
    }


def kernel(x_prompt, x_sample, cache_win_k, cache_win_v, cache_mla_ckv, cache_mla_krope, c, c_ctx, w_mod, b_mod,
           norm_g, attn_w_in, attn_sink, mla_q_norm, mla_kv_norm, mla_w_qb, mla_w_kvb, attn_w_out, conv_w_in,
           conv_dw, conv_dw_b, conv_ln_g, conv_ln_b, pool_w, pool_scale, conv_w_out, mlp_w1, mlp_w2, final_g):
    x = (x_prompt.reshape(T_CTX, D_MODEL), x_sample.reshape(T_LAT, D_MODEL))

    cond = jnp.concatenate([c_ctx[None, :], c, jnp.zeros((N_COND - 1 - DEC_BATCH, D_MODEL), F32)], axis=0)
    mod = _mod_table(cond, w_mod, b_mod).reshape(DEPTH, N_COND, 6, D_MODEL)

    rope = _rope_tables()
    even = [_prep_even(attn_w_in, mla_q_norm, mla_kv_norm, mla_w_qb, mla_w_kvb, attn_w_out, i)
            for i in range(N_EVEN)]
    cache_kr128 = _mla_lanes(None, cache_mla_krope)
    kdc, vdc, kmc, vmc = _ctx_kv(
        _pair_lanes(cache_win_k.reshape(DEC_BATCH, N_EVEN, PAST_LEN, LANES)),
        cache_win_v.reshape(DEC_BATCH, N_EVEN, PAST_LEN, LANES),
        cache_mla_ckv, cache_kr128,
        jnp.stack([e["w_kk"] for e in even]), jnp.stack([e["w_kv"] for e in even]))
    final_g2 = final_g.reshape(1, D_MODEL)
    w1_all, w2_all = mlp_w1, mlp_w2

    ks, vs, ckvs, krs = [], [], [], []
    for l in range(DEPTH):
        g_mix = norm_g[l, 0].reshape(1, D_MODEL)
        g_mlp = norm_g[l, 1].reshape(1, D_MODEL)
        if l % 2 == 0:
            i = l // 2
            wp = even[i]
            qa, kd, vd, qm, km, vm, ka32, va32, ckv32, kr32 = _attn_in(x, mod, g_mix, wp, rope, l)
            ka32 = jnp.swapaxes(ka32.reshape(T_CTX, 2, A_KV_HEADS, HEAD_DIM // 2), 1, 2)
            ks.append(ka32.reshape(BATCH, SEQ, A_KV_HEADS, HEAD_DIM))
            vs.append(va32.reshape(BATCH, SEQ, A_KV_HEADS, HEAD_DIM))
            ckvs.append(ckv32.reshape(BATCH, SEQ, KV_LORA))
            kr32 = jnp.concatenate([kr32[:, :QK_ROPE // 2], kr32[:, 64:64 + QK_ROPE // 2]], axis=-1)
            krs.append(kr32.reshape(BATCH, SEQ, QK_ROPE))
            o = _attention(attn_sink[i] * LOG2E, qa, kd, vd, qm, km, vm, kdc, vdc, kmc, vmc, i)
            w_out = wp["w_out"]
        else:
            jj = l // 2
            cp = {
                "w_dw": jnp.broadcast_to(conv_dw[jj][:, None, :], (CONV_WIDTH, SUBLANES, CONV_CH)),
                "b_dw": conv_dw_b[jj].reshape(1, CONV_CH),
                "ln_g": conv_ln_g[jj].reshape(1, CONV_CH),
                "ln_b": conv_ln_b[jj].reshape(1, CONV_CH),
                "w_grp": pool_w[jj].astype(BF16),
                "p_scale": pool_scale[jj].reshape(1, POOL_CH),
            }
            u, z = _conv_in(x, mod, g_mix, conv_w_in[jj].astype(BF16), l)
            o = _conv_mix(u, z, cp)
            w_out = conv_w_out[jj].astype(BF16)
        x = _post(o, x, mod, g_mlp, w_out, w1_all, w2_all, final_g2, l, final=(l == DEPTH - 1))

    y_prompt = x[0].reshape(BATCH, SEQ, D_MODEL)
    y_sample = x[1].reshape(DEC_BATCH, DEC_SEQ, D_MODEL)
    return (y_prompt, y_sample, jnp.stack(ks, axis=1), jnp.stack(vs, axis=1),
            jnp.stack(ckvs, axis=1), jnp.stack(krs, axis=1))
```

```python
import functools
import math

import numpy as np
import jax
import jax.numpy as jnp
from jax import lax
from jax.experimental import pallas as pl
from jax.experimental.pallas import tpu as pltpu

F32 = jnp.float32
BF16 = jnp.bfloat16

D_MODEL = 1024
BATCH = 16
SEQ = 256
DEPTH = 4
DEC_BATCH = 8
DEC_SEQ = 2048
PAST_LEN = 256
GRID_W = 64
N_EVEN = (DEPTH + 1) // 2
N_ODD = DEPTH // 2
A_HEADS = 8
A_KV_HEADS = 2
A_GROUP = A_HEADS // A_KV_HEADS
HEAD_DIM = 64
WINDOW = 128
B_HEADS = 8
Q_LORA = 192
KV_LORA = 128
QK_NOPE = 64
QK_ROPE = 32
V_DIM = 64
MLA_SCALE = (QK_NOPE + QK_ROPE) ** -0.5
CONV_CH = D_MODEL // 2
CONV_WIDTH = 31
POOL_CH = D_MODEL // 2
POOL_SIZES = (2, 4, 8, 16)
POOL_GROUP_W = POOL_CH // len(POOL_SIZES)
D_FF = 4 * D_MODEL
ROPE_BASE = 10000.0
EPS = 1e-6
NEG_INF = -1e30
LOG2E = math.log2(math.e)

LANES = 128
SUBLANES = 8
VMEM_LIMIT_BYTES = 56 * 1024 * 1024

TILE = 256
T_CTX = BATCH * SEQ
T_LAT = DEC_BATCH * DEC_SEQ
T_ALL = T_CTX + T_LAT
N_CTX_TILES = T_CTX // TILE
N_TILES = T_ALL // TILE
LAT_TILES_PER_SEQ = DEC_SEQ // TILE
TOK = 512
N_CTX_TOK = T_CTX // TOK
N_TOK = T_ALL // TOK
LAT_TOK_PER_SEQ = DEC_SEQ // TOK
N_COND = 16
BQ = 256
VT_ROWS = 80
HALO = 16
Q_LORA_PAD = 256
ATTN_IN_COLS = 512 + 128 + 128 + Q_LORA_PAD + 128 + 128


def _cparams(sem):
    return pltpu.CompilerParams(dimension_semantics=sem, vmem_limit_bytes=VMEM_LIMIT_BYTES)


def _mod_row(i):
    return jnp.where(i < N_CTX_TOK, 0, 1 + (i - N_CTX_TOK) // LAT_TOK_PER_SEQ)


def _pos_block(i):
    return jnp.where(i < N_CTX_TOK, 0, 1 + (i - N_CTX_TOK) % LAT_TOK_PER_SEQ)


def _tok_specs(a, width):
    if isinstance(a, tuple):
        return list(a), [pl.BlockSpec((TOK, width), lambda i: (jnp.minimum(i, N_CTX_TOK - 1), 0)),
                         pl.BlockSpec((TOK, width), lambda i: (jnp.maximum(i - N_CTX_TOK, 0), 0))]
    return [a], [pl.BlockSpec((TOK, width), lambda i: (i, 0))]


def _tok_load(refs, rows):
    if len(refs) == 1:
        return refs[0][rows, :]
    return jnp.where(pl.program_id(0) < N_CTX_TOK, refs[0][rows, :], refs[1][rows, :])


def _tok_store(refs, parts):
    def put(ref):
        for rows, val in parts:
            ref[rows, :] = val

    if len(refs) == 1:
        put(refs[0])
        return
    i = pl.program_id(0)

    @pl.when(i < N_CTX_TOK)
    def _():
        put(refs[0])

    @pl.when(i >= N_CTX_TOK)
    def _():
        put(refs[1])


def _layer_spec(shape, idx):
    zeros = (0,) * len(shape)
    return pl.BlockSpec((None,) + tuple(shape), lambda *_: (idx,) + zeros)


def _sub_rows():
    return [slice(s * TILE, (s + 1) * TILE) for s in range(TOK // TILE)]


def _dot(a, b):
    return jnp.dot(a, b, preferred_element_type=F32)


def _dot_nt(a, b):
    return lax.dot_general(a, b, (((1,), (1,)), ((), ())), preferred_element_type=F32)


def _rms_mod(x, g, shift, scale):
    ms = jnp.mean(x * x, axis=-1, keepdims=True)
    return x * lax.rsqrt(ms + EPS) * (g * (1.0 + scale)) + shift


def _rope(x, c, s):
    return x * c + pltpu.roll(x, 64, 1) * s


def _lane_lt64(shape):
    return lax.broadcasted_iota(jnp.int32, shape, len(shape) - 1) < 64


def _lane_even32(shape):
    return (lax.broadcasted_iota(jnp.int32, shape, len(shape) - 1) & 63) < 32


def _store_kv_dup(kd_ref, vd_ref, rows, k, v):
    even32 = _lane_even32(k.shape)
    lt64 = _lane_lt64(v.shape)
    kd_ref[rows, 0:LANES] = jnp.where(even32, k, pltpu.roll(k, 32, 1)).astype(BF16)
    kd_ref[rows, LANES:2 * LANES] = jnp.where(even32, pltpu.roll(k, LANES - 32, 1), k).astype(BF16)
    vd_ref[rows, 0:LANES] = jnp.where(lt64, v, 1.0).astype(BF16)
    vd_ref[rows, LANES:2 * LANES] = jnp.where(lt64, pltpu.roll(v, 64, 1), 1.0).astype(BF16)


def _store_vt(vt_ref, cols, ckvn, wkvt_ref):
    vt = _dot(wkvt_ref[...], ckvn.T.astype(BF16))
    ones = jnp.ones((VT_ROWS - V_DIM, ckvn.shape[0]), BF16)
    for hh in range(B_HEADS):
        vt_ref[hh * VT_ROWS:hh * VT_ROWS + V_DIM, cols] = vt[hh * V_DIM:(hh + 1) * V_DIM].astype(BF16)
        vt_ref[hh * VT_ROWS + V_DIM:(hh + 1) * VT_ROWS, cols] = ones


def _mod_kernel(cond_ref, w_ref, b_ref, o_ref):
    c = cond_ref[...]
    s = c * jax.nn.sigmoid(c)
    o_ref[...] = _dot(s.astype(BF16), w_ref[...].astype(BF16)) + b_ref[...]


def _mod_table(cond, w_mod, b_mod):
    nb = 6 * D_MODEL // 1024
    return pl.pallas_call(
        _mod_kernel,
        grid=(DEPTH, nb),
        in_specs=[
            pl.BlockSpec((N_COND, D_MODEL), lambda l, n: (0, 0)),
            pl.BlockSpec((None, D_MODEL, 1024), lambda l, n: (l, 0, n)),
            pl.BlockSpec((None, 1, 1024), lambda l, n: (l, 0, n)),
        ],
        out_specs=pl.BlockSpec((None, N_COND, 1024), lambda l, n: (l, 0, n)),
        out_shape=jax.ShapeDtypeStruct((DEPTH, N_COND, 6 * D_MODEL), F32),
        compiler_params=_cparams(("arbitrary", "arbitrary")),
        name="mod_table",
    )(cond, w_mod, b_mod.reshape(DEPTH, 1, 6 * D_MODEL))


def _attn_in_kernel(*refs, n_x):
    x_refs = refs[:n_x]
    (mod_ref, g_ref, w_ref, qn_ref, kvn_ref, wqb_ref, wkk_ref, wkv_ref, rope_ref,
     qa_ref, kd_ref, vd_ref, qm_ref, km_ref, vm_ref, ka32_ref, va32_ref, ckv32_ref, kr32_ref) = refs[n_x:]
    col = lambda n: slice(n * LANES, (n + 1) * LANES)

    def project(rows):
        h = _rms_mod(_tok_load(x_refs, rows), g_ref[...], mod_ref[0:1, :], mod_ref[1:2, :])
        return _dot(h.astype(BF16), w_ref[...])

    def derive(rows, y):
        ca, sa = rope_ref[0, rows, :], rope_ref[1, rows, :]
        cb, sb = rope_ref[2, rows, :], rope_ref[3, rows, :]

        for c in range(4):
            q = _rope(y[:, col(c)], ca, sa)
            qa_ref[rows, col(c)] = (q * (HEAD_DIM ** -0.5 * LOG2E)).astype(BF16)

        ka = _rope(y[:, 512:640], ca, sa)
        va = y[:, 640:768]
        _store_kv_dup(kd_ref, vd_ref, rows, ka, va)

        cq = y[:, 768:768 + Q_LORA_PAD]
        cqn = cq * lax.rsqrt(jnp.sum(cq * cq, axis=-1, keepdims=True) * (1.0 / Q_LORA) + EPS) * qn_ref[...]
        qm = _dot(cqn.astype(BF16), wqb_ref[...])
        for hh in range(B_HEADS):
            q = _rope(qm[:, col(hh)], cb, sb)
            qm_ref[rows, col(hh)] = (q * (MLA_SCALE * LOG2E)).astype(BF16)

        ckv = y[:, 1024:1152]
        ckvn = ckv * lax.rsqrt(jnp.mean(ckv * ckv, axis=-1, keepdims=True) + EPS) * kvn_ref[...]
        kr = y[:, 1152:1280]
        ckvn_b = ckvn.astype(BF16)
        kn = _dot(ckvn_b, wkk_ref[...])
        krr = _rope(kr, cb, sb)
        for hh in range(B_HEADS):
            km_ref[rows, col(hh)] = (kn[:, col(hh)] + krr).astype(BF16)
        _store_vt(vm_ref, rows, ckvn, wkv_ref)
        return ka, va, ckvn, kr

    subs = _sub_rows()
    cache = []
    y = project(subs[0])
    for s, rows in enumerate(subs):
        y_next = project(subs[s + 1]) if s + 1 < len(subs) else None
        cache.append(derive(rows, y))
        y = y_next

    @pl.when(pl.program_id(0) < N_CTX_TOK)
    def _():
        for rows, (ka, va, ckvn, kr) in zip(subs, cache):
            ka32_ref[rows, :] = ka
            va32_ref[rows, :] = va
            ckv32_ref[rows, :] = ckvn
            kr32_ref[rows, :] = kr


def _attn_in(x, mod, norm_g, wp, rope, layer):
    tile_map = lambda i: (i, 0)
    li = layer // 2
    ctx_map = lambda i: (jnp.minimum(i, N_CTX_TOK - 1), 0)
    bf = lambda w: jax.ShapeDtypeStruct((T_ALL, w), BF16)
    c32 = jax.ShapeDtypeStruct((T_CTX, LANES), F32)
    x_ops, x_specs = _tok_specs(x, D_MODEL)
    return pl.pallas_call(
        functools.partial(_attn_in_kernel, n_x=len(x_ops)),
        grid=(N_TOK,),
        in_specs=x_specs + [
            pl.BlockSpec((None, None, 6, D_MODEL), lambda i: (layer, _mod_row(i), 0, 0)),
            _layer_spec((1, D_MODEL), 2 * layer),
            _layer_spec((D_MODEL, ATTN_IN_COLS), li),
            _layer_spec((1, Q_LORA_PAD), li),
            _layer_spec((1, KV_LORA), li),
            _layer_spec((Q_LORA_PAD, B_HEADS * LANES), li),
            _layer_spec((KV_LORA, B_HEADS * LANES), li),
            _layer_spec((B_HEADS * V_DIM, KV_LORA), li),
            pl.BlockSpec((4, TOK, LANES), lambda i: (0, _pos_block(i), 0)),
        ],
        out_specs=[
            pl.BlockSpec((TOK, 512), tile_map),
            pl.BlockSpec((TOK, 256), tile_map),
            pl.BlockSpec((TOK, 256), tile_map),
            pl.BlockSpec((TOK, 1024), tile_map),
            pl.BlockSpec((TOK, 1024), tile_map),
            pl.BlockSpec((B_HEADS * VT_ROWS, TOK), lambda i: (0, i)),
            pl.BlockSpec((TOK, LANES), ctx_map),
            pl.BlockSpec((TOK, LANES), ctx_map),
            pl.BlockSpec((TOK, LANES), ctx_map),
            pl.BlockSpec((TOK, LANES), ctx_map),
        ],
        out_shape=[bf(512), bf(256), bf(256), bf(1024), bf(1024),
                   jax.ShapeDtypeStruct((B_HEADS * VT_ROWS, T_ALL), BF16), c32, c32, c32, c32],
        compiler_params=_cparams(("arbitrary",)),
        name="attn_in",
    )(*x_ops, mod, norm_g, wp["w_in"], wp["q_norm"], wp["kv_norm"], wp["w_qb"], wp["w_kk"], wp["w_kv"], rope)


def _ctx_kv_kernel(ck_ref, cv_ref, cckv_ref, ckr_ref, wkk_ref, wkv_ref, kd_ref, vd_ref, km_ref, vm_ref):
    _store_kv_dup(kd_ref, vd_ref, slice(None), ck_ref[...], cv_ref[...])
    cb = cckv_ref[...].astype(BF16)
    kn = _dot(cb, wkk_ref[...])
    kr = ckr_ref[...]
    for hh in range(B_HEADS):
        km_ref[:, hh * LANES:(hh + 1) * LANES] = (kn[:, hh * LANES:(hh + 1) * LANES] + kr).astype(BF16)
    _store_vt(vm_ref, slice(None), cckv_ref[...], wkv_ref)


def _ctx_kv(cache_k, cache_v, cache_ckv, cache_kr128, w_kk, w_kv):
    cache_map = lambda i, b: (b, i, 0, 0)
    w_map = lambda i, b: (i, 0, 0)
    out_map = lambda i, b: (i, b, 0, 0)
    o = lambda w: jax.ShapeDtypeStruct((N_EVEN, DEC_BATCH, PAST_LEN, w), BF16)
    return pl.pallas_call(
        _ctx_kv_kernel,
        grid=(N_EVEN, DEC_BATCH),
        in_specs=[
            pl.BlockSpec((None, None, PAST_LEN, LANES), cache_map),
            pl.BlockSpec((None, None, PAST_LEN, LANES), cache_map),
            pl.BlockSpec((None, None, PAST_LEN, KV_LORA), cache_map),
            pl.BlockSpec((None, None, PAST_LEN, LANES), cache_map),
            pl.BlockSpec((None, KV_LORA, B_HEADS * LANES), w_map),
            pl.BlockSpec((None, B_HEADS * V_DIM, KV_LORA), w_map),
        ],
        out_specs=[
            pl.BlockSpec((None, None, PAST_LEN, 256), out_map),
            pl.BlockSpec((None, None, PAST_LEN, 256), out_map),
            pl.BlockSpec((None, None, PAST_LEN, 1024), out_map),
            pl.BlockSpec((None, None, B_HEADS * VT_ROWS, PAST_LEN), out_map),
        ],
        out_shape=[o(256), o(256), o(1024),
                   jax.ShapeDtypeStruct((N_EVEN, DEC_BATCH, B_HEADS * VT_ROWS, PAST_LEN), BF16)],
        compiler_params=_cparams(("arbitrary", "arbitrary")),
        name="ctx_kv",
    )(cache_k, cache_v, cache_ckv, cache_kr128, w_kk, w_kv)


def _attn_heads(qa_ref, qm_ref, a_segs, m_segs, sink_ref, o_ref, rows, depth):
    lt64 = _lane_lt64((rows, LANES))
    lane = lax.broadcasted_iota(jnp.int32, (1, LANES), 1)
    keep = (((lane & 63) < 32).astype(BF16), ((lane & 63) >= 32).astype(BF16))
    col = lambda n: slice(n * LANES, (n + 1) * LANES)
    swap = lambda r: pltpu.roll(r, 64, 1)

    jobs = []
    for kh in range(A_KV_HEADS):
        heads = [dict(out=(kh * A_GROUP + g) // 2, half=g % 2, sink=kh * A_GROUP + g) for g in range(A_GROUP)]
        jobs.append(dict(
            heads=heads,
            q=lambda heads=heads: jnp.concatenate(
                [qa_ref[:, col(h["out"])] * keep[h["half"]] for h in heads], axis=0),
            ks=[lambda kd=kd, rs=rs, kh=kh: kd[rs, col(kh)] for kd, _, rs, _ in a_segs],
            vs=[lambda vd=vd, rs=rs, kh=kh: vd[rs, col(kh)] for _, vd, rs, _ in a_segs],
            valids=[valid for _, _, _, valid in a_segs]))
    for hh in range(B_HEADS):
        jobs.append(dict(
            keys_on_rows=True,
            heads=[dict(out=4 + hh // 2, half=hh % 2)],
            q=lambda hh=hh: qm_ref[:, col(hh)],
            ks=[lambda km=km, hh=hh: km[:, col(hh)] for km, _ in m_segs],
            vs=[lambda vmt=vmt, hh=hh: vmt[hh * VT_ROWS:(hh + 1) * VT_ROWS, :] for _, vmt in m_segs]))

    def scores(job):
        if job.get("keys_on_rows"):
            h = job["heads"][0]
            q = job["q"]()
            h["ss"] = [_dot_nt(k(), q) for k in job["ks"]]
            m = h["ss"][0].max(axis=0, keepdims=True)
            for s in h["ss"][1:]:
                m = jnp.maximum(m, s.max(axis=0, keepdims=True))
            h["m"] = m
            return
        full = [_dot_nt(job["q"](), k()) for k in job["ks"]]
        for b, h in enumerate(job["heads"]):
            blk = slice(b * rows, (b + 1) * rows)
            ss = [s[blk] if valid is None else jnp.where(valid, s[blk], NEG_INF)
                  for s, valid in zip(full, job["valids"])]
            m = ss[0].max(axis=-1, keepdims=True)
            for s in ss[1:]:
                m = jnp.maximum(m, s.max(axis=-1, keepdims=True))
            sink = sink_ref[h["sink"]]
            m = jnp.maximum(m, sink)
            h["e"] = jnp.exp2(sink - m)
            h["ss"], h["m"] = ss, m

    def values(job):
        if job.get("keys_on_rows"):
            h = job["heads"][0]
            r = None
            for s, vt in zip(h.pop("ss"), job["vs"]):
                rs_ = _dot(vt(), jnp.exp2(s - h["m"]).astype(BF16))
                r = rs_ if r is None else r + rs_
            del h["m"]
            h["r"] = r
            return
        r = None
        for i, v in enumerate(job["vs"]):
            p = [jnp.exp2(h["ss"][i] - h["m"]).astype(BF16) for h in job["heads"]]
            rs_ = _dot(jnp.concatenate(p, axis=0), v())
            r = rs_ if r is None else r + rs_
        for b, h in enumerate(job["heads"]):
            del h["ss"], h["m"]
            h["r"] = r[b * rows:(b + 1) * rows]

    def finish(job, done):
        if job.get("keys_on_rows"):
            h = job["heads"][0]
            r = h.pop("r")
            o_t = r[0:V_DIM] / r[V_DIM:V_DIM + 1]
            other = done.pop(h["out"], None)
            if other is None:
                done[h["out"]] = o_t
            else:
                lo, hi = (other, o_t) if h["half"] == 1 else (o_t, other)
                o_ref[:, col(h["out"])] = jnp.concatenate([lo, hi], axis=0).T.astype(BF16)
            return
        for h in job["heads"]:
            r = h.pop("r")
            if h["half"] == 1:
                o = swap(r) / (r + h["e"])
            else:
                o = r / (swap(r) + h["e"])
            other = done.pop(h["out"], None)
            if other is None:
                done[h["out"]] = o
            else:
                lo, hi = (other, o) if h["half"] == 1 else (o, other)
                o_ref[:, col(h["out"])] = jnp.where(lt64, lo, hi).astype(BF16)

    done = {}
    for t in range(len(jobs) + depth):
        if t < len(jobs):
            scores(jobs[t])
        if t >= depth:
            values(jobs[t - depth])
            finish(jobs[t - depth], done)


def _attn_lat_kernel(sink_ref, qa_ref, qm_ref, kd_ref, vd_ref, km_ref, vm_ref,
                     kdc_ref, vdc_ref, kmc_ref, vmc_ref, o_ref):
    j = pl.program_id(1)
    q0 = j * BQ
    nloc = BQ + 2 * WINDOW
    start = pl.multiple_of(jnp.clip(q0 - WINDOW, 0, DEC_SEQ - nloc), WINDOW)
    qpos = q0 + lax.broadcasted_iota(jnp.int32, (BQ, nloc), 0)
    kpos = start + lax.broadcasted_iota(jnp.int32, (BQ, nloc), 1)
    valid = jnp.abs(qpos - kpos) <= WINDOW
    everything = slice(None)
    a_segs = [(kd_ref, vd_ref, pl.ds(start, nloc), valid), (kdc_ref, vdc_ref, everything, None)]
    m_segs = [(km_ref, vm_ref), (kmc_ref, vmc_ref)]
    _attn_heads(qa_ref, qm_ref, a_segs, m_segs, sink_ref, o_ref, BQ, depth=2)


def _attn_ctx_kernel(sink_ref, qa_ref, qm_ref, kd_ref, vd_ref, km_ref, vm_ref, o_ref):
    a_segs = [(kd_ref, vd_ref, slice(None), None)]
    m_segs = [(km_ref, vm_ref)]
    _attn_heads(qa_ref, qm_ref, a_segs, m_segs, sink_ref, o_ref, SEQ, depth=1)


def _attention(sink2, qa, kd, vd, qm, km, vm, kdc, vdc, kmc, vmc, layer_i):
    smem = pl.BlockSpec(memory_space=pltpu.SMEM)
    nq = DEC_SEQ // BQ
    q_off = T_CTX // BQ
    s_off = T_CTX // DEC_SEQ
    q_map = lambda b, j: (q_off + b * nq + j, 0)
    kv_map = lambda b, j: (s_off + b, 0)
    c_map = lambda b, j: (layer_i, b, 0, 0)
    o_lat = pl.pallas_call(
        _attn_lat_kernel,
        grid=(DEC_BATCH, nq),
        in_specs=[
            smem,
            pl.BlockSpec((BQ, 512), q_map),
            pl.BlockSpec((BQ, 1024), q_map),
            pl.BlockSpec((DEC_SEQ, 256), kv_map),
            pl.BlockSpec((DEC_SEQ, 256), kv_map),
            pl.BlockSpec((DEC_SEQ, 1024), kv_map),
            pl.BlockSpec((B_HEADS * VT_ROWS, DEC_SEQ), lambda b, j: (0, s_off + b)),
            pl.BlockSpec((None, None, PAST_LEN, 256), c_map),
            pl.BlockSpec((None, None, PAST_LEN, 256), c_map),
            pl.BlockSpec((None, None, PAST_LEN, 1024), c_map),
            pl.BlockSpec((None, None, B_HEADS * VT_ROWS, PAST_LEN), c_map),
        ],
        out_specs=pl.BlockSpec((BQ, 1024), lambda b, j: (b * nq + j, 0)),
        out_shape=jax.ShapeDtypeStruct((T_LAT, 1024), BF16),
        compiler_params=_cparams(("arbitrary", "arbitrary")),
        name="attn_latent",
    )(sink2, qa, qm, kd, vd, km, vm, kdc, vdc, kmc, vmc)
    b_map = lambda b: (b, 0)
    o_ctx = pl.pallas_call(
        _attn_ctx_kernel,
        grid=(BATCH,),
        in_specs=[
            smem,
            pl.BlockSpec((SEQ, 512), b_map),
            pl.BlockSpec((SEQ, 1024), b_map),
            pl.BlockSpec((SEQ, 256), b_map),
            pl.BlockSpec((SEQ, 256), b_map),
            pl.BlockSpec((SEQ, 1024), b_map),
            pl.BlockSpec((B_HEADS * VT_ROWS, SEQ), lambda b: (0, b)),
        ],
        out_specs=pl.BlockSpec((SEQ, 1024), b_map),
        out_shape=jax.ShapeDtypeStruct((T_CTX, 1024), BF16),
        compiler_params=_cparams(("arbitrary",)),
        name="attn_context",
    )(sink2, qa, qm, kd, vd, km, vm)
    return (o_ctx, o_lat)


def _conv_in_kernel(x_ref, mod_ref, g_ref, w_ref, u_ref, z_ref):
    for rows in _sub_rows():
        h = _rms_mod(x_ref[rows, :], g_ref[...], mod_ref[0:1, :], mod_ref[1:2, :])
        y = _dot(h.astype(BF16), w_ref[...])
        a = y[:, 0:CONV_CH]
        gate = y[:, CONV_CH:2 * CONV_CH]
        u_ref[rows, :] = a * jax.nn.sigmoid(gate)
        z_ref[rows, :] = y[:, 2 * CONV_CH:]


def _conv_in(x, mod, norm_g, w_in, layer):
    tile_map = lambda i: (i, 0)
    return pl.pallas_call(
        _conv_in_kernel,
        grid=(N_TOK,),
        in_specs=[
            pl.BlockSpec((TOK, D_MODEL), tile_map),
            pl.BlockSpec((None, None, 6, D_MODEL), lambda i: (layer, _mod_row(i), 0, 0)),
            _layer_spec((1, D_MODEL), 2 * layer),
            _layer_spec((D_MODEL, 3 * CONV_CH), layer // 2),
        ],
        out_specs=[pl.BlockSpec((TOK, CONV_CH), tile_map), pl.BlockSpec((TOK, POOL_CH), tile_map)],
        out_shape=[jax.ShapeDtypeStruct((T_ALL, CONV_CH), F32), jax.ShapeDtypeStruct((T_ALL, POOL_CH), F32)],
        compiler_params=_cparams(("arbitrary",)),
        name="conv_in",
    )(x, mod, norm_g, w_in)


CONV_ROWS = 32


def _conv_mix_kernel(u_ref, up_ref, un_ref, z_ref, zp_ref, zn_ref, wdw_ref, bdw_ref, lng_ref, lnb_ref,
                     band_ref, wg_ref, ps_ref, o_ref, ue_ref, ze_ref):
    i = pl.program_id(0)
    j = (i - N_CTX_TILES) % LAT_TILES_PER_SEQ
    is_lat = i >= N_CTX_TILES
    has_prev = jnp.logical_and(is_lat, j > 0)
    has_next = jnp.logical_and(is_lat, j < LAT_TILES_PER_SEQ - 1)
    seq_len = jnp.where(is_lat, DEC_SEQ, SEQ)
    t0 = jnp.where(is_lat, j * TILE, 0)

    ue_ref[0, 0:HALO, :] = jnp.where(has_prev, up_ref[...], 0.0)
    ue_ref[0, HALO:HALO + TILE, :] = u_ref[...]
    ue_ref[0, HALO + TILE:, :] = jnp.where(has_next, un_ref[...], 0.0)
    ze_ref[0:HALO, :] = jnp.where(has_prev, zp_ref[...], 0.0)
    ze_ref[HALO:HALO + TILE, :] = z_ref[...]
    ze_ref[HALO + TILE:, :] = jnp.where(has_next, zn_ref[...], 0.0)

    pad = CONV_WIDTH // 2
    reach = (HALO - pad + CONV_WIDTH - 1) // SUBLANES * SUBLANES
    groups = CONV_ROWS // SUBLANES
    n_chunks = TILE // CONV_ROWS

    def shift_rows(lo, hi):
        for b in range(1, SUBLANES):
            ue_ref[b, lo:hi, :] = ue_ref[0, lo + b:hi + b, :]

    def conv_chunk(r):
        r0 = r * CONV_ROWS
        acc = jnp.zeros((groups, SUBLANES, CONV_CH), F32) + bdw_ref[...]
        for k in range(CONV_WIDTH):
            a, b = divmod(HALO + k - pad, SUBLANES)
            e0 = r0 + a * SUBLANES
            acc = acc + ue_ref[b, e0:e0 + CONV_ROWS, :].reshape(groups, SUBLANES, CONV_CH) * wdw_ref[k]
        acc = acc.reshape(CONV_ROWS, CONV_CH)
        mu = jnp.mean(acc, axis=-1, keepdims=True)
        d = acc - mu
        var = jnp.mean(d * d, axis=-1, keepdims=True)
        yn = d * lax.rsqrt(var + EPS) * lng_ref[...] + lnb_ref[...]
        o_ref[r0:r0 + CONV_ROWS, 0:CONV_CH] = (yn * jax.nn.sigmoid(yn)).astype(BF16)

    t = t0 + lax.broadcasted_iota(jnp.int32, (TILE, POOL_GROUP_W), 0)
    ze = ze_ref[...]
    z_hi = ze.astype(BF16)
    z_lo = (ze - z_hi.astype(F32)).astype(BF16)

    def pool_group(gi):
        w = POOL_SIZES[gi]
        lo = w // 2
        hi = w - lo - 1
        cols = slice(gi * POOL_GROUP_W, (gi + 1) * POOL_GROUP_W)
        tot = _dot(band_ref[gi], z_hi[:, cols]) + _dot(band_ref[gi], z_lo[:, cols])
        cnt = jnp.minimum(t + hi + 1, seq_len) - jnp.maximum(t - lo, 0)
        d = tot / cnt.astype(F32) - ze_ref[HALO:HALO + TILE, cols]
        y = _dot(d.astype(BF16), wg_ref[gi]) * ps_ref[:, cols]
        o_ref[:, CONV_CH + gi * POOL_GROUP_W:CONV_CH + (gi + 1) * POOL_GROUP_W] = y.astype(BF16)

    shift_rows(0, reach + CONV_ROWS)
    for r in range(n_chunks):
        if r + 1 < n_chunks:
            shift_rows(reach + (r + 1) * CONV_ROWS, reach + (r + 2) * CONV_ROWS)
        conv_chunk(r)
        if r % (n_chunks // len(POOL_SIZES)) == 0:
            pool_group(r // (n_chunks // len(POOL_SIZES)))


def _conv_mix(u, z, cp, jj):
    tile_map = lambda i: (i, 0)
    hb = TILE // HALO
    prev_map = lambda i: (jnp.maximum(i * hb - 1, 0), 0)
    next_map = lambda i: (jnp.minimum((i + 1) * hb, T_ALL // HALO - 1), 0)
    return pl.pallas_call(
        _conv_mix_kernel,
        grid=(N_TILES,),
        in_specs=[
            pl.BlockSpec((TILE, CONV_CH), tile_map),
            pl.BlockSpec((HALO, CONV_CH), prev_map),
            pl.BlockSpec((HALO, CONV_CH), next_map),
            pl.BlockSpec((TILE, POOL_CH), tile_map),
            pl.BlockSpec((HALO, POOL_CH), prev_map),
            pl.BlockSpec((HALO, POOL_CH), next_map),
            _layer_spec((CONV_WIDTH, SUBLANES, CONV_CH), jj),
            _layer_spec((1, CONV_CH), jj),
            _layer_spec((1, CONV_CH), jj),
            _layer_spec((1, CONV_CH), jj),
            pl.BlockSpec((len(POOL_SIZES), TILE, TILE + 2 * HALO), lambda i: (0, 0, 0)),
            _layer_spec((len(POOL_SIZES), POOL_GROUP_W, POOL_GROUP_W), jj),
            _layer_spec((1, POOL_CH), jj),
        ],
        out_specs=pl.BlockSpec((TILE, D_MODEL), tile_map),
        out_shape=jax.ShapeDtypeStruct((T_ALL, D_MODEL), BF16),
        scratch_shapes=[pltpu.VMEM((SUBLANES, TILE + 2 * HALO, CONV_CH), F32),
                        pltpu.VMEM((TILE + 2 * HALO, POOL_CH), F32)],
        compiler_params=_cparams(("arbitrary",)),
        name="conv_mix",
    )(u, u, u, z, z, z, cp["w_dw"], cp["b_dw"], cp["ln_g"], cp["ln_b"], _pool_bands(), cp["w_grp"], cp["p_scale"])


def _pool_bands():
    t = np.arange(TILE)[:, None]
    e = np.arange(TILE + 2 * HALO)[None, :] - HALO
    bands = []
    for w in POOL_SIZES:
        lo = w // 2
        hi = w - lo - 1
        bands.append((e >= t - lo) & (e <= t + hi))
    return jnp.asarray(np.stack(bands), BF16)


def _post_kernel(*refs, n_o, n_x, final):
    o_refs, x_refs = refs[:n_o], refs[n_o:n_o + n_x]
    mod_ref, g_ref, wo_ref, w1_ref, w2_ref, fg_ref = refs[n_o + n_x:n_o + n_x + 6]
    y_refs = refs[n_o + n_x + 6:]
    subs = _sub_rows()
    x1s, hs, acts, outs = [], [], [], []
    for rows in subs:
        x1 = _tok_load(x_refs, rows) + mod_ref[2:3, :] * _dot(_tok_load(o_refs, rows), wo_ref[...])
        x1s.append(x1)
        hs.append(_rms_mod(x1, g_ref[...], mod_ref[3:4, :], mod_ref[4:5, :]).astype(BF16))
    for h in hs:
        a = jnp.maximum(_dot(h, w1_ref[...]), 0.0)
        acts.append((a * a).astype(BF16))
    for rows, x1, a in zip(subs, x1s, acts):
        x2 = x1 + mod_ref[5:6, :] * _dot(a, w2_ref[...])
        if final:
            ms = jnp.mean(x2 * x2, axis=-1, keepdims=True)
            x2 = x2 * lax.rsqrt(ms + EPS) * fg_ref[...]
        outs.append((rows, x2))
    _tok_store(y_refs, outs)


def _post(o, x, mod, norm_g, w_out, w1, w2, final_g, layer, final):
    const2 = lambda i: (0, 0)
    o_ops, o_specs = _tok_specs(o, D_MODEL)
    x_ops, x_specs = _tok_specs(x, D_MODEL)
    if final:
        y = (jax.ShapeDtypeStruct((T_CTX, D_MODEL), F32), jax.ShapeDtypeStruct((T_LAT, D_MODEL), F32))
    else:
        y = jax.ShapeDtypeStruct((T_ALL, D_MODEL), F32)
    _, y_specs = _tok_specs(y, D_MODEL)
    out = pl.pallas_call(
        functools.partial(_post_kernel, n_o=len(o_ops), n_x=len(x_ops), final=final),
        grid=(N_TOK,),
        in_specs=o_specs + x_specs + [
            pl.BlockSpec((None, None, 6, D_MODEL), lambda i: (layer, _mod_row(i), 0, 0)),
            _layer_spec((1, D_MODEL), 2 * layer + 1),
            pl.BlockSpec((None, D_MODEL, D_MODEL), lambda i: (layer // 2, 0, 0), pipeline_mode=pl.Buffered(1)),
            pl.BlockSpec((None, D_MODEL, D_FF), lambda i: (layer, 0, 0), pipeline_mode=pl.Buffered(1)),
            pl.BlockSpec((None, D_FF, D_MODEL), lambda i: (layer, 0, 0), pipeline_mode=pl.Buffered(1)),
            pl.BlockSpec((1, D_MODEL), const2),
        ],
        out_specs=y_specs if final else y_specs[0],
        out_shape=y,
        compiler_params=_cparams(("arbitrary",)),
        name="post_final" if final else "post",
    )(*o_ops, *x_ops, mod, norm_g, w_out, w1, w2, final_g)
    return out


def _pair_lanes(w):
    lead = w.shape[:-1]
    w = w.reshape(lead + (w.shape[-1] // LANES, 2, 2, HEAD_DIM // 2))
    return jnp.swapaxes(w, -3, -2).reshape(lead + (-1,))


def _mla_lanes(nope, rope, xp=jnp):
    shape = (nope if nope is not None else rope).shape[:-1]
    nope = xp.zeros(shape + (QK_NOPE,), xp.float32) if nope is None else nope
    rope = xp.zeros(shape + (QK_ROPE,), xp.float32) if rope is None else rope
    h = QK_ROPE // 2
    return xp.concatenate([rope[..., :h], nope[..., :64 - h], rope[..., h:], nope[..., 64 - h:],
                           xp.zeros(shape + (LANES - QK_NOPE - QK_ROPE,), xp.float32)], axis=-1)


def _rope_tables():
    f32 = np.float32
    n = DEC_SEQ
    rows = n // GRID_W
    row = np.repeat(np.arange(rows), GRID_W).astype(f32)
    col = np.tile(np.arange(GRID_W), rows).astype(f32)

    def angles(dim):
        quarter = dim // 4
        inv_freq = (f32(ROPE_BASE) ** (-np.arange(quarter, dtype=f32) / f32(quarter))).astype(f32)
        return np.concatenate([row[:, None] * inv_freq, col[:, None] * inv_freq], axis=-1).astype(f32)

    ang_a = angles(HEAD_DIM)
    cos_a, sin_a = np.cos(ang_a), np.sin(ang_a)
    c_a = np.concatenate([cos_a] * 4, axis=-1)
    s_a = np.concatenate([-sin_a, -sin_a, sin_a, sin_a], axis=-1)
    ang_b = angles(QK_ROPE)
    cos_b, sin_b = np.cos(ang_b), np.sin(ang_b)
    c_b = _mla_lanes(np.ones((n, QK_NOPE), f32), np.concatenate([cos_b, cos_b], axis=-1), np)
    s_b = _mla_lanes(None, np.concatenate([-sin_b, sin_b], axis=-1), np)
    lat = np.stack([c_a, s_a, c_b, s_b])
    ident = np.stack([np.ones((TOK, LANES), f32), np.zeros((TOK, LANES), f32)] * 2)
    return jnp.asarray(np.concatenate([ident, lat], axis=1).astype(f32))


def _prep_even(attn_w_in, mla_q_norm, mla_kv_norm, mla_w_qb, mla_w_kvb, attn_w_out):
    w = attn_w_in
    o = np.cumsum((0, 512, 128, 128, Q_LORA, KV_LORA, QK_ROPE))
    zeros = lambda n: jnp.zeros((N_EVEN, D_MODEL, n), F32)
    w_in = jnp.concatenate([
        _pair_lanes(w[..., o[0]:o[1]]), _pair_lanes(w[..., o[1]:o[2]]), w[..., o[2]:o[3]],
        w[..., o[3]:o[4]], zeros(Q_LORA_PAD - Q_LORA),
        w[..., o[4]:o[5]],
        _mla_lanes(None, w[..., o[5]:o[6]]),
    ], axis=-1).astype(BF16)
    q_norm = jnp.pad(mla_q_norm, ((0, 0), (0, Q_LORA_PAD - Q_LORA))).reshape(N_EVEN, 1, Q_LORA_PAD)
    wqb = mla_w_qb.reshape(N_EVEN, Q_LORA, B_HEADS, QK_NOPE + QK_ROPE)
    wqb = jnp.pad(_mla_lanes(wqb[..., :QK_NOPE], wqb[..., QK_NOPE:]),
                  ((0, 0), (0, Q_LORA_PAD - Q_LORA), (0, 0), (0, 0)))
    wkvb = mla_w_kvb.reshape(N_EVEN, KV_LORA, B_HEADS, QK_NOPE + V_DIM)
    w_kk = _mla_lanes(wkvb[..., :QK_NOPE], None)
    w_kv = wkvb[..., QK_NOPE:].reshape(N_EVEN, KV_LORA, B_HEADS * V_DIM)
    return {
        "w_in": w_in,
        "q_norm": q_norm,
        "kv_norm": mla_kv_norm.reshape(N_EVEN, 1, KV_LORA),
        "w_qb": wqb.reshape(N_EVEN, Q_LORA_PAD, B_HEADS * LANES).astype(BF16),
        "w_kk": w_kk.reshape(N_EVEN, KV_LORA, B_HEADS * LANES).astype(BF16),
        "w_kv": jnp.swapaxes(w_kv, 1, 2).astype(BF16),
        "w_out": attn_w_out.astype(BF16),
    }


def kernel(x_prompt, x_sample, cache_win_k, cache_win_v, cache_mla_ckv, cache_mla_krope, c, c_ctx, w_mod, b_mod,
           norm_g, attn_w_in, attn_sink, mla_q_norm, mla_kv_norm, mla_w_qb, mla_w_kvb, attn_w_out, conv_w_in,
           conv_dw, conv_dw_b, conv_ln_g, conv_ln_b, pool_w, pool_scale, conv_w_out, mlp_w1, mlp_w2, final_g):
    x = (x_prompt.reshape(T_CTX, D_MODEL), x_sample.reshape(T_LAT, D_MODEL))

    cond = jnp.concatenate([c_ctx[None, :], c, jnp.zeros((N_COND - 1 - DEC_BATCH, D_MODEL), F32)], axis=0)
    mod = _mod_table(cond, w_mod, b_mod).reshape(DEPTH, N_COND, 6, D_MODEL)

    rope = _rope_tables()
    even = _prep_even(attn_w_in, mla_q_norm, mla_kv_norm, mla_w_qb, mla_w_kvb, attn_w_out)
    kdc, vdc, kmc, vmc = _ctx_kv(
        _pair_lanes(cache_win_k.reshape(DEC_BATCH, N_EVEN, PAST_LEN, LANES)),
        cache_win_v.reshape(DEC_BATCH, N_EVEN, PAST_LEN, LANES),
        cache_mla_ckv, _mla_lanes(None, cache_mla_krope), even["w_kk"], even["w_kv"])
    odd = {
        "w_dw": jnp.broadcast_to(conv_dw[:, :, None, :], (N_ODD, CONV_WIDTH, SUBLANES, CONV_CH)),
        "b_dw": conv_dw_b.reshape(N_ODD, 1, CONV_CH),
        "ln_g": conv_ln_g.reshape(N_ODD, 1, CONV_CH),
        "ln_b": conv_ln_b.reshape(N_ODD, 1, CONV_CH),
        "w_grp": pool_w.astype(BF16),
        "p_scale": pool_scale.reshape(N_ODD, 1, POOL_CH),
    }
    conv_w_in_b, conv_w_out_b = conv_w_in.astype(BF16), conv_w_out.astype(BF16)
    w1_all, w2_all = mlp_w1.astype(BF16), mlp_w2.astype(BF16)
    gains = norm_g.reshape(2 * DEPTH, 1, D_MODEL)
    final_g2 = final_g.reshape(1, D_MODEL)
    sink2 = attn_sink * LOG2E

    caches = []
    for l in range(DEPTH):
        if l % 2 == 0:
            i = l // 2
            qa, kd, vd, qm, km, vm, ka32, va32, ckv32, kr32 = _attn_in(x, mod, gains, even, rope, l)
            caches.append((ka32, va32, ckv32, kr32))
            o = _attention(sink2[i], qa, kd, vd, qm, km, vm, kdc, vdc, kmc, vmc, i)
            w_out = even["w_out"]
        else:
            u, z = _conv_in(x, mod, gains, conv_w_in_b, l)
            o = _conv_mix(u, z, odd, l // 2)
            w_out = conv_w_out_b
        x = _post(o, x, mod, gains, w_out, w1_all, w2_all, final_g2, l, final=(l == DEPTH - 1))

    ka, va, ckv, kr = (jnp.stack(t, axis=1) for t in zip(*caches))
    new_k = jnp.swapaxes(ka.reshape(BATCH, SEQ, N_EVEN, 2, A_KV_HEADS, HEAD_DIM // 2), 3, 4)
    new_k = jnp.swapaxes(new_k.reshape(BATCH, SEQ, N_EVEN, A_KV_HEADS, HEAD_DIM), 1, 2)
    new_v = jnp.swapaxes(va.reshape(BATCH, SEQ, N_EVEN, A_KV_HEADS, HEAD_DIM), 1, 2)
    new_ckv = jnp.swapaxes(ckv.reshape(BATCH, SEQ, N_EVEN, KV_LORA), 1, 2)
    new_kr = jnp.concatenate([kr[..., :QK_ROPE // 2], kr[..., 64:64 + QK_ROPE // 2]], axis=-1)
    new_kr = jnp.swapaxes(new_kr.reshape(BATCH, SEQ, N_EVEN, QK_ROPE), 1, 2)
    y_prompt = x[0].reshape(BATCH, SEQ, D_MODEL)
    y_sample = x[1].reshape(DEC_BATCH, DEC_SEQ, D_MODEL)
    return (y_prompt, y_sample, new_k, new_v, new_ckv, new_kr)
```

```python
import functools
import math

import numpy as np
import jax
import jax.numpy as jnp
from jax import lax
from jax.experimental import pallas as pl
from jax.experimental.pallas import tpu as pltpu

F32 = jnp.float32
BF16 = jnp.bfloat16

D_MODEL = 1024
BATCH = 16
SEQ = 256
DEPTH = 4
DEC_BATCH = 8
DEC_SEQ = 2048
PAST_LEN = 256
GRID_W = 64
N_EVEN = (DEPTH + 1) // 2
N_ODD = DEPTH // 2
A_HEADS = 8
A_KV_HEADS = 2
A_GROUP = A_HEADS // A_KV_HEADS
HEAD_DIM = 64
WINDOW = 128
B_HEADS = 8
Q_LORA = 192
KV_LORA = 128
QK_NOPE = 64
QK_ROPE = 32
V_DIM = 64
MLA_SCALE = (QK_NOPE + QK_ROPE) ** -0.5
CONV_CH = D_MODEL // 2
CONV_WIDTH = 31
POOL_CH = D_MODEL // 2
POOL_SIZES = (2, 4, 8, 16)
POOL_GROUP_W = POOL_CH // len(POOL_SIZES)
D_FF = 4 * D_MODEL
ROPE_BASE = 10000.0
EPS = 1e-6
NEG_INF = -1e30
LOG2E = math.log2(math.e)

LANES = 128
SUBLANES = 8
VMEM_LIMIT_BYTES = 56 * 1024 * 1024

TILE = 256
T_CTX = BATCH * SEQ
T_LAT = DEC_BATCH * DEC_SEQ
T_ALL = T_CTX + T_LAT
N_CTX_TILES = T_CTX // TILE
N_TILES = T_ALL // TILE
LAT_TILES_PER_SEQ = DEC_SEQ // TILE
TOK = 512
N_CTX_TOK = T_CTX // TOK
N_TOK = T_ALL // TOK
LAT_TOK_PER_SEQ = DEC_SEQ // TOK
TOK_IN = 1024
N_COND = 16
BQ = 256
VT_ROWS = 80
HALO = 16
Q_LORA_PAD = 256
ATTN_IN_COLS = 512 + 128 + 128 + Q_LORA_PAD + 128 + 128


def _cparams(sem):
    return pltpu.CompilerParams(dimension_semantics=sem, vmem_limit_bytes=VMEM_LIMIT_BYTES)


def _mod_row(i, tok=TOK):
    return jnp.where(i < T_CTX // tok, 0, 1 + (i - T_CTX // tok) // (DEC_SEQ // tok))


def _pos_block(i, tok):
    return jnp.where(i < T_CTX // tok, 0, 1 + (i - T_CTX // tok) % (DEC_SEQ // tok))


def _tok_specs(a, width, tok=TOK):
    if isinstance(a, tuple):
        return list(a), [pl.BlockSpec((tok, width), lambda i: (jnp.minimum(i, T_CTX // tok - 1), 0)),
                         pl.BlockSpec((tok, width), lambda i: (jnp.maximum(i - T_CTX // tok, 0), 0))]
    return [a], [pl.BlockSpec((tok, width), lambda i: (i, 0))]


def _tok_load(refs, rows, tok=TOK):
    if len(refs) == 1:
        return refs[0][rows, :]
    return jnp.where(pl.program_id(0) < T_CTX // tok, refs[0][rows, :], refs[1][rows, :])


def _tok_store(refs, parts):
    def put(ref):
        for rows, val in parts:
            ref[rows, :] = val

    if len(refs) == 1:
        put(refs[0])
        return
    i = pl.program_id(0)

    @pl.when(i < N_CTX_TOK)
    def _():
        put(refs[0])

    @pl.when(i >= N_CTX_TOK)
    def _():
        put(refs[1])


def _layer_spec(shape, idx):
    zeros = (0,) * len(shape)
    return pl.BlockSpec((None,) + tuple(shape), lambda *_: (idx,) + zeros)


def _sub_rows(tok=TOK):
    return [slice(s * TILE, (s + 1) * TILE) for s in range(tok // TILE)]


def _dot(a, b):
    return jnp.dot(a, b, preferred_element_type=F32)


def _dot_nt(a, b):
    return lax.dot_general(a, b, (((1,), (1,)), ((), ())), preferred_element_type=F32)


def _rms_mod(x, g, shift, scale):
    ms = jnp.mean(x * x, axis=-1, keepdims=True)
    return x * lax.rsqrt(ms + EPS) * (g * (1.0 + scale)) + shift


def _rope(x, c, s):
    return x * c + pltpu.roll(x, 64, 1) * s


def _lane_lt64(shape):
    return lax.broadcasted_iota(jnp.int32, shape, len(shape) - 1) < 64


def _lane_even32(shape):
    return (lax.broadcasted_iota(jnp.int32, shape, len(shape) - 1) & 63) < 32


def _store_kv_dup(kd_ref, vd_ref, rows, k, v):
    even32 = _lane_even32(k.shape)
    lt64 = _lane_lt64(v.shape)
    kd_ref[rows, 0:LANES] = jnp.where(even32, k, pltpu.roll(k, 32, 1)).astype(BF16)
    kd_ref[rows, LANES:2 * LANES] = jnp.where(even32, pltpu.roll(k, LANES - 32, 1), k).astype(BF16)
    vd_ref[rows, 0:LANES] = jnp.where(lt64, v, 1.0).astype(BF16)
    vd_ref[rows, LANES:2 * LANES] = jnp.where(lt64, pltpu.roll(v, 64, 1), 1.0).astype(BF16)


def _store_vt(vt_ref, cols, ckvn, wkvt_ref):
    vt = _dot(wkvt_ref[...], ckvn.T.astype(BF16))
    ones = jnp.ones((VT_ROWS - V_DIM, ckvn.shape[0]), BF16)
    for hh in range(B_HEADS):
        vt_ref[hh * VT_ROWS:hh * VT_ROWS + V_DIM, cols] = vt[hh * V_DIM:(hh + 1) * V_DIM].astype(BF16)
        vt_ref[hh * VT_ROWS + V_DIM:(hh + 1) * VT_ROWS, cols] = ones


def _mod_kernel(cond_ref, w_ref, b_ref, o_ref):
    c = cond_ref[...]
    s = c * jax.nn.sigmoid(c)
    o_ref[...] = _dot(s.astype(BF16), w_ref[...].astype(BF16)) + b_ref[...]


def _mod_table(cond, w_mod, b_mod):
    nb = 6 * D_MODEL // 1024
    return pl.pallas_call(
        _mod_kernel,
        grid=(DEPTH, nb),
        in_specs=[
            pl.BlockSpec((N_COND, D_MODEL), lambda l, n: (0, 0)),
            pl.BlockSpec((None, D_MODEL, 1024), lambda l, n: (l, 0, n)),
            pl.BlockSpec((None, 1, 1024), lambda l, n: (l, 0, n)),
        ],
        out_specs=pl.BlockSpec((None, N_COND, 1024), lambda l, n: (l, 0, n)),
        out_shape=jax.ShapeDtypeStruct((DEPTH, N_COND, 6 * D_MODEL), F32),
        compiler_params=_cparams(("arbitrary", "arbitrary")),
        name="mod_table",
    )(cond, w_mod, b_mod.reshape(DEPTH, 1, 6 * D_MODEL))


def _attn_in_kernel(*refs, n_x):
    x_refs = refs[:n_x]
    (mod_ref, g_ref, w_ref, qn_ref, kvn_ref, wqb_ref, wkk_ref, wkv_ref, rope_ref,
     qa_ref, kd_ref, vd_ref, qm_ref, km_ref, vm_ref, ka32_ref, va32_ref, ckv32_ref, kr32_ref) = refs[n_x:]
    col = lambda n: slice(n * LANES, (n + 1) * LANES)

    def project(rows):
        h = _rms_mod(_tok_load(x_refs, rows, TOK_IN), g_ref[...], mod_ref[0:1, :], mod_ref[1:2, :])
        return _dot(h.astype(BF16), w_ref[...])

    def derive(rows, y):
        ca, sa = rope_ref[0, rows, :], rope_ref[1, rows, :]
        cb, sb = rope_ref[2, rows, :], rope_ref[3, rows, :]

        for c in range(4):
            q = _rope(y[:, col(c)], ca, sa)
            qa_ref[rows, col(c)] = (q * (HEAD_DIM ** -0.5 * LOG2E)).astype(BF16)

        ka = _rope(y[:, 512:640], ca, sa)
        va = y[:, 640:768]
        _store_kv_dup(kd_ref, vd_ref, rows, ka, va)

        cq = y[:, 768:768 + Q_LORA_PAD]
        cqn = cq * lax.rsqrt(jnp.sum(cq * cq, axis=-1, keepdims=True) * (1.0 / Q_LORA) + EPS) * qn_ref[...]
        qm = _dot(cqn.astype(BF16), wqb_ref[...])
        for hh in range(B_HEADS):
            q = _rope(qm[:, col(hh)], cb, sb)
            qm_ref[rows, col(hh)] = (q * (MLA_SCALE * LOG2E)).astype(BF16)

        ckv = y[:, 1024:1152]
        ckvn = ckv * lax.rsqrt(jnp.mean(ckv * ckv, axis=-1, keepdims=True) + EPS) * kvn_ref[...]
        kr = y[:, 1152:1280]
        ckvn_b = ckvn.astype(BF16)
        kn = _dot(ckvn_b, wkk_ref[...])
        krr = _rope(kr, cb, sb)
        for hh in range(B_HEADS):
            km_ref[rows, col(hh)] = (kn[:, col(hh)] + krr).astype(BF16)
        _store_vt(vm_ref, rows, ckvn, wkv_ref)
        return ka, va, ckvn, kr

    subs = _sub_rows(TOK_IN)
    cache = []
    y = project(subs[0])
    for s, rows in enumerate(subs):
        y_next = project(subs[s + 1]) if s + 1 < len(subs) else None
        cache.append(derive(rows, y))
        y = y_next

    @pl.when(pl.program_id(0) < T_CTX // TOK_IN)
    def _():
        for rows, (ka, va, ckvn, kr) in zip(subs, cache):
            ka32_ref[rows, :] = ka
            va32_ref[rows, :] = va
            ckv32_ref[rows, :] = ckvn
            kr32_ref[rows, :] = kr


def _attn_in(x, mod, norm_g, wp, rope, layer):
    tile_map = lambda i: (i, 0)
    li = layer // 2
    ctx_map = lambda i: (jnp.minimum(i, T_CTX // TOK_IN - 1), 0)
    bf = lambda w: jax.ShapeDtypeStruct((T_ALL, w), BF16)
    c32 = jax.ShapeDtypeStruct((T_CTX, LANES), F32)
    x_ops, x_specs = _tok_specs(x, D_MODEL, TOK_IN)
    return pl.pallas_call(
        functools.partial(_attn_in_kernel, n_x=len(x_ops)),
        grid=(T_ALL // TOK_IN,),
        in_specs=x_specs + [
            pl.BlockSpec((None, None, 6, D_MODEL), lambda i: (layer, _mod_row(i, TOK_IN), 0, 0)),
            _layer_spec((1, D_MODEL), 2 * layer),
            _layer_spec((D_MODEL, ATTN_IN_COLS), li),
            _layer_spec((1, Q_LORA_PAD), li),
            _layer_spec((1, KV_LORA), li),
            _layer_spec((Q_LORA_PAD, B_HEADS * LANES), li),
            _layer_spec((KV_LORA, B_HEADS * LANES), li),
            _layer_spec((B_HEADS * V_DIM, KV_LORA), li),
            pl.BlockSpec((4, TOK_IN, LANES), lambda i: (0, _pos_block(i, TOK_IN), 0)),
        ],
        out_specs=[
            pl.BlockSpec((TOK_IN, 512), tile_map),
            pl.BlockSpec((TOK_IN, 256), tile_map),
            pl.BlockSpec((TOK_IN, 256), tile_map),
            pl.BlockSpec((TOK_IN, 1024), tile_map),
            pl.BlockSpec((TOK_IN, 1024), tile_map),
            pl.BlockSpec((B_HEADS * VT_ROWS, TOK_IN), lambda i: (0, i)),
            pl.BlockSpec((TOK_IN, LANES), ctx_map),
            pl.BlockSpec((TOK_IN, LANES), ctx_map),
            pl.BlockSpec((TOK_IN, LANES), ctx_map),
            pl.BlockSpec((TOK_IN, LANES), ctx_map),
        ],
        out_shape=[bf(512), bf(256), bf(256), bf(1024), bf(1024),
                   jax.ShapeDtypeStruct((B_HEADS * VT_ROWS, T_ALL), BF16), c32, c32, c32, c32],
        compiler_params=_cparams(("arbitrary",)),
        name="attn_in",
    )(*x_ops, mod, norm_g, wp["w_in"], wp["q_norm"], wp["kv_norm"], wp["w_qb"], wp["w_kk"], wp["w_kv"], rope)


def _ctx_kv_kernel(ck_ref, cv_ref, cckv_ref, ckr_ref, wkk_ref, wkv_ref, kd_ref, vd_ref, km_ref, vm_ref):
    _store_kv_dup(kd_ref, vd_ref, slice(None), ck_ref[...], cv_ref[...])
    cb = cckv_ref[...].astype(BF16)
    kn = _dot(cb, wkk_ref[...])
    kr = ckr_ref[...]
    for hh in range(B_HEADS):
        km_ref[:, hh * LANES:(hh + 1) * LANES] = (kn[:, hh * LANES:(hh + 1) * LANES] + kr).astype(BF16)
    _store_vt(vm_ref, slice(None), cckv_ref[...], wkv_ref)


def _ctx_kv(cache_k, cache_v, cache_ckv, cache_kr128, w_kk, w_kv):
    cache_map = lambda i, b: (b, i, 0, 0)
    w_map = lambda i, b: (i, 0, 0)
    out_map = lambda i, b: (i, b, 0, 0)
    o = lambda w: jax.ShapeDtypeStruct((N_EVEN, DEC_BATCH, PAST_LEN, w), BF16)
    return pl.pallas_call(
        _ctx_kv_kernel,
        grid=(N_EVEN, DEC_BATCH),
        in_specs=[
            pl.BlockSpec((None, None, PAST_LEN, LANES), cache_map),
            pl.BlockSpec((None, None, PAST_LEN, LANES), cache_map),
            pl.BlockSpec((None, None, PAST_LEN, KV_LORA), cache_map),
            pl.BlockSpec((None, None, PAST_LEN, LANES), cache_map),
            pl.BlockSpec((None, KV_LORA, B_HEADS * LANES), w_map),
            pl.BlockSpec((None, B_HEADS * V_DIM, KV_LORA), w_map),
        ],
        out_specs=[
            pl.BlockSpec((None, None, PAST_LEN, 256), out_map),
            pl.BlockSpec((None, None, PAST_LEN, 256), out_map),
            pl.BlockSpec((None, None, PAST_LEN, 1024), out_map),
            pl.BlockSpec((None, None, B_HEADS * VT_ROWS, PAST_LEN), out_map),
        ],
        out_shape=[o(256), o(256), o(1024),
                   jax.ShapeDtypeStruct((N_EVEN, DEC_BATCH, B_HEADS * VT_ROWS, PAST_LEN), BF16)],
        compiler_params=_cparams(("arbitrary", "arbitrary")),
        name="ctx_kv",
    )(cache_k, cache_v, cache_ckv, cache_kr128, w_kk, w_kv)


def _attn_heads(qa_ref, qm_ref, a_segs, m_segs, sink_ref, o_ref, rows, depth):
    lt64 = _lane_lt64((rows, LANES))
    lane = lax.broadcasted_iota(jnp.int32, (1, LANES), 1)
    keep = (((lane & 63) < 32).astype(BF16), ((lane & 63) >= 32).astype(BF16))
    col = lambda n: slice(n * LANES, (n + 1) * LANES)
    swap = lambda r: pltpu.roll(r, 64, 1)

    jobs = []
    for kh in range(A_KV_HEADS):
        heads = [dict(out=(kh * A_GROUP + g) // 2, half=g % 2, sink=kh * A_GROUP + g) for g in range(A_GROUP)]
        jobs.append(dict(
            heads=heads,
            q=lambda heads=heads: jnp.concatenate(
                [qa_ref[:, col(h["out"])] * keep[h["half"]] for h in heads], axis=0),
            ks=[lambda kd=kd, rs=rs, kh=kh: kd[rs, col(kh)] for kd, _, rs, _ in a_segs],
            vs=[lambda vd=vd, rs=rs, kh=kh: vd[rs, col(kh)] for _, vd, rs, _ in a_segs],
            valids=[valid for _, _, _, valid in a_segs]))
    for hh in range(B_HEADS):
        jobs.append(dict(
            keys_on_rows=True,
            heads=[dict(out=4 + hh // 2, half=hh % 2)],
            q=lambda hh=hh: qm_ref[:, col(hh)],
            ks=[lambda km=km, hh=hh: km[:, col(hh)] for km, _ in m_segs],
            vs=[lambda vmt=vmt, hh=hh: vmt[hh * VT_ROWS:(hh + 1) * VT_ROWS, :] for _, vmt in m_segs]))

    def scores(job):
        if job.get("keys_on_rows"):
            h = job["heads"][0]
            q = job["q"]()
            h["ss"] = [_dot_nt(k(), q) for k in job["ks"]]
            m = h["ss"][0].max(axis=0, keepdims=True)
            for s in h["ss"][1:]:
                m = jnp.maximum(m, s.max(axis=0, keepdims=True))
            h["m"] = m
            return
        full = [_dot_nt(job["q"](), k()) for k in job["ks"]]
        for b, h in enumerate(job["heads"]):
            blk = slice(b * rows, (b + 1) * rows)
            ss = [s[blk] if valid is None else jnp.where(valid, s[blk], NEG_INF)
                  for s, valid in zip(full, job["valids"])]
            m = ss[0].max(axis=-1, keepdims=True)
            for s in ss[1:]:
                m = jnp.maximum(m, s.max(axis=-1, keepdims=True))
            sink = sink_ref[h["sink"]]
            m = jnp.maximum(m, sink)
            h["e"] = jnp.exp2(sink - m)
            h["ss"], h["m"] = ss, m

    def values(job):
        if job.get("keys_on_rows"):
            h = job["heads"][0]
            r = None
            for s, vt in zip(h.pop("ss"), job["vs"]):
                rs_ = _dot(vt(), jnp.exp2(s - h["m"]).astype(BF16))
                r = rs_ if r is None else r + rs_
            del h["m"]
            h["r"] = r
            return
        r = None
        for i, v in enumerate(job["vs"]):
            p = [jnp.exp2(h["ss"][i] - h["m"]).astype(BF16) for h in job["heads"]]
            rs_ = _dot(jnp.concatenate(p, axis=0), v())
            r = rs_ if r is None else r + rs_
        for b, h in enumerate(job["heads"]):
            del h["ss"], h["m"]
            h["r"] = r[b * rows:(b + 1) * rows]

    def finish(job, done):
        if job.get("keys_on_rows"):
            h = job["heads"][0]
            r = h.pop("r")
            o_t = r[0:V_DIM] / r[V_DIM:V_DIM + 1]
            other = done.pop(h["out"], None)
            if other is None:
                done[h["out"]] = o_t
            else:
                lo, hi = (other, o_t) if h["half"] == 1 else (o_t, other)
                o_ref[:, col(h["out"])] = jnp.concatenate([lo, hi], axis=0).T.astype(BF16)
            return
        for h in job["heads"]:
            r = h.pop("r")
            if h["half"] == 1:
                o = swap(r) / (r + h["e"])
            else:
                o = r / (swap(r) + h["e"])
            other = done.pop(h["out"], None)
            if other is None:
                done[h["out"]] = o
            else:
                lo, hi = (other, o) if h["half"] == 1 else (o, other)
                o_ref[:, col(h["out"])] = jnp.where(lt64, lo, hi).astype(BF16)

    done = {}
    for t in range(len(jobs) + depth):
        if t < len(jobs):
            scores(jobs[t])
        if t >= depth:
            values(jobs[t - depth])
            finish(jobs[t - depth], done)


def _attn_lat_kernel(sink_ref, qa_ref, qm_ref, kd_ref, vd_ref, km_ref, vm_ref,
                     kdc_ref, vdc_ref, kmc_ref, vmc_ref, o_ref):
    j = pl.program_id(1)
    q0 = j * BQ
    nloc = BQ + 2 * WINDOW
    start = pl.multiple_of(jnp.clip(q0 - WINDOW, 0, DEC_SEQ - nloc), WINDOW)
    qpos = q0 + lax.broadcasted_iota(jnp.int32, (BQ, nloc), 0)
    kpos = start + lax.broadcasted_iota(jnp.int32, (BQ, nloc), 1)
    valid = jnp.abs(qpos - kpos) <= WINDOW
    everything = slice(None)
    a_segs = [(kd_ref, vd_ref, pl.ds(start, nloc), valid), (kdc_ref, vdc_ref, everything, None)]
    m_segs = [(km_ref, vm_ref), (kmc_ref, vmc_ref)]
    _attn_heads(qa_ref, qm_ref, a_segs, m_segs, sink_ref, o_ref, BQ, depth=2)


def _attn_ctx_kernel(sink_ref, qa_ref, qm_ref, kd_ref, vd_ref, km_ref, vm_ref, o_ref):
    a_segs = [(kd_ref, vd_ref, slice(None), None)]
    m_segs = [(km_ref, vm_ref)]
    _attn_heads(qa_ref, qm_ref, a_segs, m_segs, sink_ref, o_ref, SEQ, depth=1)


def _attention(sink2, qa, kd, vd, qm, km, vm, kdc, vdc, kmc, vmc, layer_i):
    smem = pl.BlockSpec(memory_space=pltpu.SMEM)
    nq = DEC_SEQ // BQ
    q_off = T_CTX // BQ
    s_off = T_CTX // DEC_SEQ
    q_map = lambda b, j: (q_off + b * nq + j, 0)
    kv_map = lambda b, j: (s_off + b, 0)
    c_map = lambda b, j: (layer_i, b, 0, 0)
    o_lat = pl.pallas_call(
        _attn_lat_kernel,
        grid=(DEC_BATCH, nq),
        in_specs=[
            smem,
            pl.BlockSpec((BQ, 512), q_map),
            pl.BlockSpec((BQ, 1024), q_map),
            pl.BlockSpec((DEC_SEQ, 256), kv_map),
            pl.BlockSpec((DEC_SEQ, 256), kv_map),
            pl.BlockSpec((DEC_SEQ, 1024), kv_map),
            pl.BlockSpec((B_HEADS * VT_ROWS, DEC_SEQ), lambda b, j: (0, s_off + b)),
            pl.BlockSpec((None, None, PAST_LEN, 256), c_map),
            pl.BlockSpec((None, None, PAST_LEN, 256), c_map),
            pl.BlockSpec((None, None, PAST_LEN, 1024), c_map),
            pl.BlockSpec((None, None, B_HEADS * VT_ROWS, PAST_LEN), c_map),
        ],
        out_specs=pl.BlockSpec((BQ, 1024), lambda b, j: (b * nq + j, 0)),
        out_shape=jax.ShapeDtypeStruct((T_LAT, 1024), BF16),
        compiler_params=_cparams(("arbitrary", "arbitrary")),
        name="attn_latent",
    )(sink2, qa, qm, kd, vd, km, vm, kdc, vdc, kmc, vmc)
    b_map = lambda b: (b, 0)
    o_ctx = pl.pallas_call(
        _attn_ctx_kernel,
        grid=(BATCH,),
        in_specs=[
            smem,
            pl.BlockSpec((SEQ, 512), b_map),
            pl.BlockSpec((SEQ, 1024), b_map),
            pl.BlockSpec((SEQ, 256), b_map),
            pl.BlockSpec((SEQ, 256), b_map),
            pl.BlockSpec((SEQ, 1024), b_map),
            pl.BlockSpec((B_HEADS * VT_ROWS, SEQ), lambda b: (0, b)),
        ],
        out_specs=pl.BlockSpec((SEQ, 1024), b_map),
        out_shape=jax.ShapeDtypeStruct((T_CTX, 1024), BF16),
        compiler_params=_cparams(("arbitrary",)),
        name="attn_context",
    )(sink2, qa, qm, kd, vd, km, vm)
    return (o_ctx, o_lat)


def _conv_in_kernel(x_ref, mod_ref, g_ref, w_ref, u_ref, z_ref):
    for rows in _sub_rows():
        h = _rms_mod(x_ref[rows, :], g_ref[...], mod_ref[0:1, :], mod_ref[1:2, :])
        y = _dot(h.astype(BF16), w_ref[...])
        a = y[:, 0:CONV_CH]
        gate = y[:, CONV_CH:2 * CONV_CH]
        u_ref[rows, :] = a * jax.nn.sigmoid(gate)
        z_ref[rows, :] = y[:, 2 * CONV_CH:]


def _conv_in(x, mod, norm_g, w_in, layer):
    tile_map = lambda i: (i, 0)
    return pl.pallas_call(
        _conv_in_kernel,
        grid=(N_TOK,),
        in_specs=[
            pl.BlockSpec((TOK, D_MODEL), tile_map),
            pl.BlockSpec((None, None, 6, D_MODEL), lambda i: (layer, _mod_row(i), 0, 0)),
            _layer_spec((1, D_MODEL), 2 * layer),
            _layer_spec((D_MODEL, 3 * CONV_CH), layer // 2),
        ],
        out_specs=[pl.BlockSpec((TOK, CONV_CH), tile_map), pl.BlockSpec((TOK, POOL_CH), tile_map)],
        out_shape=[jax.ShapeDtypeStruct((T_ALL, CONV_CH), F32), jax.ShapeDtypeStruct((T_ALL, POOL_CH), F32)],
        compiler_params=_cparams(("arbitrary",)),
        name="conv_in",
    )(x, mod, norm_g, w_in)


CONV_ROWS = 32


def _conv_mix_kernel(u_ref, up_ref, un_ref, z_ref, zp_ref, zn_ref, wdw_ref, bdw_ref, lng_ref, lnb_ref,
                     band_ref, wg_ref, ps_ref, o_ref, ue_ref, ze_ref):
    i = pl.program_id(0)
    j = (i - N_CTX_TILES) % LAT_TILES_PER_SEQ
    is_lat = i >= N_CTX_TILES
    has_prev = jnp.logical_and(is_lat, j > 0)
    has_next = jnp.logical_and(is_lat, j < LAT_TILES_PER_SEQ - 1)
    seq_len = jnp.where(is_lat, DEC_SEQ, SEQ)
    t0 = jnp.where(is_lat, j * TILE, 0)

    ue_ref[0, 0:HALO, :] = jnp.where(has_prev, up_ref[...], 0.0)
    ue_ref[0, HALO:HALO + TILE, :] = u_ref[...]
    ue_ref[0, HALO + TILE:, :] = jnp.where(has_next, un_ref[...], 0.0)
    ze_ref[0:HALO, :] = jnp.where(has_prev, zp_ref[...], 0.0)
    ze_ref[HALO:HALO + TILE, :] = z_ref[...]
    ze_ref[HALO + TILE:, :] = jnp.where(has_next, zn_ref[...], 0.0)

    pad = CONV_WIDTH // 2
    reach = (HALO - pad + CONV_WIDTH - 1) // SUBLANES * SUBLANES
    groups = CONV_ROWS // SUBLANES
    n_chunks = TILE // CONV_ROWS

    def shift_rows(lo, hi):
        for b in range(1, SUBLANES):
            ue_ref[b, lo:hi, :] = ue_ref[0, lo + b:hi + b, :]

    def conv_chunk(r):
        r0 = r * CONV_ROWS
        acc = jnp.zeros((groups, SUBLANES, CONV_CH), F32) + bdw_ref[...]
        for k in range(CONV_WIDTH):
            a, b = divmod(HALO + k - pad, SUBLANES)
            e0 = r0 + a * SUBLANES
            acc = acc + ue_ref[b, e0:e0 + CONV_ROWS, :].reshape(groups, SUBLANES, CONV_CH) * wdw_ref[k]
        acc = acc.reshape(CONV_ROWS, CONV_CH)
        mu = jnp.mean(acc, axis=-1, keepdims=True)
        d = acc - mu
        var = jnp.mean(d * d, axis=-1, keepdims=True)
        yn = d * lax.rsqrt(var + EPS) * lng_ref[...] + lnb_ref[...]
        o_ref[r0:r0 + CONV_ROWS, 0:CONV_CH] = (yn * jax.nn.sigmoid(yn)).astype(BF16)

    t = t0 + lax.broadcasted_iota(jnp.int32, (TILE, POOL_GROUP_W), 0)
    ze = ze_ref[...]
    z_hi = ze.astype(BF16)
    z_lo = (ze - z_hi.astype(F32)).astype(BF16)

    def pool_group(gi):
        w = POOL_SIZES[gi]
        lo = w // 2
        hi = w - lo - 1
        cols = slice(gi * POOL_GROUP_W, (gi + 1) * POOL_GROUP_W)
        tot = _dot(band_ref[gi], z_hi[:, cols]) + _dot(band_ref[gi], z_lo[:, cols])
        cnt = jnp.minimum(t + hi + 1, seq_len) - jnp.maximum(t - lo, 0)
        d = tot / cnt.astype(F32) - ze_ref[HALO:HALO + TILE, cols]
        y = _dot(d.astype(BF16), wg_ref[gi]) * ps_ref[:, cols]
        o_ref[:, CONV_CH + gi * POOL_GROUP_W:CONV_CH + (gi + 1) * POOL_GROUP_W] = y.astype(BF16)

    shift_rows(0, reach + CONV_ROWS)
    for r in range(n_chunks):
        if r + 1 < n_chunks:
            shift_rows(reach + (r + 1) * CONV_ROWS, reach + (r + 2) * CONV_ROWS)
        conv_chunk(r)
        if r % (n_chunks // len(POOL_SIZES)) == 0:
            pool_group(r // (n_chunks // len(POOL_SIZES)))


def _conv_mix(u, z, cp, jj):
    tile_map = lambda i: (i, 0)
    hb = TILE // HALO
    prev_map = lambda i: (jnp.maximum(i * hb - 1, 0), 0)
    next_map = lambda i: (jnp.minimum((i + 1) * hb, T_ALL // HALO - 1), 0)
    return pl.pallas_call(
        _conv_mix_kernel,
        grid=(N_TILES,),
        in_specs=[
            pl.BlockSpec((TILE, CONV_CH), tile_map),
            pl.BlockSpec((HALO, CONV_CH), prev_map),
            pl.BlockSpec((HALO, CONV_CH), next_map),
            pl.BlockSpec((TILE, POOL_CH), tile_map),
            pl.BlockSpec((HALO, POOL_CH), prev_map),
            pl.BlockSpec((HALO, POOL_CH), next_map),
            _layer_spec((CONV_WIDTH, SUBLANES, CONV_CH), jj),
            _layer_spec((1, CONV_CH), jj),
            _layer_spec((1, CONV_CH), jj),
            _layer_spec((1, CONV_CH), jj),
            pl.BlockSpec((len(POOL_SIZES), TILE, TILE + 2 * HALO), lambda i: (0, 0, 0)),
            _layer_spec((len(POOL_SIZES), POOL_GROUP_W, POOL_GROUP_W), jj),
            _layer_spec((1, POOL_CH), jj),
        ],
        out_specs=pl.BlockSpec((TILE, D_MODEL), tile_map),
        out_shape=jax.ShapeDtypeStruct((T_ALL, D_MODEL), BF16),
        scratch_shapes=[pltpu.VMEM((SUBLANES, TILE + 2 * HALO, CONV_CH), F32),
                        pltpu.VMEM((TILE + 2 * HALO, POOL_CH), F32)],
        compiler_params=_cparams(("arbitrary",)),
        name="conv_mix",
    )(u, u, u, z, z, z, cp["w_dw"], cp["b_dw"], cp["ln_g"], cp["ln_b"], _pool_bands(), cp["w_grp"], cp["p_scale"])


def _pool_bands():
    t = np.arange(TILE)[:, None]
    e = np.arange(TILE + 2 * HALO)[None, :] - HALO
    bands = []
    for w in POOL_SIZES:
        lo = w // 2
        hi = w - lo - 1
        bands.append((e >= t - lo) & (e <= t + hi))
    return jnp.asarray(np.stack(bands), BF16)


def _post_kernel(*refs, n_o, n_x, final):
    o_refs, x_refs = refs[:n_o], refs[n_o:n_o + n_x]
    mod_ref, g_ref, wo_ref, w1_ref, w2_ref, fg_ref = refs[n_o + n_x:n_o + n_x + 6]
    y_refs = refs[n_o + n_x + 6:]
    subs = _sub_rows()
    x1s, hs, acts, outs = [], [], [], []
    for rows in subs:
        x1 = _tok_load(x_refs, rows) + mod_ref[2:3, :] * _dot(_tok_load(o_refs, rows), wo_ref[...])
        x1s.append(x1)
        hs.append(_rms_mod(x1, g_ref[...], mod_ref[3:4, :], mod_ref[4:5, :]).astype(BF16))
    for h in hs:
        a = jnp.maximum(_dot(h, w1_ref[...]), 0.0)
        acts.append((a * a).astype(BF16))
    for rows, x1, a in zip(subs, x1s, acts):
        x2 = x1 + mod_ref[5:6, :] * _dot(a, w2_ref[...])
        if final:
            ms = jnp.mean(x2 * x2, axis=-1, keepdims=True)
            x2 = x2 * lax.rsqrt(ms + EPS) * fg_ref[...]
        outs.append((rows, x2))
    _tok_store(y_refs, outs)


def _post(o, x, mod, norm_g, w_out, w1, w2, final_g, layer, final):
    const2 = lambda i: (0, 0)
    o_ops, o_specs = _tok_specs(o, D_MODEL)
    x_ops, x_specs = _tok_specs(x, D_MODEL)
    if final:
        y = (jax.ShapeDtypeStruct((T_CTX, D_MODEL), F32), jax.ShapeDtypeStruct((T_LAT, D_MODEL), F32))
    else:
        y = jax.ShapeDtypeStruct((T_ALL, D_MODEL), F32)
    _, y_specs = _tok_specs(y, D_MODEL)
    out = pl.pallas_call(
        functools.partial(_post_kernel, n_o=len(o_ops), n_x=len(x_ops), final=final),
        grid=(N_TOK,),
        in_specs=o_specs + x_specs + [
            pl.BlockSpec((None, None, 6, D_MODEL), lambda i: (layer, _mod_row(i), 0, 0)),
            _layer_spec((1, D_MODEL), 2 * layer + 1),
            pl.BlockSpec((None, D_MODEL, D_MODEL), lambda i: (layer // 2, 0, 0), pipeline_mode=pl.Buffered(1)),
            pl.BlockSpec((None, D_MODEL, D_FF), lambda i: (layer, 0, 0), pipeline_mode=pl.Buffered(1)),
            pl.BlockSpec((None, D_FF, D_MODEL), lambda i: (layer, 0, 0), pipeline_mode=pl.Buffered(1)),
            pl.BlockSpec((1, D_MODEL), const2),
        ],
        out_specs=y_specs if final else y_specs[0],
        out_shape=y,
        compiler_params=_cparams(("arbitrary",)),
        name="post_final" if final else "post",
    )(*o_ops, *x_ops, mod, norm_g, w_out, w1, w2, final_g)
    return out


def _pair_lanes(w):
    lead = w.shape[:-1]
    w = w.reshape(lead + (w.shape[-1] // LANES, 2, 2, HEAD_DIM // 2))
    return jnp.swapaxes(w, -3, -2).reshape(lead + (-1,))


def _mla_lanes(nope, rope, xp=jnp):
    shape = (nope if nope is not None else rope).shape[:-1]
    nope = xp.zeros(shape + (QK_NOPE,), xp.float32) if nope is None else nope
    rope = xp.zeros(shape + (QK_ROPE,), xp.float32) if rope is None else rope
    h = QK_ROPE // 2
    return xp.concatenate([rope[..., :h], nope[..., :64 - h], rope[..., h:], nope[..., 64 - h:],
                           xp.zeros(shape + (LANES - QK_NOPE - QK_ROPE,), xp.float32)], axis=-1)


def _rope_tables():
    f32 = np.float32
    n = DEC_SEQ
    rows = n // GRID_W
    row = np.repeat(np.arange(rows), GRID_W).astype(f32)
    col = np.tile(np.arange(GRID_W), rows).astype(f32)

    def angles(dim):
        quarter = dim // 4
        inv_freq = (f32(ROPE_BASE) ** (-np.arange(quarter, dtype=f32) / f32(quarter))).astype(f32)
        return np.concatenate([row[:, None] * inv_freq, col[:, None] * inv_freq], axis=-1).astype(f32)

    ang_a = angles(HEAD_DIM)
    cos_a, sin_a = np.cos(ang_a), np.sin(ang_a)
    c_a = np.concatenate([cos_a] * 4, axis=-1)
    s_a = np.concatenate([-sin_a, -sin_a, sin_a, sin_a], axis=-1)
    ang_b = angles(QK_ROPE)
    cos_b, sin_b = np.cos(ang_b), np.sin(ang_b)
    c_b = _mla_lanes(np.ones((n, QK_NOPE), f32), np.concatenate([cos_b, cos_b], axis=-1), np)
    s_b = _mla_lanes(None, np.concatenate([-sin_b, sin_b], axis=-1), np)
    lat = np.stack([c_a, s_a, c_b, s_b])
    ident = np.stack([np.ones((TOK_IN, LANES), f32), np.zeros((TOK_IN, LANES), f32)] * 2)
    return jnp.asarray(np.concatenate([ident, lat], axis=1).astype(f32))


def _prep_even(attn_w_in, mla_q_norm, mla_kv_norm, mla_w_qb, mla_w_kvb, attn_w_out):
    w = attn_w_in
    o = np.cumsum((0, 512, 128, 128, Q_LORA, KV_LORA, QK_ROPE))
    zeros = lambda n: jnp.zeros((N_EVEN, D_MODEL, n), F32)
    w_in = jnp.concatenate([
        _pair_lanes(w[..., o[0]:o[1]]), _pair_lanes(w[..., o[1]:o[2]]), w[..., o[2]:o[3]],
        w[..., o[3]:o[4]], zeros(Q_LORA_PAD - Q_LORA),
        w[..., o[4]:o[5]],
        _mla_lanes(None, w[..., o[5]:o[6]]),
    ], axis=-1).astype(BF16)
    q_norm = jnp.pad(mla_q_norm, ((0, 0), (0, Q_LORA_PAD - Q_LORA))).reshape(N_EVEN, 1, Q_LORA_PAD)
    wqb = mla_w_qb.reshape(N_EVEN, Q_LORA, B_HEADS, QK_NOPE + QK_ROPE)
    wqb = jnp.pad(_mla_lanes(wqb[..., :QK_NOPE], wqb[..., QK_NOPE:]),
                  ((0, 0), (0, Q_LORA_PAD - Q_LORA), (0, 0), (0, 0)))
    wkvb = mla_w_kvb.reshape(N_EVEN, KV_LORA, B_HEADS, QK_NOPE + V_DIM)
    w_kk = _mla_lanes(wkvb[..., :QK_NOPE], None)
    w_kv = wkvb[..., QK_NOPE:].reshape(N_EVEN, KV_LORA, B_HEADS * V_DIM)
    return {
        "w_in": w_in,
        "q_norm": q_norm,
        "kv_norm": mla_kv_norm.reshape(N_EVEN, 1, KV_LORA),
        "w_qb": wqb.reshape(N_EVEN, Q_LORA_PAD, B_HEADS * LANES).astype(BF16),
        "w_kk": w_kk.reshape(N_EVEN, KV_LORA, B_HEADS * LANES).astype(BF16),
        "w_kv": jnp.swapaxes(w_kv, 1, 2).astype(BF16),
        "w_out": attn_w_out.astype(BF16),
    }


def kernel(x_prompt, x_sample, cache_win_k, cache_win_v, cache_mla_ckv, cache_mla_krope, c, c_ctx, w_mod, b_mod,
           norm_g, attn_w_in, attn_sink, mla_q_norm, mla_kv_norm, mla_w_qb, mla_w_kvb, attn_w_out, conv_w_in,
           conv_dw, conv_dw_b, conv_ln_g, conv_ln_b, pool_w, pool_scale, conv_w_out, mlp_w1, mlp_w2, final_g):
    x = (x_prompt.reshape(T_CTX, D_MODEL), x_sample.reshape(T_LAT, D_MODEL))

    cond = jnp.concatenate([c_ctx[None, :], c, jnp.zeros((N_COND - 1 - DEC_BATCH, D_MODEL), F32)], axis=0)
    mod = _mod_table(cond, w_mod, b_mod).reshape(DEPTH, N_COND, 6, D_MODEL)

    rope = _rope_tables()
    even = _prep_even(attn_w_in, mla_q_norm, mla_kv_norm, mla_w_qb, mla_w_kvb, attn_w_out)
    kdc, vdc, kmc, vmc = _ctx_kv(
        _pair_lanes(cache_win_k.reshape(DEC_BATCH, N_EVEN, PAST_LEN, LANES)),
        cache_win_v.reshape(DEC_BATCH, N_EVEN, PAST_LEN, LANES),
        cache_mla_ckv, _mla_lanes(None, cache_mla_krope), even["w_kk"], even["w_kv"])
    odd = {
        "w_dw": jnp.broadcast_to(conv_dw[:, :, None, :], (N_ODD, CONV_WIDTH, SUBLANES, CONV_CH)),
        "b_dw": conv_dw_b.reshape(N_ODD, 1, CONV_CH),
        "ln_g": conv_ln_g.reshape(N_ODD, 1, CONV_CH),
        "ln_b": conv_ln_b.reshape(N_ODD, 1, CONV_CH),
        "w_grp": pool_w.astype(BF16),
        "p_scale": pool_scale.reshape(N_ODD, 1, POOL_CH),
    }
    conv_w_in_b, conv_w_out_b = conv_w_in.astype(BF16), conv_w_out.astype(BF16)
    w1_all, w2_all = mlp_w1.astype(BF16), mlp_w2.astype(BF16)
    gains = norm_g.reshape(2 * DEPTH, 1, D_MODEL)
    final_g2 = final_g.reshape(1, D_MODEL)
    sink2 = attn_sink * LOG2E

    caches = []
    for l in range(DEPTH):
        if l % 2 == 0:
            i = l // 2
            qa, kd, vd, qm, km, vm, ka32, va32, ckv32, kr32 = _attn_in(x, mod, gains, even, rope, l)
            caches.append((ka32, va32, ckv32, kr32))
            o = _attention(sink2[i], qa, kd, vd, qm, km, vm, kdc, vdc, kmc, vmc, i)
            w_out = even["w_out"]
        else:
            u, z = _conv_in(x, mod, gains, conv_w_in_b, l)
            o = _conv_mix(u, z, odd, l // 2)
            w_out = conv_w_out_b
        x = _post(o, x, mod, gains, w_out, w1_all, w2_all, final_g2, l, final=(l == DEPTH - 1))

    ka, va, ckv, kr = (jnp.stack(t, axis=1) for t in zip(*caches))
    new_k = jnp.swapaxes(ka.reshape(BATCH, SEQ, N_EVEN, 2, A_KV_HEADS, HEAD_DIM // 2), 3, 4)
    new_k = jnp.swapaxes(new_k.reshape(BATCH, SEQ, N_EVEN, A_KV_HEADS, HEAD_DIM), 1, 2)
    new_v = jnp.swapaxes(va.reshape(BATCH, SEQ, N_EVEN, A_KV_HEADS, HEAD_DIM), 1, 2)
    new_ckv = jnp.swapaxes(ckv.reshape(BATCH, SEQ, N_EVEN, KV_LORA), 1, 2)
    new_kr = jnp.concatenate([kr[..., :QK_ROPE // 2], kr[..., 64:64 + QK_ROPE // 2]], axis=-1)
    new_kr = jnp.swapaxes(new_kr.reshape(BATCH, SEQ, N_EVEN, QK_ROPE), 1, 2)
    y_prompt = x[0].reshape(BATCH, SEQ, D_MODEL)
    y_sample = x[1].reshape(DEC_BATCH, DEC_SEQ, D_MODEL)
    return (y_prompt, y_sample, new_k, new_v, new_ckv, new_kr)
```

```python
import functools
import math

import numpy as np
import jax
import jax.numpy as jnp
from jax import lax
from jax.experimental import pallas as pl
from jax.experimental.pallas import tpu as pltpu

F32 = jnp.float32
BF16 = jnp.bfloat16

D_MODEL = 1024
BATCH = 16
SEQ = 256
DEPTH = 4
DEC_BATCH = 8
DEC_SEQ = 2048
PAST_LEN = 256
GRID_W = 64
N_EVEN = (DEPTH + 1) // 2
N_ODD = DEPTH // 2
A_HEADS = 8
A_KV_HEADS = 2
A_GROUP = A_HEADS // A_KV_HEADS
HEAD_DIM = 64
WINDOW = 128
B_HEADS = 8
Q_LORA = 192
KV_LORA = 128
QK_NOPE = 64
QK_ROPE = 32
V_DIM = 64
MLA_SCALE = (QK_NOPE + QK_ROPE) ** -0.5
CONV_CH = D_MODEL // 2
CONV_WIDTH = 31
POOL_CH = D_MODEL // 2
POOL_SIZES = (2, 4, 8, 16)
POOL_GROUP_W = POOL_CH // len(POOL_SIZES)
D_FF = 4 * D_MODEL
ROPE_BASE = 10000.0
EPS = 1e-6
NEG_INF = -1e30
LOG2E = math.log2(math.e)

LANES = 128
SUBLANES = 8
VMEM_LIMIT_BYTES = 56 * 1024 * 1024

TILE = 256
T_CTX = BATCH * SEQ
T_LAT = DEC_BATCH * DEC_SEQ
T_ALL = T_CTX + T_LAT
N_CTX_TILES = T_CTX // TILE
N_TILES = T_ALL // TILE
LAT_TILES_PER_SEQ = DEC_SEQ // TILE
TOK = 512
N_CTX_TOK = T_CTX // TOK
N_TOK = T_ALL // TOK
TOK_IN = 1024
N_COND = 16
BQ = 256
VT_ROWS = 80
HALO = 16
Q_LORA_PAD = 256
ATTN_IN_COLS = 512 + 128 + 128 + Q_LORA_PAD + 128 + 128


def _cparams(sem):
    return pltpu.CompilerParams(dimension_semantics=sem, vmem_limit_bytes=VMEM_LIMIT_BYTES)


def _mod_row(i, tok=TOK):
    return jnp.where(i < T_CTX // tok, 0, 1 + (i - T_CTX // tok) // (DEC_SEQ // tok))


def _pos_block(i, tok):
    return jnp.where(i < T_CTX // tok, 0, 1 + (i - T_CTX // tok) % (DEC_SEQ // tok))


def _tok_specs(a, width, tok=TOK):
    if isinstance(a, tuple):
        return list(a), [pl.BlockSpec((tok, width), lambda i: (jnp.minimum(i, T_CTX // tok - 1), 0)),
                         pl.BlockSpec((tok, width), lambda i: (jnp.maximum(i - T_CTX // tok, 0), 0))]
    return [a], [pl.BlockSpec((tok, width), lambda i: (i, 0))]


def _tok_load(refs, rows, tok=TOK):
    if len(refs) == 1:
        return refs[0][rows, :]
    return jnp.where(pl.program_id(0) < T_CTX // tok, refs[0][rows, :], refs[1][rows, :])


def _tok_store(refs, parts):
    def put(ref):
        for rows, val in parts:
            ref[rows, :] = val

    if len(refs) == 1:
        put(refs[0])
        return
    i = pl.program_id(0)

    @pl.when(i < N_CTX_TOK)
    def _():
        put(refs[0])

    @pl.when(i >= N_CTX_TOK)
    def _():
        put(refs[1])


def _layer_spec(shape, idx):
    zeros = (0,) * len(shape)
    return pl.BlockSpec((None,) + tuple(shape), lambda *_: (idx,) + zeros)


def _sub_rows(tok=TOK):
    return [slice(s * TILE, (s + 1) * TILE) for s in range(tok // TILE)]


def _dot(a, b):
    return jnp.dot(a, b, preferred_element_type=F32)


def _dot_nt(a, b):
    return lax.dot_general(a, b, (((1,), (1,)), ((), ())), preferred_element_type=F32)


def _rms_mod(x, g, shift, scale):
    ms = jnp.mean(x * x, axis=-1, keepdims=True)
    return x * lax.rsqrt(ms + EPS) * (g * (1.0 + scale)) + shift


def _rope(x, c, s):
    return x * c + pltpu.roll(x, 64, 1) * s


def _lane_lt64(shape):
    return lax.broadcasted_iota(jnp.int32, shape, len(shape) - 1) < 64


def _lane_even32(shape):
    return (lax.broadcasted_iota(jnp.int32, shape, len(shape) - 1) & 63) < 32


def _store_kv_dup(kd_ref, vd_ref, rows, k, v):
    even32 = _lane_even32(k.shape)
    lt64 = _lane_lt64(v.shape)
    kd_ref[rows, 0:LANES] = jnp.where(even32, k, pltpu.roll(k, 32, 1)).astype(BF16)
    kd_ref[rows, LANES:2 * LANES] = jnp.where(even32, pltpu.roll(k, LANES - 32, 1), k).astype(BF16)
    vd_ref[rows, 0:LANES] = jnp.where(lt64, v, 1.0).astype(BF16)
    vd_ref[rows, LANES:2 * LANES] = jnp.where(lt64, pltpu.roll(v, 64, 1), 1.0).astype(BF16)


def _store_vt(vt_ref, cols, ckvn, wkvt_ref):
    vt = _dot(wkvt_ref[...], ckvn.T.astype(BF16))
    ones = jnp.ones((VT_ROWS - V_DIM, ckvn.shape[0]), BF16)
    for hh in range(B_HEADS):
        vt_ref[hh * VT_ROWS:hh * VT_ROWS + V_DIM, cols] = vt[hh * V_DIM:(hh + 1) * V_DIM].astype(BF16)
        vt_ref[hh * VT_ROWS + V_DIM:(hh + 1) * VT_ROWS, cols] = ones


def _mod_kernel(cond_ref, w_ref, b_ref, o_ref):
    c = cond_ref[...]
    s = c * jax.nn.sigmoid(c)
    o_ref[...] = _dot(s.astype(BF16), w_ref[...].astype(BF16)) + b_ref[...]


def _mod_table(cond, w_mod, b_mod):
    nb = 6 * D_MODEL // 1024
    return pl.pallas_call(
        _mod_kernel,
        grid=(DEPTH, nb),
        in_specs=[
            pl.BlockSpec((N_COND, D_MODEL), lambda l, n: (0, 0)),
            pl.BlockSpec((None, D_MODEL, 1024), lambda l, n: (l, 0, n)),
            pl.BlockSpec((None, 1, 1024), lambda l, n: (l, 0, n)),
        ],
        out_specs=pl.BlockSpec((None, N_COND, 1024), lambda l, n: (l, 0, n)),
        out_shape=jax.ShapeDtypeStruct((DEPTH, N_COND, 6 * D_MODEL), F32),
        compiler_params=_cparams(("arbitrary", "arbitrary")),
        name="mod_table",
    )(cond, w_mod, b_mod.reshape(DEPTH, 1, 6 * D_MODEL))


def _attn_in_kernel(*refs, n_x):
    x_refs = refs[:n_x]
    (mod_ref, g_ref, w_ref, qn_ref, kvn_ref, wqb_ref, wkk_ref, wkv_ref, rope_ref,
     qa_ref, kd_ref, vd_ref, qm_ref, km_ref, vm_ref, ka32_ref, va32_ref, ckv32_ref, kr32_ref) = refs[n_x:]
    col = lambda n: slice(n * LANES, (n + 1) * LANES)

    def project(rows):
        h = _rms_mod(_tok_load(x_refs, rows, TOK_IN), g_ref[...], mod_ref[0:1, :], mod_ref[1:2, :])
        return _dot(h.astype(BF16), w_ref[...])

    def derive(rows, y):
        ca, sa = rope_ref[0, rows, :], rope_ref[1, rows, :]
        cb, sb = rope_ref[2, rows, :], rope_ref[3, rows, :]

        for c in range(4):
            q = _rope(y[:, col(c)], ca, sa)
            qa_ref[rows, col(c)] = (q * (HEAD_DIM ** -0.5 * LOG2E)).astype(BF16)

        ka = _rope(y[:, 512:640], ca, sa)
        va = y[:, 640:768]
        _store_kv_dup(kd_ref, vd_ref, rows, ka, va)

        cq = y[:, 768:768 + Q_LORA_PAD]
        cqn = cq * lax.rsqrt(jnp.sum(cq * cq, axis=-1, keepdims=True) * (1.0 / Q_LORA) + EPS) * qn_ref[...]
        qm = _dot(cqn.astype(BF16), wqb_ref[...])
        for hh in range(B_HEADS):
            q = _rope(qm[:, col(hh)], cb, sb)
            qm_ref[rows, col(hh)] = (q * (MLA_SCALE * LOG2E)).astype(BF16)

        ckv = y[:, 1024:1152]
        ckvn = ckv * lax.rsqrt(jnp.mean(ckv * ckv, axis=-1, keepdims=True) + EPS) * kvn_ref[...]
        kr = y[:, 1152:1280]
        ckvn_b = ckvn.astype(BF16)
        kn = _dot(ckvn_b, wkk_ref[...])
        krr = _rope(kr, cb, sb)
        for hh in range(B_HEADS):
            km_ref[rows, col(hh)] = (kn[:, col(hh)] + krr).astype(BF16)
        _store_vt(vm_ref, rows, ckvn, wkv_ref)
        return ka, va, ckvn, kr

    subs = _sub_rows(TOK_IN)
    cache = []
    y = project(subs[0])
    for s, rows in enumerate(subs):
        y_next = project(subs[s + 1]) if s + 1 < len(subs) else None
        cache.append(derive(rows, y))
        y = y_next

    @pl.when(pl.program_id(0) < T_CTX // TOK_IN)
    def _():
        for rows, (ka, va, ckvn, kr) in zip(subs, cache):
            ka32_ref[rows, :] = ka
            va32_ref[rows, :] = va
            ckv32_ref[rows, :] = ckvn
            kr32_ref[rows, :] = kr


def _attn_in(x, mod, norm_g, wp, rope, layer):
    tile_map = lambda i: (i, 0)
    li = layer // 2
    ctx_map = lambda i: (jnp.minimum(i, T_CTX // TOK_IN - 1), 0)
    bf = lambda w: jax.ShapeDtypeStruct((T_ALL, w), BF16)
    c32 = jax.ShapeDtypeStruct((T_CTX, LANES), F32)
    x_ops, x_specs = _tok_specs(x, D_MODEL, TOK_IN)
    return pl.pallas_call(
        functools.partial(_attn_in_kernel, n_x=len(x_ops)),
        grid=(T_ALL // TOK_IN,),
        in_specs=x_specs + [
            pl.BlockSpec((None, None, 6, D_MODEL), lambda i: (layer, _mod_row(i, TOK_IN), 0, 0)),
            _layer_spec((1, D_MODEL), 2 * layer),
            _layer_spec((D_MODEL, ATTN_IN_COLS), li),
            _layer_spec((1, Q_LORA_PAD), li),
            _layer_spec((1, KV_LORA), li),
            _layer_spec((Q_LORA_PAD, B_HEADS * LANES), li),
            _layer_spec((KV_LORA, B_HEADS * LANES), li),
            _layer_spec((B_HEADS * V_DIM, KV_LORA), li),
            pl.BlockSpec((4, TOK_IN, LANES), lambda i: (0, _pos_block(i, TOK_IN), 0)),
        ],
        out_specs=[
            pl.BlockSpec((TOK_IN, 512), tile_map),
            pl.BlockSpec((TOK_IN, 256), tile_map),
            pl.BlockSpec((TOK_IN, 256), tile_map),
            pl.BlockSpec((TOK_IN, 1024), tile_map),
            pl.BlockSpec((TOK_IN, 1024), tile_map),
            pl.BlockSpec((B_HEADS * VT_ROWS, TOK_IN), lambda i: (0, i)),
            pl.BlockSpec((TOK_IN, LANES), ctx_map),
            pl.BlockSpec((TOK_IN, LANES), ctx_map),
            pl.BlockSpec((TOK_IN, LANES), ctx_map),
            pl.BlockSpec((TOK_IN, LANES), ctx_map),
        ],
        out_shape=[bf(512), bf(256), bf(256), bf(1024), bf(1024),
                   jax.ShapeDtypeStruct((B_HEADS * VT_ROWS, T_ALL), BF16), c32, c32, c32, c32],
        compiler_params=_cparams(("arbitrary",)),
        name="attn_in",
    )(*x_ops, mod, norm_g, wp["w_in"], wp["q_norm"], wp["kv_norm"], wp["w_qb"], wp["w_kk"], wp["w_kv"], rope)


def _ctx_kv_kernel(ck_ref, cv_ref, cckv_ref, ckr_ref, wkk_ref, wkv_ref, kd_ref, vd_ref, km_ref, vm_ref):
    _store_kv_dup(kd_ref, vd_ref, slice(None), ck_ref[...], cv_ref[...])
    cb = cckv_ref[...].astype(BF16)
    kn = _dot(cb, wkk_ref[...])
    kr = ckr_ref[...]
    for hh in range(B_HEADS):
        km_ref[:, hh * LANES:(hh + 1) * LANES] = (kn[:, hh * LANES:(hh + 1) * LANES] + kr).astype(BF16)
    _store_vt(vm_ref, slice(None), cckv_ref[...], wkv_ref)


def _ctx_kv(cache_k, cache_v, cache_ckv, cache_kr128, w_kk, w_kv):
    cache_map = lambda i, b: (b, i, 0, 0)
    w_map = lambda i, b: (i, 0, 0)
    out_map = lambda i, b: (i, b, 0, 0)
    o = lambda w: jax.ShapeDtypeStruct((N_EVEN, DEC_BATCH, PAST_LEN, w), BF16)
    return pl.pallas_call(
        _ctx_kv_kernel,
        grid=(N_EVEN, DEC_BATCH),
        in_specs=[
            pl.BlockSpec((None, None, PAST_LEN, LANES), cache_map),
            pl.BlockSpec((None, None, PAST_LEN, LANES), cache_map),
            pl.BlockSpec((None, None, PAST_LEN, KV_LORA), cache_map),
            pl.BlockSpec((None, None, PAST_LEN, LANES), cache_map),
            pl.BlockSpec((None, KV_LORA, B_HEADS * LANES), w_map),
            pl.BlockSpec((None, B_HEADS * V_DIM, KV_LORA), w_map),
        ],
        out_specs=[
            pl.BlockSpec((None, None, PAST_LEN, 256), out_map),
            pl.BlockSpec((None, None, PAST_LEN, 256), out_map),
            pl.BlockSpec((None, None, PAST_LEN, 1024), out_map),
            pl.BlockSpec((None, None, B_HEADS * VT_ROWS, PAST_LEN), out_map),
        ],
        out_shape=[o(256), o(256), o(1024),
                   jax.ShapeDtypeStruct((N_EVEN, DEC_BATCH, B_HEADS * VT_ROWS, PAST_LEN), BF16)],
        compiler_params=_cparams(("arbitrary", "arbitrary")),
        name="ctx_kv",
    )(cache_k, cache_v, cache_ckv, cache_kr128, w_kk, w_kv)


def _attn_heads(qa_ref, qm_ref, a_segs, m_segs, sink_ref, o_ref, rows, depth):
    lt64 = _lane_lt64((rows, LANES))
    lane = lax.broadcasted_iota(jnp.int32, (1, LANES), 1)
    keep = (((lane & 63) < 32).astype(BF16), ((lane & 63) >= 32).astype(BF16))
    col = lambda n: slice(n * LANES, (n + 1) * LANES)
    swap = lambda r: pltpu.roll(r, 64, 1)

    jobs = []
    for kh in range(A_KV_HEADS):
        heads = [dict(out=(kh * A_GROUP + g) // 2, half=g % 2, sink=kh * A_GROUP + g) for g in range(A_GROUP)]
        jobs.append(dict(
            heads=heads,
            q=lambda heads=heads: jnp.concatenate(
                [qa_ref[:, col(h["out"])] * keep[h["half"]] for h in heads], axis=0),
            ks=[lambda kd=kd, rs=rs, kh=kh: kd[rs, col(kh)] for kd, _, rs, _ in a_segs],
            vs=[lambda vd=vd, rs=rs, kh=kh: vd[rs, col(kh)] for _, vd, rs, _ in a_segs],
            valids=[valid for _, _, _, valid in a_segs]))
    for hh in range(B_HEADS):
        jobs.append(dict(
            keys_on_rows=True,
            heads=[dict(out=4 + hh // 2, half=hh % 2)],
            q=lambda hh=hh: qm_ref[:, col(hh)],
            ks=[lambda km=km, hh=hh: km[:, col(hh)] for km, _ in m_segs],
            vs=[lambda vmt=vmt, hh=hh: vmt[hh * VT_ROWS:(hh + 1) * VT_ROWS, :] for _, vmt in m_segs]))

    def scores(job):
        if job.get("keys_on_rows"):
            h = job["heads"][0]
            q = job["q"]()
            h["ss"] = [_dot_nt(k(), q) for k in job["ks"]]
            m = h["ss"][0].max(axis=0, keepdims=True)
            for s in h["ss"][1:]:
                m = jnp.maximum(m, s.max(axis=0, keepdims=True))
            h["m"] = m
            return
        full = [_dot_nt(job["q"](), k()) for k in job["ks"]]
        for b, h in enumerate(job["heads"]):
            blk = slice(b * rows, (b + 1) * rows)
            ss = [s[blk] if valid is None else jnp.where(valid, s[blk], NEG_INF)
                  for s, valid in zip(full, job["valids"])]
            m = ss[0].max(axis=-1, keepdims=True)
            for s in ss[1:]:
                m = jnp.maximum(m, s.max(axis=-1, keepdims=True))
            sink = sink_ref[h["sink"]]
            m = jnp.maximum(m, sink)
            h["e"] = jnp.exp2(sink - m)
            h["ss"], h["m"] = ss, m

    def values(job):
        if job.get("keys_on_rows"):
            h = job["heads"][0]
            r = None
            for s, vt in zip(h.pop("ss"), job["vs"]):
                rs_ = _dot(vt(), jnp.exp2(s - h["m"]).astype(BF16))
                r = rs_ if r is None else r + rs_
            del h["m"]
            h["r"] = r
            return
        r = None
        for i, v in enumerate(job["vs"]):
            p = [jnp.exp2(h["ss"][i] - h["m"]).astype(BF16) for h in job["heads"]]
            rs_ = _dot(jnp.concatenate(p, axis=0), v())
            r = rs_ if r is None else r + rs_
        for b, h in enumerate(job["heads"]):
            del h["ss"], h["m"]
            h["r"] = r[b * rows:(b + 1) * rows]

    def finish(job, done):
        if job.get("keys_on_rows"):
            h = job["heads"][0]
            r = h.pop("r")
            o_t = r[0:V_DIM] / r[V_DIM:V_DIM + 1]
            other = done.pop(h["out"], None)
            if other is None:
                done[h["out"]] = o_t
            else:
                lo, hi = (other, o_t) if h["half"] == 1 else (o_t, other)
                o_ref[:, col(h["out"])] = jnp.concatenate([lo, hi], axis=0).T.astype(BF16)
            return
        for h in job["heads"]:
            r = h.pop("r")
            if h["half"] == 1:
                o = swap(r) / (r + h["e"])
            else:
                o = r / (swap(r) + h["e"])
            other = done.pop(h["out"], None)
            if other is None:
                done[h["out"]] = o
            else:
                lo, hi = (other, o) if h["half"] == 1 else (o, other)
                o_ref[:, col(h["out"])] = jnp.where(lt64, lo, hi).astype(BF16)

    done = {}
    for t in range(len(jobs) + depth):
        if t < len(jobs):
            scores(jobs[t])
        if t >= depth:
            values(jobs[t - depth])
            finish(jobs[t - depth], done)


def _attn_lat_kernel(sink_ref, qa_ref, qm_ref, kd_ref, vd_ref, km_ref, vm_ref,
                     kdc_ref, vdc_ref, kmc_ref, vmc_ref, o_ref):
    j = pl.program_id(1)
    q0 = j * BQ
    nloc = BQ + 2 * WINDOW
    start = pl.multiple_of(jnp.clip(q0 - WINDOW, 0, DEC_SEQ - nloc), WINDOW)
    qpos = q0 + lax.broadcasted_iota(jnp.int32, (BQ, nloc), 0)
    kpos = start + lax.broadcasted_iota(jnp.int32, (BQ, nloc), 1)
    valid = jnp.abs(qpos - kpos) <= WINDOW
    everything = slice(None)
    a_segs = [(kd_ref, vd_ref, pl.ds(start, nloc), valid), (kdc_ref, vdc_ref, everything, None)]
    m_segs = [(km_ref, vm_ref), (kmc_ref, vmc_ref)]
    _attn_heads(qa_ref, qm_ref, a_segs, m_segs, sink_ref, o_ref, BQ, depth=3)


def _attn_ctx_kernel(sink_ref, qa_ref, qm_ref, kd_ref, vd_ref, km_ref, vm_ref, o_ref):
    a_segs = [(kd_ref, vd_ref, slice(None), None)]
    m_segs = [(km_ref, vm_ref)]
    _attn_heads(qa_ref, qm_ref, a_segs, m_segs, sink_ref, o_ref, SEQ, depth=1)


def _attention(sink2, qa, kd, vd, qm, km, vm, kdc, vdc, kmc, vmc, layer_i):
    smem = pl.BlockSpec(memory_space=pltpu.SMEM)
    nq = DEC_SEQ // BQ
    q_off = T_CTX // BQ
    s_off = T_CTX // DEC_SEQ
    q_map = lambda b, j: (q_off + b * nq + j, 0)
    kv_map = lambda b, j: (s_off + b, 0)
    c_map = lambda b, j: (layer_i, b, 0, 0)
    o_lat = pl.pallas_call(
        _attn_lat_kernel,
        grid=(DEC_BATCH, nq),
        in_specs=[
            smem,
            pl.BlockSpec((BQ, 512), q_map),
            pl.BlockSpec((BQ, 1024), q_map),
            pl.BlockSpec((DEC_SEQ, 256), kv_map),
            pl.BlockSpec((DEC_SEQ, 256), kv_map),
            pl.BlockSpec((DEC_SEQ, 1024), kv_map),
            pl.BlockSpec((B_HEADS * VT_ROWS, DEC_SEQ), lambda b, j: (0, s_off + b)),
            pl.BlockSpec((None, None, PAST_LEN, 256), c_map),
            pl.BlockSpec((None, None, PAST_LEN, 256), c_map),
            pl.BlockSpec((None, None, PAST_LEN, 1024), c_map),
            pl.BlockSpec((None, None, B_HEADS * VT_ROWS, PAST_LEN), c_map),
        ],
        out_specs=pl.BlockSpec((BQ, 1024), lambda b, j: (b * nq + j, 0)),
        out_shape=jax.ShapeDtypeStruct((T_LAT, 1024), BF16),
        compiler_params=_cparams(("arbitrary", "arbitrary")),
        name="attn_latent",
    )(sink2, qa, qm, kd, vd, km, vm, kdc, vdc, kmc, vmc)
    b_map = lambda b: (b, 0)
    o_ctx = pl.pallas_call(
        _attn_ctx_kernel,
        grid=(BATCH,),
        in_specs=[
            smem,
            pl.BlockSpec((SEQ, 512), b_map),
            pl.BlockSpec((SEQ, 1024), b_map),
            pl.BlockSpec((SEQ, 256), b_map),
            pl.BlockSpec((SEQ, 256), b_map),
            pl.BlockSpec((SEQ, 1024), b_map),
            pl.BlockSpec((B_HEADS * VT_ROWS, SEQ), lambda b: (0, b)),
        ],
        out_specs=pl.BlockSpec((SEQ, 1024), b_map),
        out_shape=jax.ShapeDtypeStruct((T_CTX, 1024), BF16),
        compiler_params=_cparams(("arbitrary",)),
        name="attn_context",
    )(sink2, qa, qm, kd, vd, km, vm)
    return (o_ctx, o_lat)


def _conv_in_kernel(x_ref, mod_ref, g_ref, w_ref, u_ref, z_ref):
    for rows in _sub_rows(TOK_IN):
        h = _rms_mod(x_ref[rows, :], g_ref[...], mod_ref[0:1, :], mod_ref[1:2, :])
        y = _dot(h.astype(BF16), w_ref[...])
        a = y[:, 0:CONV_CH]
        gate = y[:, CONV_CH:2 * CONV_CH]
        u_ref[rows, :] = a * jax.nn.sigmoid(gate)
        z_ref[rows, :] = y[:, 2 * CONV_CH:]


def _conv_in(x, mod, norm_g, w_in, layer):
    tile_map = lambda i: (i, 0)
    return pl.pallas_call(
        _conv_in_kernel,
        grid=(T_ALL // TOK_IN,),
        in_specs=[
            pl.BlockSpec((TOK_IN, D_MODEL), tile_map),
            pl.BlockSpec((None, None, 6, D_MODEL), lambda i: (layer, _mod_row(i, TOK_IN), 0, 0)),
            _layer_spec((1, D_MODEL), 2 * layer),
            _layer_spec((D_MODEL, 3 * CONV_CH), layer // 2),
        ],
        out_specs=[pl.BlockSpec((TOK_IN, CONV_CH), tile_map), pl.BlockSpec((TOK_IN, POOL_CH), tile_map)],
        out_shape=[jax.ShapeDtypeStruct((T_ALL, CONV_CH), F32), jax.ShapeDtypeStruct((T_ALL, POOL_CH), F32)],
        compiler_params=_cparams(("arbitrary",)),
        name="conv_in",
    )(x, mod, norm_g, w_in)


CONV_ROWS = 32


def _conv_mix_kernel(u_ref, up_ref, un_ref, z_ref, zp_ref, zn_ref, wdw_ref, bdw_ref, lng_ref, lnb_ref,
                     band_ref, wg_ref, ps_ref, o_ref, ue_ref, ze_ref):
    i = pl.program_id(0)
    j = (i - N_CTX_TILES) % LAT_TILES_PER_SEQ
    is_lat = i >= N_CTX_TILES
    has_prev = jnp.logical_and(is_lat, j > 0)
    has_next = jnp.logical_and(is_lat, j < LAT_TILES_PER_SEQ - 1)
    seq_len = jnp.where(is_lat, DEC_SEQ, SEQ)
    t0 = jnp.where(is_lat, j * TILE, 0)

    ue_ref[0, 0:HALO, :] = jnp.where(has_prev, up_ref[...], 0.0)
    ue_ref[0, HALO:HALO + TILE, :] = u_ref[...]
    ue_ref[0, HALO + TILE:, :] = jnp.where(has_next, un_ref[...], 0.0)
    ze_ref[0:HALO, :] = jnp.where(has_prev, zp_ref[...], 0.0)
    ze_ref[HALO:HALO + TILE, :] = z_ref[...]
    ze_ref[HALO + TILE:, :] = jnp.where(has_next, zn_ref[...], 0.0)

    pad = CONV_WIDTH // 2
    reach = (HALO - pad + CONV_WIDTH - 1) // SUBLANES * SUBLANES
    groups = CONV_ROWS // SUBLANES
    n_chunks = TILE // CONV_ROWS

    def shift_rows(lo, hi):
        for b in range(1, SUBLANES):
            ue_ref[b, lo:hi, :] = ue_ref[0, lo + b:hi + b, :]

    def conv_chunk(r):
        r0 = r * CONV_ROWS
        acc = jnp.zeros((groups, SUBLANES, CONV_CH), F32) + bdw_ref[...]
        for k in range(CONV_WIDTH):
            a, b = divmod(HALO + k - pad, SUBLANES)
            e0 = r0 + a * SUBLANES
            acc = acc + ue_ref[b, e0:e0 + CONV_ROWS, :].reshape(groups, SUBLANES, CONV_CH) * wdw_ref[k]
        acc = acc.reshape(CONV_ROWS, CONV_CH)
        mu = jnp.mean(acc, axis=-1, keepdims=True)
        d = acc - mu
        var = jnp.mean(d * d, axis=-1, keepdims=True)
        yn = d * lax.rsqrt(var + EPS) * lng_ref[...] + lnb_ref[...]
        o_ref[r0:r0 + CONV_ROWS, 0:CONV_CH] = (yn * jax.nn.sigmoid(yn)).astype(BF16)

    t = t0 + lax.broadcasted_iota(jnp.int32, (TILE, POOL_GROUP_W), 0)
    ze = ze_ref[...]
    z_hi = ze.astype(BF16)
    z_lo = (ze - z_hi.astype(F32)).astype(BF16)

    def pool_group(gi):
        w = POOL_SIZES[gi]
        lo = w // 2
        hi = w - lo - 1
        cols = slice(gi * POOL_GROUP_W, (gi + 1) * POOL_GROUP_W)
        tot = _dot(band_ref[gi], z_hi[:, cols]) + _dot(band_ref[gi], z_lo[:, cols])
        cnt = jnp.minimum(t + hi + 1, seq_len) - jnp.maximum(t - lo, 0)
        d = tot / cnt.astype(F32) - ze_ref[HALO:HALO + TILE, cols]
        y = _dot(d.astype(BF16), wg_ref[gi]) * ps_ref[:, cols]
        o_ref[:, CONV_CH + gi * POOL_GROUP_W:CONV_CH + (gi + 1) * POOL_GROUP_W] = y.astype(BF16)

    shift_rows(0, reach + CONV_ROWS)
    for r in range(n_chunks):
        if r + 1 < n_chunks:
            shift_rows(reach + (r + 1) * CONV_ROWS, reach + (r + 2) * CONV_ROWS)
        conv_chunk(r)
        if r % (n_chunks // len(POOL_SIZES)) == 0:
            pool_group(r // (n_chunks // len(POOL_SIZES)))


def _conv_mix(u, z, cp, jj):
    tile_map = lambda i: (i, 0)
    hb = TILE // HALO
    prev_map = lambda i: (jnp.maximum(i * hb - 1, 0), 0)
    next_map = lambda i: (jnp.minimum((i + 1) * hb, T_ALL // HALO - 1), 0)
    return pl.pallas_call(
        _conv_mix_kernel,
        grid=(N_TILES,),
        in_specs=[
            pl.BlockSpec((TILE, CONV_CH), tile_map),
            pl.BlockSpec((HALO, CONV_CH), prev_map),
            pl.BlockSpec((HALO, CONV_CH), next_map),
            pl.BlockSpec((TILE, POOL_CH), tile_map),
            pl.BlockSpec((HALO, POOL_CH), prev_map),
            pl.BlockSpec((HALO, POOL_CH), next_map),
            _layer_spec((CONV_WIDTH, SUBLANES, CONV_CH), jj),
            _layer_spec((1, CONV_CH), jj),
            _layer_spec((1, CONV_CH), jj),
            _layer_spec((1, CONV_CH), jj),
            pl.BlockSpec((len(POOL_SIZES), TILE, TILE + 2 * HALO), lambda i: (0, 0, 0)),
            _layer_spec((len(POOL_SIZES), POOL_GROUP_W, POOL_GROUP_W), jj),
            _layer_spec((1, POOL_CH), jj),
        ],
        out_specs=pl.BlockSpec((TILE, D_MODEL), tile_map),
        out_shape=jax.ShapeDtypeStruct((T_ALL, D_MODEL), BF16),
        scratch_shapes=[pltpu.VMEM((SUBLANES, TILE + 2 * HALO, CONV_CH), F32),
                        pltpu.VMEM((TILE + 2 * HALO, POOL_CH), F32)],
        compiler_params=_cparams(("arbitrary",)),
        name="conv_mix",
    )(u, u, u, z, z, z, cp["w_dw"], cp["b_dw"], cp["ln_g"], cp["ln_b"], _pool_bands(), cp["w_grp"], cp["p_scale"])


def _pool_bands():
    t = np.arange(TILE)[:, None]
    e = np.arange(TILE + 2 * HALO)[None, :] - HALO
    bands = []
    for w in POOL_SIZES:
        lo = w // 2
        hi = w - lo - 1
        bands.append((e >= t - lo) & (e <= t + hi))
    return jnp.asarray(np.stack(bands), BF16)


def _post_kernel(*refs, n_o, n_x, final):
    o_refs, x_refs = refs[:n_o], refs[n_o:n_o + n_x]
    mod_ref, g_ref, wo_ref, w1_ref, w2_ref, fg_ref = refs[n_o + n_x:n_o + n_x + 6]
    y_refs = refs[n_o + n_x + 6:]
    subs = _sub_rows()
    x1s, hs, acts, outs = [], [], [], []
    for rows in subs:
        x1 = _tok_load(x_refs, rows) + mod_ref[2:3, :] * _dot(_tok_load(o_refs, rows), wo_ref[...])
        x1s.append(x1)
        hs.append(_rms_mod(x1, g_ref[...], mod_ref[3:4, :], mod_ref[4:5, :]).astype(BF16))
    for h in hs:
        a = jnp.maximum(_dot(h, w1_ref[...]), 0.0)
        acts.append((a * a).astype(BF16))
    for rows, x1, a in zip(subs, x1s, acts):
        x2 = x1 + mod_ref[5:6, :] * _dot(a, w2_ref[...])
        if final:
            ms = jnp.mean(x2 * x2, axis=-1, keepdims=True)
            x2 = x2 * lax.rsqrt(ms + EPS) * fg_ref[...]
        outs.append((rows, x2))
    _tok_store(y_refs, outs)


def _post(o, x, mod, norm_g, w_out, w1, w2, final_g, layer, final):
    const2 = lambda i: (0, 0)
    o_ops, o_specs = _tok_specs(o, D_MODEL)
    x_ops, x_specs = _tok_specs(x, D_MODEL)
    if final:
        y = (jax.ShapeDtypeStruct((T_CTX, D_MODEL), F32), jax.ShapeDtypeStruct((T_LAT, D_MODEL), F32))
    else:
        y = jax.ShapeDtypeStruct((T_ALL, D_MODEL), F32)
    _, y_specs = _tok_specs(y, D_MODEL)
    out = pl.pallas_call(
        functools.partial(_post_kernel, n_o=len(o_ops), n_x=len(x_ops), final=final),
        grid=(N_TOK,),
        in_specs=o_specs + x_specs + [
            pl.BlockSpec((None, None, 6, D_MODEL), lambda i: (layer, _mod_row(i), 0, 0)),
            _layer_spec((1, D_MODEL), 2 * layer + 1),
            pl.BlockSpec((None, D_MODEL, D_MODEL), lambda i: (layer // 2, 0, 0), pipeline_mode=pl.Buffered(1)),
            pl.BlockSpec((None, D_MODEL, D_FF), lambda i: (layer, 0, 0), pipeline_mode=pl.Buffered(1)),
            pl.BlockSpec((None, D_FF, D_MODEL), lambda i: (layer, 0, 0), pipeline_mode=pl.Buffered(1)),
            pl.BlockSpec((1, D_MODEL), const2),
        ],
        out_specs=y_specs if final else y_specs[0],
        out_shape=y,
        compiler_params=_cparams(("arbitrary",)),
        name="post_final" if final else "post",
    )(*o_ops, *x_ops, mod, norm_g, w_out, w1, w2, final_g)
    return out


def _pair_lanes(w):
    lead = w.shape[:-1]
    w = w.reshape(lead + (w.shape[-1] // LANES, 2, 2, HEAD_DIM // 2))
    return jnp.swapaxes(w, -3, -2).reshape(lead + (-1,))


def _mla_lanes(nope, rope, xp=jnp):
    shape = (nope if nope is not None else rope).shape[:-1]
    nope = xp.zeros(shape + (QK_NOPE,), xp.float32) if nope is None else nope
    rope = xp.zeros(shape + (QK_ROPE,), xp.float32) if rope is None else rope
    h = QK_ROPE // 2
    return xp.concatenate([rope[..., :h], nope[..., :64 - h], rope[..., h:], nope[..., 64 - h:],
                           xp.zeros(shape + (LANES - QK_NOPE - QK_ROPE,), xp.float32)], axis=-1)


def _rope_tables():
    f32 = np.float32
    n = DEC_SEQ
    rows = n // GRID_W
    row = np.repeat(np.arange(rows), GRID_W).astype(f32)
    col = np.tile(np.arange(GRID_W), rows).astype(f32)

    def angles(dim):
        quarter = dim // 4
        inv_freq = (f32(ROPE_BASE) ** (-np.arange(quarter, dtype=f32) / f32(quarter))).astype(f32)
        return np.concatenate([row[:, None] * inv_freq, col[:, None] * inv_freq], axis=-1).astype(f32)

    ang_a = angles(HEAD_DIM)
    cos_a, sin_a = np.cos(ang_a), np.sin(ang_a)
    c_a = np.concatenate([cos_a] * 4, axis=-1)
    s_a = np.concatenate([-sin_a, -sin_a, sin_a, sin_a], axis=-1)
    ang_b = angles(QK_ROPE)
    cos_b, sin_b = np.cos(ang_b), np.sin(ang_b)
    c_b = _mla_lanes(np.ones((n, QK_NOPE), f32), np.concatenate([cos_b, cos_b], axis=-1), np)
    s_b = _mla_lanes(None, np.concatenate([-sin_b, sin_b], axis=-1), np)
    lat = np.stack([c_a, s_a, c_b, s_b])
    ident = np.stack([np.ones((TOK_IN, LANES), f32), np.zeros((TOK_IN, LANES), f32)] * 2)
    return jnp.asarray(np.concatenate([ident, lat], axis=1).astype(f32))


def _prep_even(attn_w_in, mla_q_norm, mla_kv_norm, mla_w_qb, mla_w_kvb, attn_w_out):
    w = attn_w_in
    o = np.cumsum((0, 512, 128, 128, Q_LORA, KV_LORA, QK_ROPE))
    zeros = lambda n: jnp.zeros((N_EVEN, D_MODEL, n), F32)
    w_in = jnp.concatenate([
        _pair_lanes(w[..., o[0]:o[1]]), _pair_lanes(w[..., o[1]:o[2]]), w[..., o[2]:o[3]],
        w[..., o[3]:o[4]], zeros(Q_LORA_PAD - Q_LORA),
        w[..., o[4]:o[5]],
        _mla_lanes(None, w[..., o[5]:o[6]]),
    ], axis=-1).astype(BF16)
    q_norm = jnp.pad(mla_q_norm, ((0, 0), (0, Q_LORA_PAD - Q_LORA))).reshape(N_EVEN, 1, Q_LORA_PAD)
    wqb = mla_w_qb.reshape(N_EVEN, Q_LORA, B_HEADS, QK_NOPE + QK_ROPE)
    wqb = jnp.pad(_mla_lanes(wqb[..., :QK_NOPE], wqb[..., QK_NOPE:]),
                  ((0, 0), (0, Q_LORA_PAD - Q_LORA), (0, 0), (0, 0)))
    wkvb = mla_w_kvb.reshape(N_EVEN, KV_LORA, B_HEADS, QK_NOPE + V_DIM)
    w_kk = _mla_lanes(wkvb[..., :QK_NOPE], None)
    w_kv = wkvb[..., QK_NOPE:].reshape(N_EVEN, KV_LORA, B_HEADS * V_DIM)
    return {
        "w_in": w_in,
        "q_norm": q_norm,
        "kv_norm": mla_kv_norm.reshape(N_EVEN, 1, KV_LORA),
        "w_qb": wqb.reshape(N_EVEN, Q_LORA_PAD, B_HEADS * LANES).astype(BF16),
        "w_kk": w_kk.reshape(N_EVEN, KV_LORA, B_HEADS * LANES).astype(BF16),
        "w_kv": jnp.swapaxes(w_kv, 1, 2).astype(BF16),
        "w_out": attn_w_out.astype(BF16),
    }


def kernel(x_prompt, x_sample, cache_win_k, cache_win_v, cache_mla_ckv, cache_mla_krope, c, c_ctx, w_mod, b_mod,
           norm_g, attn_w_in, attn_sink, mla_q_norm, mla_kv_norm, mla_w_qb, mla_w_kvb, attn_w_out, conv_w_in,
           conv_dw, conv_dw_b, conv_ln_g, conv_ln_b, pool_w, pool_scale, conv_w_out, mlp_w1, mlp_w2, final_g):
    x = (x_prompt.reshape(T_CTX, D_MODEL), x_sample.reshape(T_LAT, D_MODEL))

    cond = jnp.concatenate([c_ctx[None, :], c, jnp.zeros((N_COND - 1 - DEC_BATCH, D_MODEL), F32)], axis=0)
    mod = _mod_table(cond, w_mod, b_mod).reshape(DEPTH, N_COND, 6, D_MODEL)

    rope = _rope_tables()
    even = _prep_even(attn_w_in, mla_q_norm, mla_kv_norm, mla_w_qb, mla_w_kvb, attn_w_out)
    kdc, vdc, kmc, vmc = _ctx_kv(
        _pair_lanes(cache_win_k.reshape(DEC_BATCH, N_EVEN, PAST_LEN, LANES)),
        cache_win_v.reshape(DEC_BATCH, N_EVEN, PAST_LEN, LANES),
        cache_mla_ckv, _mla_lanes(None, cache_mla_krope), even["w_kk"], even["w_kv"])
    odd = {
        "w_dw": jnp.broadcast_to(conv_dw[:, :, None, :], (N_ODD, CONV_WIDTH, SUBLANES, CONV_CH)),
        "b_dw": conv_dw_b.reshape(N_ODD, 1, CONV_CH),
        "ln_g": conv_ln_g.reshape(N_ODD, 1, CONV_CH),
        "ln_b": conv_ln_b.reshape(N_ODD, 1, CONV_CH),
        "w_grp": pool_w.astype(BF16),
        "p_scale": pool_scale.reshape(N_ODD, 1, POOL_CH),
    }
    conv_w_in_b, conv_w_out_b = conv_w_in.astype(BF16), conv_w_out.astype(BF16)
    w1_all, w2_all = mlp_w1.astype(BF16), mlp_w2.astype(BF16)
    gains = norm_g.reshape(2 * DEPTH, 1, D_MODEL)
    final_g2 = final_g.reshape(1, D_MODEL)
    sink2 = attn_sink * LOG2E

    caches = []
    for l in range(DEPTH):
        if l % 2 == 0:
            i = l // 2
            qa, kd, vd, qm, km, vm, ka32, va32, ckv32, kr32 = _attn_in(x, mod, gains, even, rope, l)
            caches.append((ka32, va32, ckv32, kr32))
            o = _attention(sink2[i], qa, kd, vd, qm, km, vm, kdc, vdc, kmc, vmc, i)
            w_out = even["w_out"]
        else:
            u, z = _conv_in(x, mod, gains, conv_w_in_b, l)
            o = _conv_mix(u, z, odd, l // 2)
            w_out = conv_w_out_b
        x = _post(o, x, mod, gains, w_out, w1_all, w2_all, final_g2, l, final=(l == DEPTH - 1))

    ka, va, ckv, kr = (jnp.stack(t, axis=1) for t in zip(*caches))
    new_k = jnp.swapaxes(ka.reshape(BATCH, SEQ, N_EVEN, 2, A_KV_HEADS, HEAD_DIM // 2), 3, 4)
    new_k = jnp.swapaxes(new_k.reshape(BATCH, SEQ, N_EVEN, A_KV_HEADS, HEAD_DIM), 1, 2)
    new_v = jnp.swapaxes(va.reshape(BATCH, SEQ, N_EVEN, A_KV_HEADS, HEAD_DIM), 1, 2)
    new_ckv = jnp.swapaxes(ckv.reshape(BATCH, SEQ, N_EVEN, KV_LORA), 1, 2)
    new_kr = jnp.concatenate([kr[..., :QK_ROPE // 2], kr[..., 64:64 + QK_ROPE // 2]], axis=-1)
    new_kr = jnp.swapaxes(new_kr.reshape(BATCH, SEQ, N_EVEN, QK_ROPE), 1, 2)
    y_prompt = x[0].reshape(BATCH, SEQ, D_MODEL)
    y_sample = x[1].reshape(DEC_BATCH, DEC_SEQ, D_MODEL)
    return (y_prompt, y_sample, new_k, new_v, new_ckv, new_kr)
```

```python
import functools
import math

import numpy as np
import jax
import jax.numpy as jnp
from jax import lax
from jax.experimental import pallas as pl
from jax.experimental.pallas import tpu as pltpu

F32 = jnp.float32
BF16 = jnp.bfloat16

D_MODEL = 1024
BATCH = 16
SEQ = 256
DEPTH = 4
DEC_BATCH = 8
DEC_SEQ = 2048
PAST_LEN = 256
GRID_W = 64
N_EVEN = (DEPTH + 1) // 2
N_ODD = DEPTH // 2
A_HEADS = 8
A_KV_HEADS = 2
A_GROUP = A_HEADS // A_KV_HEADS
HEAD_DIM = 64
WINDOW = 128
B_HEADS = 8
Q_LORA = 192
KV_LORA = 128
QK_NOPE = 64
QK_ROPE = 32
V_DIM = 64
MLA_SCALE = (QK_NOPE + QK_ROPE) ** -0.5
CONV_CH = D_MODEL // 2
CONV_WIDTH = 31
POOL_CH = D_MODEL // 2
POOL_SIZES = (2, 4, 8, 16)
POOL_GROUP_W = POOL_CH // len(POOL_SIZES)
D_FF = 4 * D_MODEL
ROPE_BASE = 10000.0
EPS = 1e-6
NEG_INF = -1e30
LOG2E = math.log2(math.e)

LANES = 128
SUBLANES = 8
VMEM_LIMIT_BYTES = 56 * 1024 * 1024

TILE = 256
T_CTX = BATCH * SEQ
T_LAT = DEC_BATCH * DEC_SEQ
T_ALL = T_CTX + T_LAT
N_CTX_TILES = T_CTX // TILE
N_TILES = T_ALL // TILE
LAT_TILES_PER_SEQ = DEC_SEQ // TILE
TOK = 512
N_CTX_TOK = T_CTX // TOK
N_TOK = T_ALL // TOK
TOK_IN = 1024
N_COND = 16
BQ = 256
VT_ROWS = 80
HALO = 16
Q_LORA_PAD = 256
ATTN_IN_COLS = 512 + 128 + 128 + Q_LORA_PAD + 128 + 128


def _cparams(sem):
    return pltpu.CompilerParams(dimension_semantics=sem, vmem_limit_bytes=VMEM_LIMIT_BYTES)


def _mod_row(i, tok=TOK):
    return jnp.where(i < T_CTX // tok, 0, 1 + (i - T_CTX // tok) // (DEC_SEQ // tok))


def _pos_block(i, tok):
    return jnp.where(i < T_CTX // tok, 0, 1 + (i - T_CTX // tok) % (DEC_SEQ // tok))


def _tok_specs(a, width, tok=TOK):
    if isinstance(a, tuple):
        return list(a), [pl.BlockSpec((tok, width), lambda i: (jnp.minimum(i, T_CTX // tok - 1), 0)),
                         pl.BlockSpec((tok, width), lambda i: (jnp.maximum(i - T_CTX // tok, 0), 0))]
    return [a], [pl.BlockSpec((tok, width), lambda i: (i, 0))]


def _tok_load(refs, rows, tok=TOK):
    if len(refs) == 1:
        return refs[0][rows, :]
    return jnp.where(pl.program_id(0) < T_CTX // tok, refs[0][rows, :], refs[1][rows, :])


def _tok_store(refs, parts):
    def put(ref):
        for rows, val in parts:
            ref[rows, :] = val

    if len(refs) == 1:
        put(refs[0])
        return
    i = pl.program_id(0)

    @pl.when(i < N_CTX_TOK)
    def _():
        put(refs[0])

    @pl.when(i >= N_CTX_TOK)
    def _():
        put(refs[1])


def _layer_spec(shape, idx):
    zeros = (0,) * len(shape)
    return pl.BlockSpec((None,) + tuple(shape), lambda *_: (idx,) + zeros)


def _sub_rows(tok=TOK):
    return [slice(s * TILE, (s + 1) * TILE) for s in range(tok // TILE)]


def _dot(a, b):
    return jnp.dot(a, b, preferred_element_type=F32)


def _dot_nt(a, b):
    return lax.dot_general(a, b, (((1,), (1,)), ((), ())), preferred_element_type=F32)


def _rms_mod(x, g, shift, scale):
    ms = jnp.mean(x * x, axis=-1, keepdims=True)
    return x * lax.rsqrt(ms + EPS) * (g * (1.0 + scale)) + shift


def _rope(x, c, s):
    return x * c + pltpu.roll(x, 64, 1) * s


def _lane_lt64(shape):
    return lax.broadcasted_iota(jnp.int32, shape, len(shape) - 1) < 64


def _lane_even32(shape):
    return (lax.broadcasted_iota(jnp.int32, shape, len(shape) - 1) & 63) < 32


def _store_kv_dup(kd_ref, vd_ref, rows, k, v):
    even32 = _lane_even32(k.shape)
    lt64 = _lane_lt64(v.shape)
    kd_ref[rows, 0:LANES] = jnp.where(even32, k, pltpu.roll(k, 32, 1)).astype(BF16)
    kd_ref[rows, LANES:2 * LANES] = jnp.where(even32, pltpu.roll(k, LANES - 32, 1), k).astype(BF16)
    vd_ref[rows, 0:LANES] = jnp.where(lt64, v, 1.0).astype(BF16)
    vd_ref[rows, LANES:2 * LANES] = jnp.where(lt64, pltpu.roll(v, 64, 1), 1.0).astype(BF16)


def _store_vt(vt_ref, cols, ckvn, wkvt_ref):
    vt = _dot(wkvt_ref[...], ckvn.T.astype(BF16))
    ones = jnp.ones((VT_ROWS - V_DIM, ckvn.shape[0]), BF16)
    for hh in range(B_HEADS):
        vt_ref[hh * VT_ROWS:hh * VT_ROWS + V_DIM, cols] = vt[hh * V_DIM:(hh + 1) * V_DIM].astype(BF16)
        vt_ref[hh * VT_ROWS + V_DIM:(hh + 1) * VT_ROWS, cols] = ones


def _mod_kernel(cond_ref, w_ref, b_ref, o_ref):
    c = cond_ref[...]
    s = c * jax.nn.sigmoid(c)
    o_ref[...] = _dot(s.astype(BF16), w_ref[...].astype(BF16)) + b_ref[...]


def _mod_table(cond, w_mod, b_mod):
    nb = 6 * D_MODEL // 1024
    return pl.pallas_call(
        _mod_kernel,
        grid=(DEPTH, nb),
        in_specs=[
            pl.BlockSpec((N_COND, D_MODEL), lambda l, n: (0, 0)),
            pl.BlockSpec((None, D_MODEL, 1024), lambda l, n: (l, 0, n)),
            pl.BlockSpec((None, 1, 1024), lambda l, n: (l, 0, n)),
        ],
        out_specs=pl.BlockSpec((None, N_COND, 1024), lambda l, n: (l, 0, n)),
        out_shape=jax.ShapeDtypeStruct((DEPTH, N_COND, 6 * D_MODEL), F32),
        compiler_params=_cparams(("arbitrary", "arbitrary")),
        name="mod_table",
    )(cond, w_mod, b_mod.reshape(DEPTH, 1, 6 * D_MODEL))


def _attn_in_kernel(*refs, n_x):
    x_refs = refs[:n_x]
    (mod_ref, g_ref, w_ref, qn_ref, kvn_ref, wqb_ref, wkk_ref, wkv_ref, rope_ref,
     qa_ref, kd_ref, vd_ref, qm_ref, km_ref, vm_ref, ka32_ref, va32_ref, ckv32_ref, kr32_ref) = refs[n_x:]
    col = lambda n: slice(n * LANES, (n + 1) * LANES)

    def project(rows):
        h = _rms_mod(_tok_load(x_refs, rows, TOK_IN), g_ref[...], mod_ref[0:1, :], mod_ref[1:2, :])
        return _dot(h.astype(BF16), w_ref[...])

    def derive(rows, y):
        ca, sa = rope_ref[0, rows, :], rope_ref[1, rows, :]
        cb, sb = rope_ref[2, rows, :], rope_ref[3, rows, :]

        for c in range(4):
            q = _rope(y[:, col(c)], ca, sa)
            qa_ref[rows, col(c)] = (q * (HEAD_DIM ** -0.5 * LOG2E)).astype(BF16)

        ka = _rope(y[:, 512:640], ca, sa)
        va = y[:, 640:768]
        _store_kv_dup(kd_ref, vd_ref, rows, ka, va)

        cq = y[:, 768:768 + Q_LORA_PAD]
        cqn = cq * lax.rsqrt(jnp.sum(cq * cq, axis=-1, keepdims=True) * (1.0 / Q_LORA) + EPS) * qn_ref[...]
        qm = _dot(cqn.astype(BF16), wqb_ref[...])
        for hh in range(B_HEADS):
            q = _rope(qm[:, col(hh)], cb, sb)
            qm_ref[rows, col(hh)] = (q * (MLA_SCALE * LOG2E)).astype(BF16)

        ckv = y[:, 1024:1152]
        ckvn = ckv * lax.rsqrt(jnp.mean(ckv * ckv, axis=-1, keepdims=True) + EPS) * kvn_ref[...]
        kr = y[:, 1152:1280]
        ckvn_b = ckvn.astype(BF16)
        kn = _dot(ckvn_b, wkk_ref[...])
        krr = _rope(kr, cb, sb)
        for hh in range(B_HEADS):
            km_ref[rows, col(hh)] = (kn[:, col(hh)] + krr).astype(BF16)
        _store_vt(vm_ref, rows, ckvn, wkv_ref)
        return ka, va, ckvn, kr

    subs = _sub_rows(TOK_IN)
    cache = []
    y = project(subs[0])
    for s, rows in enumerate(subs):
        y_next = project(subs[s + 1]) if s + 1 < len(subs) else None
        cache.append(derive(rows, y))
        y = y_next

    @pl.when(pl.program_id(0) < T_CTX // TOK_IN)
    def _():
        for rows, (ka, va, ckvn, kr) in zip(subs, cache):
            ka32_ref[rows, :] = ka
            va32_ref[rows, :] = va
            ckv32_ref[rows, :] = ckvn
            kr32_ref[rows, :] = kr


def _attn_in(x, mod, norm_g, wp, rope, layer):
    tile_map = lambda i: (i, 0)
    li = layer // 2
    ctx_map = lambda i: (jnp.minimum(i, T_CTX // TOK_IN - 1), 0)
    bf = lambda w: jax.ShapeDtypeStruct((T_ALL, w), BF16)
    c32 = jax.ShapeDtypeStruct((T_CTX, LANES), F32)
    x_ops, x_specs = _tok_specs(x, D_MODEL, TOK_IN)
    return pl.pallas_call(
        functools.partial(_attn_in_kernel, n_x=len(x_ops)),
        grid=(T_ALL // TOK_IN,),
        in_specs=x_specs + [
            pl.BlockSpec((None, None, 6, D_MODEL), lambda i: (layer, _mod_row(i, TOK_IN), 0, 0)),
            _layer_spec((1, D_MODEL), 2 * layer),
            _layer_spec((D_MODEL, ATTN_IN_COLS), li),
            _layer_spec((1, Q_LORA_PAD), li),
            _layer_spec((1, KV_LORA), li),
            _layer_spec((Q_LORA_PAD, B_HEADS * LANES), li),
            _layer_spec((KV_LORA, B_HEADS * LANES), li),
            _layer_spec((B_HEADS * V_DIM, KV_LORA), li),
            pl.BlockSpec((4, TOK_IN, LANES), lambda i: (0, _pos_block(i, TOK_IN), 0)),
        ],
        out_specs=[
            pl.BlockSpec((TOK_IN, 512), tile_map),
            pl.BlockSpec((TOK_IN, 256), tile_map),
            pl.BlockSpec((TOK_IN, 256), tile_map),
            pl.BlockSpec((TOK_IN, 1024), tile_map),
            pl.BlockSpec((TOK_IN, 1024), tile_map),
            pl.BlockSpec((B_HEADS * VT_ROWS, TOK_IN), lambda i: (0, i)),
            pl.BlockSpec((TOK_IN, LANES), ctx_map),
            pl.BlockSpec((TOK_IN, LANES), ctx_map),
            pl.BlockSpec((TOK_IN, LANES), ctx_map),
            pl.BlockSpec((TOK_IN, LANES), ctx_map),
        ],
        out_shape=[bf(512), bf(256), bf(256), bf(1024), bf(1024),
                   jax.ShapeDtypeStruct((B_HEADS * VT_ROWS, T_ALL), BF16), c32, c32, c32, c32],
        compiler_params=_cparams(("arbitrary",)),
        name="attn_in",
    )(*x_ops, mod, norm_g, wp["w_in"], wp["q_norm"], wp["kv_norm"], wp["w_qb"], wp["w_kk"], wp["w_kv"], rope)


def _ctx_kv_kernel(ck_ref, cv_ref, cckv_ref, ckr_ref, wkk_ref, wkv_ref, kd_ref, vd_ref, km_ref, vm_ref):
    _store_kv_dup(kd_ref, vd_ref, slice(None), ck_ref[...], cv_ref[...])
    cb = cckv_ref[...].astype(BF16)
    kn = _dot(cb, wkk_ref[...])
    kr = ckr_ref[...]
    for hh in range(B_HEADS):
        km_ref[:, hh * LANES:(hh + 1) * LANES] = (kn[:, hh * LANES:(hh + 1) * LANES] + kr).astype(BF16)
    _store_vt(vm_ref, slice(None), cckv_ref[...], wkv_ref)


def _ctx_kv(cache_k, cache_v, cache_ckv, cache_kr128, w_kk, w_kv):
    cache_map = lambda i, b: (b, i, 0, 0)
    w_map = lambda i, b: (i, 0, 0)
    out_map = lambda i, b: (i, b, 0, 0)
    o = lambda w: jax.ShapeDtypeStruct((N_EVEN, DEC_BATCH, PAST_LEN, w), BF16)
    return pl.pallas_call(
        _ctx_kv_kernel,
        grid=(N_EVEN, DEC_BATCH),
        in_specs=[
            pl.BlockSpec((None, None, PAST_LEN, LANES), cache_map),
            pl.BlockSpec((None, None, PAST_LEN, LANES), cache_map),
            pl.BlockSpec((None, None, PAST_LEN, KV_LORA), cache_map),
            pl.BlockSpec((None, None, PAST_LEN, LANES), cache_map),
            pl.BlockSpec((None, KV_LORA, B_HEADS * LANES), w_map),
            pl.BlockSpec((None, B_HEADS * V_DIM, KV_LORA), w_map),
        ],
        out_specs=[
            pl.BlockSpec((None, None, PAST_LEN, 256), out_map),
            pl.BlockSpec((None, None, PAST_LEN, 256), out_map),
            pl.BlockSpec((None, None, PAST_LEN, 1024), out_map),
            pl.BlockSpec((None, None, B_HEADS * VT_ROWS, PAST_LEN), out_map),
        ],
        out_shape=[o(256), o(256), o(1024),
                   jax.ShapeDtypeStruct((N_EVEN, DEC_BATCH, B_HEADS * VT_ROWS, PAST_LEN), BF16)],
        compiler_params=_cparams(("arbitrary", "arbitrary")),
        name="ctx_kv",
    )(cache_k, cache_v, cache_ckv, cache_kr128, w_kk, w_kv)


def _attn_heads(qa_ref, qm_ref, a_segs, m_segs, sink_ref, o_ref, rows, depth):
    lt64 = _lane_lt64((rows, LANES))
    lane = lax.broadcasted_iota(jnp.int32, (1, LANES), 1)
    keep = (((lane & 63) < 32).astype(BF16), ((lane & 63) >= 32).astype(BF16))
    col = lambda n: slice(n * LANES, (n + 1) * LANES)
    swap = lambda r: pltpu.roll(r, 64, 1)

    jobs = []
    for kh in range(A_KV_HEADS):
        heads = [dict(out=(kh * A_GROUP + g) // 2, half=g % 2, sink=kh * A_GROUP + g) for g in range(A_GROUP)]
        jobs.append(dict(
            heads=heads,
            q=lambda heads=heads: jnp.concatenate(
                [qa_ref[:, col(h["out"])] * keep[h["half"]] for h in heads], axis=0),
            ks=[lambda kd=kd, rs=rs, kh=kh: kd[rs, col(kh)] for kd, _, rs, _ in a_segs],
            vs=[lambda vd=vd, rs=rs, kh=kh: vd[rs, col(kh)] for _, vd, rs, _ in a_segs],
            valids=[valid for _, _, _, valid in a_segs]))
    for hh in range(B_HEADS):
        jobs.append(dict(
            keys_on_rows=True,
            heads=[dict(out=4 + hh // 2, half=hh % 2)],
            q=lambda hh=hh: qm_ref[:, col(hh)],
            ks=[lambda km=km, hh=hh: km[:, col(hh)] for km, _ in m_segs],
            vs=[lambda vmt=vmt, hh=hh: vmt[hh * VT_ROWS:(hh + 1) * VT_ROWS, :] for _, vmt in m_segs]))

    def scores(job):
        if job.get("keys_on_rows"):
            h = job["heads"][0]
            q = job["q"]()
            h["ss"] = [_dot_nt(k(), q) for k in job["ks"]]
            m = h["ss"][0].max(axis=0, keepdims=True)
            for s in h["ss"][1:]:
                m = jnp.maximum(m, s.max(axis=0, keepdims=True))
            h["m"] = m
            return
        full = [_dot_nt(job["q"](), k()) for k in job["ks"]]
        for b, h in enumerate(job["heads"]):
            blk = slice(b * rows, (b + 1) * rows)
            ss = [s[blk] if valid is None else jnp.where(valid, s[blk], NEG_INF)
                  for s, valid in zip(full, job["valids"])]
            m = ss[0].max(axis=-1, keepdims=True)
            for s in ss[1:]:
                m = jnp.maximum(m, s.max(axis=-1, keepdims=True))
            sink = sink_ref[h["sink"]]
            m = jnp.maximum(m, sink)
            h["e"] = jnp.exp2(sink - m)
            h["ss"], h["m"] = ss, m

    def values(job):
        if job.get("keys_on_rows"):
            h = job["heads"][0]
            r = None
            for s, vt in zip(h.pop("ss"), job["vs"]):
                rs_ = _dot(vt(), jnp.exp2(s - h["m"]).astype(BF16))
                r = rs_ if r is None else r + rs_
            del h["m"]
            h["r"] = r
            return
        r = None
        for i, v in enumerate(job["vs"]):
            p = [jnp.exp2(h["ss"][i] - h["m"]).astype(BF16) for h in job["heads"]]
            rs_ = _dot(jnp.concatenate(p, axis=0), v())
            r = rs_ if r is None else r + rs_
        for b, h in enumerate(job["heads"]):
            del h["ss"], h["m"]
            h["r"] = r[b * rows:(b + 1) * rows]

    def finish(job, done):
        if job.get("keys_on_rows"):
            h = job["heads"][0]
            r = h.pop("r")
            o_t = r[0:V_DIM] / r[V_DIM:V_DIM + 1]
            other = done.pop(h["out"], None)
            if other is None:
                done[h["out"]] = o_t
            else:
                lo, hi = (other, o_t) if h["half"] == 1 else (o_t, other)
                o_ref[:, col(h["out"])] = jnp.concatenate([lo, hi], axis=0).T.astype(BF16)
            return
        for h in job["heads"]:
            r = h.pop("r")
            if h["half"] == 1:
                o = swap(r) / (r + h["e"])
            else:
                o = r / (swap(r) + h["e"])
            other = done.pop(h["out"], None)
            if other is None:
                done[h["out"]] = o
            else:
                lo, hi = (other, o) if h["half"] == 1 else (o, other)
                o_ref[:, col(h["out"])] = jnp.where(lt64, lo, hi).astype(BF16)

    done = {}
    for t in range(len(jobs) + depth):
        if t < len(jobs):
            scores(jobs[t])
        if t >= depth:
            values(jobs[t - depth])
            finish(jobs[t - depth], done)


def _attn_lat_kernel(sink_ref, qa_ref, qm_ref, kd_ref, vd_ref, km_ref, vm_ref,
                     kdc_ref, vdc_ref, kmc_ref, vmc_ref, o_ref):
    j = pl.program_id(1)
    q0 = j * BQ
    nloc = BQ + 2 * WINDOW
    start = pl.multiple_of(jnp.clip(q0 - WINDOW, 0, DEC_SEQ - nloc), WINDOW)
    qpos = q0 + lax.broadcasted_iota(jnp.int32, (BQ, nloc), 0)
    kpos = start + lax.broadcasted_iota(jnp.int32, (BQ, nloc), 1)
    valid = jnp.abs(qpos - kpos) <= WINDOW
    everything = slice(None)
    a_segs = [(kd_ref, vd_ref, pl.ds(start, nloc), valid), (kdc_ref, vdc_ref, everything, None)]
    m_segs = [(km_ref, vm_ref), (kmc_ref, vmc_ref)]
    _attn_heads(qa_ref, qm_ref, a_segs, m_segs, sink_ref, o_ref, BQ, depth=3)


def _attn_ctx_kernel(sink_ref, qa_ref, qm_ref, kd_ref, vd_ref, km_ref, vm_ref, o_ref):
    a_segs = [(kd_ref, vd_ref, slice(None), None)]
    m_segs = [(km_ref, vm_ref)]
    _attn_heads(qa_ref, qm_ref, a_segs, m_segs, sink_ref, o_ref, SEQ, depth=3)


def _attention(sink2, qa, kd, vd, qm, km, vm, kdc, vdc, kmc, vmc, layer_i):
    smem = pl.BlockSpec(memory_space=pltpu.SMEM)
    nq = DEC_SEQ // BQ
    q_off = T_CTX // BQ
    s_off = T_CTX // DEC_SEQ
    q_map = lambda b, j: (q_off + b * nq + j, 0)
    kv_map = lambda b, j: (s_off + b, 0)
    c_map = lambda b, j: (layer_i, b, 0, 0)
    o_lat = pl.pallas_call(
        _attn_lat_kernel,
        grid=(DEC_BATCH, nq),
        in_specs=[
            smem,
            pl.BlockSpec((BQ, 512), q_map),
            pl.BlockSpec((BQ, 1024), q_map),
            pl.BlockSpec((DEC_SEQ, 256), kv_map),
            pl.BlockSpec((DEC_SEQ, 256), kv_map),
            pl.BlockSpec((DEC_SEQ, 1024), kv_map),
            pl.BlockSpec((B_HEADS * VT_ROWS, DEC_SEQ), lambda b, j: (0, s_off + b)),
            pl.BlockSpec((None, None, PAST_LEN, 256), c_map),
            pl.BlockSpec((None, None, PAST_LEN, 256), c_map),
            pl.BlockSpec((None, None, PAST_LEN, 1024), c_map),
            pl.BlockSpec((None, None, B_HEADS * VT_ROWS, PAST_LEN), c_map),
        ],
        out_specs=pl.BlockSpec((BQ, 1024), lambda b, j: (b * nq + j, 0)),
        out_shape=jax.ShapeDtypeStruct((T_LAT, 1024), BF16),
        compiler_params=_cparams(("arbitrary", "arbitrary")),
        name="attn_latent",
    )(sink2, qa, qm, kd, vd, km, vm, kdc, vdc, kmc, vmc)
    b_map = lambda b: (b, 0)
    o_ctx = pl.pallas_call(
        _attn_ctx_kernel,
        grid=(BATCH,),
        in_specs=[
            smem,
            pl.BlockSpec((SEQ, 512), b_map),
            pl.BlockSpec((SEQ, 1024), b_map),
            pl.BlockSpec((SEQ, 256), b_map),
            pl.BlockSpec((SEQ, 256), b_map),
            pl.BlockSpec((SEQ, 1024), b_map),
            pl.BlockSpec((B_HEADS * VT_ROWS, SEQ), lambda b: (0, b)),
        ],
        out_specs=pl.BlockSpec((SEQ, 1024), b_map),
        out_shape=jax.ShapeDtypeStruct((T_CTX, 1024), BF16),
        compiler_params=_cparams(("arbitrary",)),
        name="attn_context",
    )(sink2, qa, qm, kd, vd, km, vm)
    return (o_ctx, o_lat)


def _conv_in_kernel(x_ref, mod_ref, g_ref, w_ref, u_ref, z_ref):
    for rows in _sub_rows(TOK_IN):
        h = _rms_mod(x_ref[rows, :], g_ref[...], mod_ref[0:1, :], mod_ref[1:2, :])
        y = _dot(h.astype(BF16), w_ref[...])
        a = y[:, 0:CONV_CH]
        gate = y[:, CONV_CH:2 * CONV_CH]
        u_ref[rows, :] = a * jax.nn.sigmoid(gate)
        z_ref[rows, :] = y[:, 2 * CONV_CH:]


def _conv_in(x, mod, norm_g, w_in, layer):
    tile_map = lambda i: (i, 0)
    return pl.pallas_call(
        _conv_in_kernel,
        grid=(T_ALL // TOK_IN,),
        in_specs=[
            pl.BlockSpec((TOK_IN, D_MODEL), tile_map),
            pl.BlockSpec((None, None, 6, D_MODEL), lambda i: (layer, _mod_row(i, TOK_IN), 0, 0)),
            _layer_spec((1, D_MODEL), 2 * layer),
            _layer_spec((D_MODEL, 3 * CONV_CH), layer // 2),
        ],
        out_specs=[pl.BlockSpec((TOK_IN, CONV_CH), tile_map), pl.BlockSpec((TOK_IN, POOL_CH), tile_map)],
        out_shape=[jax.ShapeDtypeStruct((T_ALL, CONV_CH), F32), jax.ShapeDtypeStruct((T_ALL, POOL_CH), F32)],
        compiler_params=_cparams(("arbitrary",)),
        name="conv_in",
    )(x, mod, norm_g, w_in)


CONV_ROWS = 32


def _conv_mix_kernel(u_ref, up_ref, un_ref, z_ref, zp_ref, zn_ref, wdw_ref, bdw_ref, lng_ref, lnb_ref,
                     band_ref, wg_ref, ps_ref, o_ref, ue_ref, ze_ref):
    i = pl.program_id(0)
    j = (i - N_CTX_TILES) % LAT_TILES_PER_SEQ
    is_lat = i >= N_CTX_TILES
    has_prev = jnp.logical_and(is_lat, j > 0)
    has_next = jnp.logical_and(is_lat, j < LAT_TILES_PER_SEQ - 1)
    seq_len = jnp.where(is_lat, DEC_SEQ, SEQ)
    t0 = jnp.where(is_lat, j * TILE, 0)

    ue_ref[0, 0:HALO, :] = jnp.where(has_prev, up_ref[...], 0.0)
    ue_ref[0, HALO:HALO + TILE, :] = u_ref[...]
    ue_ref[0, HALO + TILE:, :] = jnp.where(has_next, un_ref[...], 0.0)
    ze_ref[0:HALO, :] = jnp.where(has_prev, zp_ref[...], 0.0)
    ze_ref[HALO:HALO + TILE, :] = z_ref[...]
    ze_ref[HALO + TILE:, :] = jnp.where(has_next, zn_ref[...], 0.0)

    pad = CONV_WIDTH // 2
    reach = (HALO - pad + CONV_WIDTH - 1) // SUBLANES * SUBLANES
    groups = CONV_ROWS // SUBLANES
    n_chunks = TILE // CONV_ROWS

    def shift_rows(lo, hi):
        for b in range(1, SUBLANES):
            ue_ref[b, lo:hi, :] = ue_ref[0, lo + b:hi + b, :]

    def conv_chunk(r):
        r0 = r * CONV_ROWS
        acc = jnp.zeros((groups, SUBLANES, CONV_CH), F32) + bdw_ref[...]
        for k in range(CONV_WIDTH):
            a, b = divmod(HALO + k - pad, SUBLANES)
            e0 = r0 + a * SUBLANES
            acc = acc + ue_ref[b, e0:e0 + CONV_ROWS, :].reshape(groups, SUBLANES, CONV_CH) * wdw_ref[k]
        acc = acc.reshape(CONV_ROWS, CONV_CH)
        mu = jnp.mean(acc, axis=-1, keepdims=True)
        d = acc - mu
        var = jnp.mean(d * d, axis=-1, keepdims=True)
        yn = d * lax.rsqrt(var + EPS) * lng_ref[...] + lnb_ref[...]
        o_ref[r0:r0 + CONV_ROWS, 0:CONV_CH] = (yn * jax.nn.sigmoid(yn)).astype(BF16)

    t = t0 + lax.broadcasted_iota(jnp.int32, (TILE, POOL_GROUP_W), 0)
    ze = ze_ref[...]
    z_hi = ze.astype(BF16)
    z_lo = (ze - z_hi.astype(F32)).astype(BF16)

    def pool_group(gi):
        w = POOL_SIZES[gi]
        lo = w // 2
        hi = w - lo - 1
        cols = slice(gi * POOL_GROUP_W, (gi + 1) * POOL_GROUP_W)
        tot = _dot(band_ref[gi], z_hi[:, cols]) + _dot(band_ref[gi], z_lo[:, cols])
        cnt = jnp.minimum(t + hi + 1, seq_len) - jnp.maximum(t - lo, 0)
        d = tot / cnt.astype(F32) - ze_ref[HALO:HALO + TILE, cols]
        y = _dot(d.astype(BF16), wg_ref[gi]) * ps_ref[:, cols]
        o_ref[:, CONV_CH + gi * POOL_GROUP_W:CONV_CH + (gi + 1) * POOL_GROUP_W] = y.astype(BF16)

    shift_rows(0, reach + CONV_ROWS)
    for r in range(n_chunks):
        if r + 1 < n_chunks:
            shift_rows(reach + (r + 1) * CONV_ROWS, reach + (r + 2) * CONV_ROWS)
        conv_chunk(r)
        if r % (n_chunks // len(POOL_SIZES)) == 0:
            pool_group(r // (n_chunks // len(POOL_SIZES)))


def _conv_mix(u, z, cp, jj):
    tile_map = lambda i: (i, 0)
    hb = TILE // HALO
    prev_map = lambda i: (jnp.maximum(i * hb - 1, 0), 0)
    next_map = lambda i: (jnp.minimum((i + 1) * hb, T_ALL // HALO - 1), 0)
    return pl.pallas_call(
        _conv_mix_kernel,
        grid=(N_TILES,),
        in_specs=[
            pl.BlockSpec((TILE, CONV_CH), tile_map),
            pl.BlockSpec((HALO, CONV_CH), prev_map),
            pl.BlockSpec((HALO, CONV_CH), next_map),
            pl.BlockSpec((TILE, POOL_CH), tile_map),
            pl.BlockSpec((HALO, POOL_CH), prev_map),
            pl.BlockSpec((HALO, POOL_CH), next_map),
            _layer_spec((CONV_WIDTH, SUBLANES, CONV_CH), jj),
            _layer_spec((1, CONV_CH), jj),
            _layer_spec((1, CONV_CH), jj),
            _layer_spec((1, CONV_CH), jj),
            pl.BlockSpec((len(POOL_SIZES), TILE, TILE + 2 * HALO), lambda i: (0, 0, 0)),
            _layer_spec((len(POOL_SIZES), POOL_GROUP_W, POOL_GROUP_W), jj),
            _layer_spec((1, POOL_CH), jj),
        ],
        out_specs=pl.BlockSpec((TILE, D_MODEL), tile_map),
        out_shape=jax.ShapeDtypeStruct((T_ALL, D_MODEL), BF16),
        scratch_shapes=[pltpu.VMEM((SUBLANES, TILE + 2 * HALO, CONV_CH), F32),
                        pltpu.VMEM((TILE + 2 * HALO, POOL_CH), F32)],
        compiler_params=_cparams(("arbitrary",)),
        name="conv_mix",
    )(u, u, u, z, z, z, cp["w_dw"], cp["b_dw"], cp["ln_g"], cp["ln_b"], _pool_bands(), cp["w_grp"], cp["p_scale"])


def _pool_bands():
    t = np.arange(TILE)[:, None]
    e = np.arange(TILE + 2 * HALO)[None, :] - HALO
    bands = []
    for w in POOL_SIZES:
        lo = w // 2
        hi = w - lo - 1
        bands.append((e >= t - lo) & (e <= t + hi))
    return jnp.asarray(np.stack(bands), BF16)


def _post_kernel(*refs, n_o, n_x, final):
    o_refs, x_refs = refs[:n_o], refs[n_o:n_o + n_x]
    mod_ref, g_ref, wo_ref, w1_ref, w2_ref, fg_ref = refs[n_o + n_x:n_o + n_x + 6]
    y_refs = refs[n_o + n_x + 6:]
    subs = _sub_rows()
    x1s, hs, acts, outs = [], [], [], []
    for rows in subs:
        x1 = _tok_load(x_refs, rows) + mod_ref[2:3, :] * _dot(_tok_load(o_refs, rows), wo_ref[...])
        x1s.append(x1)
        hs.append(_rms_mod(x1, g_ref[...], mod_ref[3:4, :], mod_ref[4:5, :]).astype(BF16))
    for h in hs:
        a = jnp.maximum(_dot(h, w1_ref[...]), 0.0)
        acts.append((a * a).astype(BF16))
    for rows, x1, a in zip(subs, x1s, acts):
        x2 = x1 + mod_ref[5:6, :] * _dot(a, w2_ref[...])
        if final:
            ms = jnp.mean(x2 * x2, axis=-1, keepdims=True)
            x2 = x2 * lax.rsqrt(ms + EPS) * fg_ref[...]
        outs.append((rows, x2))
    _tok_store(y_refs, outs)


def _post(o, x, mod, norm_g, w_out, w1, w2, final_g, layer, final):
    const2 = lambda i: (0, 0)
    o_ops, o_specs = _tok_specs(o, D_MODEL)
    x_ops, x_specs = _tok_specs(x, D_MODEL)
    if final:
        y = (jax.ShapeDtypeStruct((T_CTX, D_MODEL), F32), jax.ShapeDtypeStruct((T_LAT, D_MODEL), F32))
    else:
        y = jax.ShapeDtypeStruct((T_ALL, D_MODEL), F32)
    _, y_specs = _tok_specs(y, D_MODEL)
    out = pl.pallas_call(
        functools.partial(_post_kernel, n_o=len(o_ops), n_x=len(x_ops), final=final),
        grid=(N_TOK,),
        in_specs=o_specs + x_specs + [
            pl.BlockSpec((None, None, 6, D_MODEL), lambda i: (layer, _mod_row(i), 0, 0)),
            _layer_spec((1, D_MODEL), 2 * layer + 1),
            pl.BlockSpec((None, D_MODEL, D_MODEL), lambda i: (layer // 2, 0, 0), pipeline_mode=pl.Buffered(1)),
            pl.BlockSpec((None, D_MODEL, D_FF), lambda i: (layer, 0, 0), pipeline_mode=pl.Buffered(1)),
            pl.BlockSpec((None, D_FF, D_MODEL), lambda i: (layer, 0, 0), pipeline_mode=pl.Buffered(1)),
            pl.BlockSpec((1, D_MODEL), const2),
        ],
        out_specs=y_specs if final else y_specs[0],
        out_shape=y,
        compiler_params=_cparams(("arbitrary",)),
        name="post_final" if final else "post",
    )(*o_ops, *x_ops, mod, norm_g, w_out, w1, w2, final_g)
    return out


def _pair_lanes(w):
    lead = w.shape[:-1]
    w = w.reshape(lead + (w.shape[-1] // LANES, 2, 2, HEAD_DIM // 2))
    return jnp.swapaxes(w, -3, -2).reshape(lead + (-1,))


def _mla_lanes(nope, rope, xp=jnp):
    shape = (nope if nope is not None else rope).shape[:-1]
    nope = xp.zeros(shape + (QK_NOPE,), xp.float32) if nope is None else nope
    rope = xp.zeros(shape + (QK_ROPE,), xp.float32) if rope is None else rope
    h = QK_ROPE // 2
    return xp.concatenate([rope[..., :h], nope[..., :64 - h], rope[..., h:], nope[..., 64 - h:],
                           xp.zeros(shape + (LANES - QK_NOPE - QK_ROPE,), xp.float32)], axis=-1)


def _rope_tables():
    f32 = np.float32
    n = DEC_SEQ
    rows = n // GRID_W
    row = np.repeat(np.arange(rows), GRID_W).astype(f32)
    col = np.tile(np.arange(GRID_W), rows).astype(f32)

    def angles(dim):
        quarter = dim // 4
        inv_freq = (f32(ROPE_BASE) ** (-np.arange(quarter, dtype=f32) / f32(quarter))).astype(f32)
        return np.concatenate([row[:, None] * inv_freq, col[:, None] * inv_freq], axis=-1).astype(f32)

    ang_a = angles(HEAD_DIM)
    cos_a, sin_a = np.cos(ang_a), np.sin(ang_a)
    c_a = np.concatenate([cos_a] * 4, axis=-1)
    s_a = np.concatenate([-sin_a, -sin_a, sin_a, sin_a], axis=-1)
    ang_b = angles(QK_ROPE)
    cos_b, sin_b = np.cos(ang_b), np.sin(ang_b)
    c_b = _mla_lanes(np.ones((n, QK_NOPE), f32), np.concatenate([cos_b, cos_b], axis=-1), np)
    s_b = _mla_lanes(None, np.concatenate([-sin_b, sin_b], axis=-1), np)
    lat = np.stack([c_a, s_a, c_b, s_b])
    ident = np.stack([np.ones((TOK_IN, LANES), f32), np.zeros((TOK_IN, LANES), f32)] * 2)
    return jnp.asarray(np.concatenate([ident, lat], axis=1).astype(f32))


def _prep_even(attn_w_in, mla_q_norm, mla_kv_norm, mla_w_qb, mla_w_kvb, attn_w_out):
    w = attn_w_in
    o = np.cumsum((0, 512, 128, 128, Q_LORA, KV_LORA, QK_ROPE))
    zeros = lambda n: jnp.zeros((N_EVEN, D_MODEL, n), F32)
    w_in = jnp.concatenate([
        _pair_lanes(w[..., o[0]:o[1]]), _pair_lanes(w[..., o[1]:o[2]]), w[..., o[2]:o[3]],
        w[..., o[3]:o[4]], zeros(Q_LORA_PAD - Q_LORA),
        w[..., o[4]:o[5]],
        _mla_lanes(None, w[..., o[5]:o[6]]),
    ], axis=-1).astype(BF16)
    q_norm = jnp.pad(mla_q_norm, ((0, 0), (0, Q_LORA_PAD - Q_LORA))).reshape(N_EVEN, 1, Q_LORA_PAD)
    wqb = mla_w_qb.reshape(N_EVEN, Q_LORA, B_HEADS, QK_NOPE + QK_ROPE)
    wqb = jnp.pad(_mla_lanes(wqb[..., :QK_NOPE], wqb[..., QK_NOPE:]),
                  ((0, 0), (0, Q_LORA_PAD - Q_LORA), (0, 0), (0, 0)))
    wkvb = mla_w_kvb.reshape(N_EVEN, KV_LORA, B_HEADS, QK_NOPE + V_DIM)
    w_kk = _mla_lanes(wkvb[..., :QK_NOPE], None)
    w_kv = wkvb[..., QK_NOPE:].reshape(N_EVEN, KV_LORA, B_HEADS * V_DIM)
    return {
        "w_in": w_in,
        "q_norm": q_norm,
        "kv_norm": mla_kv_norm.reshape(N_EVEN, 1, KV_LORA),
        "w_qb": wqb.reshape(N_EVEN, Q_LORA_PAD, B_HEADS * LANES).astype(BF16),
        "w_kk": w_kk.reshape(N_EVEN, KV_LORA, B_HEADS * LANES).astype(BF16),
        "w_kv": jnp.swapaxes(w_kv, 1, 2).astype(BF16),
        "w_out": attn_w_out.astype(BF16),
    }


def kernel(x_prompt, x_sample, cache_win_k, cache_win_v, cache_mla_ckv, cache_mla_krope, c, c_ctx, w_mod, b_mod,
           norm_g, attn_w_in, attn_sink, mla_q_norm, mla_kv_norm, mla_w_qb, mla_w_kvb, attn_w_out, conv_w_in,
           conv_dw, conv_dw_b, conv_ln_g, conv_ln_b, pool_w, pool_scale, conv_w_out, mlp_w1, mlp_w2, final_g):
    x = (x_prompt.reshape(T_CTX, D_MODEL), x_sample.reshape(T_LAT, D_MODEL))

    cond = jnp.concatenate([c_ctx[None, :], c, jnp.zeros((N_COND - 1 - DEC_BATCH, D_MODEL), F32)], axis=0)
    mod = _mod_table(cond, w_mod, b_mod).reshape(DEPTH, N_COND, 6, D_MODEL)

    rope = _rope_tables()
    even = _prep_even(attn_w_in, mla_q_norm, mla_kv_norm, mla_w_qb, mla_w_kvb, attn_w_out)
    kdc, vdc, kmc, vmc = _ctx_kv(
        _pair_lanes(cache_win_k.reshape(DEC_BATCH, N_EVEN, PAST_LEN, LANES)),
        cache_win_v.reshape(DEC_BATCH, N_EVEN, PAST_LEN, LANES),
        cache_mla_ckv, _mla_lanes(None, cache_mla_krope), even["w_kk"], even["w_kv"])
    odd = {
        "w_dw": jnp.broadcast_to(conv_dw[:, :, None, :], (N_ODD, CONV_WIDTH, SUBLANES, CONV_CH)),
        "b_dw": conv_dw_b.reshape(N_ODD, 1, CONV_CH),
        "ln_g": conv_ln_g.reshape(N_ODD, 1, CONV_CH),
        "ln_b": conv_ln_b.reshape(N_ODD, 1, CONV_CH),
        "w_grp": pool_w.astype(BF16),
        "p_scale": pool_scale.reshape(N_ODD, 1, POOL_CH),
    }
    conv_w_in_b, conv_w_out_b = conv_w_in.astype(BF16), conv_w_out.astype(BF16)
    w1_all, w2_all = mlp_w1.astype(BF16), mlp_w2.astype(BF16)
    gains = norm_g.reshape(2 * DEPTH, 1, D_MODEL)
    final_g2 = final_g.reshape(1, D_MODEL)
    sink2 = attn_sink * LOG2E

    caches = []
    for l in range(DEPTH):
        if l % 2 == 0:
            i = l // 2
            qa, kd, vd, qm, km, vm, ka32, va32, ckv32, kr32 = _attn_in(x, mod, gains, even, rope, l)
            caches.append((ka32, va32, ckv32, kr32))
            o = _attention(sink2[i], qa, kd, vd, qm, km, vm, kdc, vdc, kmc, vmc, i)
            w_out = even["w_out"]
        else:
            u, z = _conv_in(x, mod, gains, conv_w_in_b, l)
            o = _conv_mix(u, z, odd, l // 2)
            w_out = conv_w_out_b
        x = _post(o, x, mod, gains, w_out, w1_all, w2_all, final_g2, l, final=(l == DEPTH - 1))

    ka, va, ckv, kr = (jnp.stack(t, axis=1) for t in zip(*caches))
    new_k = jnp.swapaxes(ka.reshape(BATCH, SEQ, N_EVEN, 2, A_KV_HEADS, HEAD_DIM // 2), 3, 4)
    new_k = jnp.swapaxes(new_k.reshape(BATCH, SEQ, N_EVEN, A_KV_HEADS, HEAD_DIM), 1, 2)
    new_v = jnp.swapaxes(va.reshape(BATCH, SEQ, N_EVEN, A_KV_HEADS, HEAD_DIM), 1, 2)
    new_ckv = jnp.swapaxes(ckv.reshape(BATCH, SEQ, N_EVEN, KV_LORA), 1, 2)
    new_kr = jnp.concatenate([kr[..., :QK_ROPE // 2], kr[..., 64:64 + QK_ROPE // 2]], axis=-1)
    new_kr = jnp.swapaxes(new_kr.reshape(BATCH, SEQ, N_EVEN, QK_ROPE), 1, 2)
    y_prompt = x[0].reshape(BATCH, SEQ, D_MODEL)
    y_sample = x[1].reshape(DEC_BATCH, DEC_SEQ, D_MODEL)
    return (y_prompt, y_sample, new_k, new_v, new_ckv, new_kr)
```

```python
import functools
import math

import numpy as np
import jax
import jax.numpy as jnp
from jax import lax
from jax.experimental import pallas as pl
from jax.experimental.pallas import tpu as pltpu

F32 = jnp.float32
BF16 = jnp.bfloat16

D_MODEL = 1024
BATCH = 16
SEQ = 256
DEPTH = 4
DEC_BATCH = 8
DEC_SEQ = 2048
PAST_LEN = 256
GRID_W = 64
N_EVEN = (DEPTH + 1) // 2
N_ODD = DEPTH // 2
A_HEADS = 8
A_KV_HEADS = 2
A_GROUP = A_HEADS // A_KV_HEADS
HEAD_DIM = 64
WINDOW = 128
B_HEADS = 8
Q_LORA = 192
KV_LORA = 128
QK_NOPE = 64
QK_ROPE = 32
V_DIM = 64
MLA_SCALE = (QK_NOPE + QK_ROPE) ** -0.5
CONV_CH = D_MODEL // 2
CONV_WIDTH = 31
POOL_CH = D_MODEL // 2
POOL_SIZES = (2, 4, 8, 16)
POOL_GROUP_W = POOL_CH // len(POOL_SIZES)
D_FF = 4 * D_MODEL
ROPE_BASE = 10000.0
EPS = 1e-6
NEG_INF = -1e30
LOG2E = math.log2(math.e)

LANES = 128
SUBLANES = 8
VMEM_LIMIT_BYTES = 56 * 1024 * 1024

TILE = 256
T_CTX = BATCH * SEQ
T_LAT = DEC_BATCH * DEC_SEQ
T_ALL = T_CTX + T_LAT
N_CTX_TILES = T_CTX // TILE
N_TILES = T_ALL // TILE
LAT_TILES_PER_SEQ = DEC_SEQ // TILE
TOK = 512
N_CTX_TOK = T_CTX // TOK
N_TOK = T_ALL // TOK
TOK_IN = 1024
N_COND = 16
BQ = 256
VT_ROWS = 80
HALO = 16
Q_LORA_PAD = 256
ATTN_IN_COLS = 512 + 128 + 128 + Q_LORA_PAD + 128 + 128


def _cparams(sem):
    return pltpu.CompilerParams(dimension_semantics=sem, vmem_limit_bytes=VMEM_LIMIT_BYTES)


def _mod_row(i, tok=TOK):
    return jnp.where(i < T_CTX // tok, 0, 1 + (i - T_CTX // tok) // (DEC_SEQ // tok))


def _pos_block(i, tok):
    return jnp.where(i < T_CTX // tok, 0, 1 + (i - T_CTX // tok) % (DEC_SEQ // tok))


def _tok_specs(a, width, tok=TOK):
    if isinstance(a, tuple):
        return list(a), [pl.BlockSpec((tok, width), lambda i: (jnp.minimum(i, T_CTX // tok - 1), 0)),
                         pl.BlockSpec((tok, width), lambda i: (jnp.maximum(i - T_CTX // tok, 0), 0))]
    return [a], [pl.BlockSpec((tok, width), lambda i: (i, 0))]


def _tok_load(refs, rows, tok=TOK):
    if len(refs) == 1:
        return refs[0][rows, :]
    return jnp.where(pl.program_id(0) < T_CTX // tok, refs[0][rows, :], refs[1][rows, :])


def _tok_store(refs, parts):
    def put(ref):
        for rows, val in parts:
            ref[rows, :] = val

    if len(refs) == 1:
        put(refs[0])
        return
    i = pl.program_id(0)

    @pl.when(i < N_CTX_TOK)
    def _():
        put(refs[0])

    @pl.when(i >= N_CTX_TOK)
    def _():
        put(refs[1])


def _layer_spec(shape, idx):
    zeros = (0,) * len(shape)
    return pl.BlockSpec((None,) + tuple(shape), lambda *_: (idx,) + zeros)


def _sub_rows(tok=TOK):
    return [slice(s * TILE, (s + 1) * TILE) for s in range(tok // TILE)]


def _dot(a, b):
    return jnp.dot(a, b, preferred_element_type=F32)


def _dot_nt(a, b):
    return lax.dot_general(a, b, (((1,), (1,)), ((), ())), preferred_element_type=F32)


def _rms_mod(x, g, shift, scale):
    ms = jnp.mean(x * x, axis=-1, keepdims=True)
    return x * lax.rsqrt(ms + EPS) * (g * (1.0 + scale)) + shift


def _rope(x, c, s):
    return x * c + pltpu.roll(x, 64, 1) * s


def _lane_lt64(shape):
    return lax.broadcasted_iota(jnp.int32, shape, len(shape) - 1) < 64


def _lane_even32(shape):
    return (lax.broadcasted_iota(jnp.int32, shape, len(shape) - 1) & 63) < 32


def _store_kv_dup(kd_ref, vd_ref, rows, k, v):
    even32 = _lane_even32(k.shape)
    lt64 = _lane_lt64(v.shape)
    kd_ref[rows, 0:LANES] = jnp.where(even32, k, pltpu.roll(k, 32, 1)).astype(BF16)
    kd_ref[rows, LANES:2 * LANES] = jnp.where(even32, pltpu.roll(k, LANES - 32, 1), k).astype(BF16)
    vd_ref[rows, 0:LANES] = jnp.where(lt64, v, 1.0).astype(BF16)
    vd_ref[rows, LANES:2 * LANES] = jnp.where(lt64, pltpu.roll(v, 64, 1), 1.0).astype(BF16)


def _store_vt(vt_ref, cols, ckvn, wkvt_ref):
    vt = _dot(wkvt_ref[...], ckvn.T.astype(BF16))
    ones = jnp.ones((VT_ROWS - V_DIM, ckvn.shape[0]), BF16)
    for hh in range(B_HEADS):
        vt_ref[hh * VT_ROWS:hh * VT_ROWS + V_DIM, cols] = vt[hh * V_DIM:(hh + 1) * V_DIM].astype(BF16)
        vt_ref[hh * VT_ROWS + V_DIM:(hh + 1) * VT_ROWS, cols] = ones


def _mod_kernel(cond_ref, w_ref, b_ref, o_ref):
    c = cond_ref[...]
    s = c * jax.nn.sigmoid(c)
    o_ref[...] = _dot(s.astype(BF16), w_ref[...].astype(BF16)) + b_ref[...]


def _mod_table(cond, w_mod, b_mod):
    nb = 6 * D_MODEL // 1024
    return pl.pallas_call(
        _mod_kernel,
        grid=(DEPTH, nb),
        in_specs=[
            pl.BlockSpec((N_COND, D_MODEL), lambda l, n: (0, 0)),
            pl.BlockSpec((None, D_MODEL, 1024), lambda l, n: (l, 0, n)),
            pl.BlockSpec((None, 1, 1024), lambda l, n: (l, 0, n)),
        ],
        out_specs=pl.BlockSpec((None, N_COND, 1024), lambda l, n: (l, 0, n)),
        out_shape=jax.ShapeDtypeStruct((DEPTH, N_COND, 6 * D_MODEL), F32),
        compiler_params=_cparams(("arbitrary", "arbitrary")),
        name="mod_table",
    )(cond, w_mod, b_mod.reshape(DEPTH, 1, 6 * D_MODEL))


def _attn_in_kernel(*refs, n_x):
    x_refs = refs[:n_x]
    (mod_ref, g_ref, w_ref, qn_ref, kvn_ref, wqb_ref, wkk_ref, wkv_ref, rope_ref,
     qa_ref, kd_ref, vd_ref, qm_ref, km_ref, vm_ref, ka32_ref, va32_ref, ckv32_ref, kr32_ref) = refs[n_x:]
    col = lambda n: slice(n * LANES, (n + 1) * LANES)

    def project(rows):
        h = _rms_mod(_tok_load(x_refs, rows, TOK_IN), g_ref[...], mod_ref[0:1, :], mod_ref[1:2, :])
        return _dot(h.astype(BF16), w_ref[...])

    def derive(rows, y):
        ca, sa = rope_ref[0, rows, :], rope_ref[1, rows, :]
        cb, sb = rope_ref[2, rows, :], rope_ref[3, rows, :]

        for c in range(4):
            q = _rope(y[:, col(c)], ca, sa)
            qa_ref[rows, col(c)] = (q * (HEAD_DIM ** -0.5 * LOG2E)).astype(BF16)

        ka = _rope(y[:, 512:640], ca, sa)
        va = y[:, 640:768]
        _store_kv_dup(kd_ref, vd_ref, rows, ka, va)

        cq = y[:, 768:768 + Q_LORA_PAD]
        cqn = cq * lax.rsqrt(jnp.sum(cq * cq, axis=-1, keepdims=True) * (1.0 / Q_LORA) + EPS) * qn_ref[...]
        qm = _dot(cqn.astype(BF16), wqb_ref[...])
        for hh in range(B_HEADS):
            q = _rope(qm[:, col(hh)], cb, sb)
            qm_ref[rows, col(hh)] = (q * (MLA_SCALE * LOG2E)).astype(BF16)

        ckv = y[:, 1024:1152]
        ckvn = ckv * lax.rsqrt(jnp.mean(ckv * ckv, axis=-1, keepdims=True) + EPS) * kvn_ref[...]
        kr = y[:, 1152:1280]
        ckvn_b = ckvn.astype(BF16)
        kn = _dot(ckvn_b, wkk_ref[...])
        krr = _rope(kr, cb, sb)
        for hh in range(B_HEADS):
            km_ref[rows, col(hh)] = (kn[:, col(hh)] + krr).astype(BF16)
        _store_vt(vm_ref, rows, ckvn, wkv_ref)
        return ka, va, ckvn, kr

    subs = _sub_rows(TOK_IN)
    cache = []
    y = project(subs[0])
    for s, rows in enumerate(subs):
        y_next = project(subs[s + 1]) if s + 1 < len(subs) else None
        cache.append(derive(rows, y))
        y = y_next

    @pl.when(pl.program_id(0) < T_CTX // TOK_IN)
    def _():
        for rows, (ka, va, ckvn, kr) in zip(subs, cache):
            ka32_ref[rows, :] = ka
            va32_ref[rows, :] = va
            ckv32_ref[rows, :] = ckvn
            kr32_ref[rows, :] = kr


def _attn_in(x, mod, norm_g, wp, rope, layer):
    tile_map = lambda i: (i, 0)
    li = layer // 2
    ctx_map = lambda i: (jnp.minimum(i, T_CTX // TOK_IN - 1), 0)
    bf = lambda w: jax.ShapeDtypeStruct((T_ALL, w), BF16)
    c32 = jax.ShapeDtypeStruct((T_CTX, LANES), F32)
    x_ops, x_specs = _tok_specs(x, D_MODEL, TOK_IN)
    return pl.pallas_call(
        functools.partial(_attn_in_kernel, n_x=len(x_ops)),
        grid=(T_ALL // TOK_IN,),
        in_specs=x_specs + [
            pl.BlockSpec((None, None, 6, D_MODEL), lambda i: (layer, _mod_row(i, TOK_IN), 0, 0)),
            _layer_spec((1, D_MODEL), 2 * layer),
            _layer_spec((D_MODEL, ATTN_IN_COLS), li),
            _layer_spec((1, Q_LORA_PAD), li),
            _layer_spec((1, KV_LORA), li),
            _layer_spec((Q_LORA_PAD, B_HEADS * LANES), li),
            _layer_spec((KV_LORA, B_HEADS * LANES), li),
            _layer_spec((B_HEADS * V_DIM, KV_LORA), li),
            pl.BlockSpec((4, TOK_IN, LANES), lambda i: (0, _pos_block(i, TOK_IN), 0)),
        ],
        out_specs=[
            pl.BlockSpec((TOK_IN, 512), tile_map),
            pl.BlockSpec((TOK_IN, 256), tile_map),
            pl.BlockSpec((TOK_IN, 256), tile_map),
            pl.BlockSpec((TOK_IN, 1024), tile_map),
            pl.BlockSpec((TOK_IN, 1024), tile_map),
            pl.BlockSpec((B_HEADS * VT_ROWS, TOK_IN), lambda i: (0, i)),
            pl.BlockSpec((TOK_IN, LANES), ctx_map),
            pl.BlockSpec((TOK_IN, LANES), ctx_map),
            pl.BlockSpec((TOK_IN, LANES), ctx_map),
            pl.BlockSpec((TOK_IN, LANES), ctx_map),
        ],
        out_shape=[bf(512), bf(256), bf(256), bf(1024), bf(1024),
                   jax.ShapeDtypeStruct((B_HEADS * VT_ROWS, T_ALL), BF16), c32, c32, c32, c32],
        compiler_params=_cparams(("arbitrary",)),
        name="attn_in",
    )(*x_ops, mod, norm_g, wp["w_in"], wp["q_norm"], wp["kv_norm"], wp["w_qb"], wp["w_kk"], wp["w_kv"], rope)


def _ctx_kv_kernel(ck_ref, cv_ref, cckv_ref, ckr_ref, wkk_ref, wkv_ref, kd_ref, vd_ref, km_ref, vm_ref):
    _store_kv_dup(kd_ref, vd_ref, slice(None), ck_ref[...], cv_ref[...])
    cb = cckv_ref[...].astype(BF16)
    kn = _dot(cb, wkk_ref[...])
    kr = ckr_ref[...]
    for hh in range(B_HEADS):
        km_ref[:, hh * LANES:(hh + 1) * LANES] = (kn[:, hh * LANES:(hh + 1) * LANES] + kr).astype(BF16)
    _store_vt(vm_ref, slice(None), cckv_ref[...], wkv_ref)


def _ctx_kv(cache_k, cache_v, cache_ckv, cache_kr128, w_kk, w_kv):
    cache_map = lambda i, b: (b, i, 0, 0)
    w_map = lambda i, b: (i, 0, 0)
    out_map = lambda i, b: (i, b, 0, 0)
    o = lambda w: jax.ShapeDtypeStruct((N_EVEN, DEC_BATCH, PAST_LEN, w), BF16)
    return pl.pallas_call(
        _ctx_kv_kernel,
        grid=(N_EVEN, DEC_BATCH),
        in_specs=[
            pl.BlockSpec((None, None, PAST_LEN, LANES), cache_map),
            pl.BlockSpec((None, None, PAST_LEN, LANES), cache_map),
            pl.BlockSpec((None, None, PAST_LEN, KV_LORA), cache_map),
            pl.BlockSpec((None, None, PAST_LEN, LANES), cache_map),
            pl.BlockSpec((None, KV_LORA, B_HEADS * LANES), w_map),
            pl.BlockSpec((None, B_HEADS * V_DIM, KV_LORA), w_map),
        ],
        out_specs=[
            pl.BlockSpec((None, None, PAST_LEN, 256), out_map),
            pl.BlockSpec((None, None, PAST_LEN, 256), out_map),
            pl.BlockSpec((None, None, PAST_LEN, 1024), out_map),
            pl.BlockSpec((None, None, B_HEADS * VT_ROWS, PAST_LEN), out_map),
        ],
        out_shape=[o(256), o(256), o(1024),
                   jax.ShapeDtypeStruct((N_EVEN, DEC_BATCH, B_HEADS * VT_ROWS, PAST_LEN), BF16)],
        compiler_params=_cparams(("arbitrary", "arbitrary")),
        name="ctx_kv",
    )(cache_k, cache_v, cache_ckv, cache_kr128, w_kk, w_kv)


def _attn_heads(qa_ref, qm_ref, a_segs, m_segs, sink_ref, o_ref, rows, depth):
    lt64 = _lane_lt64((rows, LANES))
    lane = lax.broadcasted_iota(jnp.int32, (1, LANES), 1)
    keep = (((lane & 63) < 32).astype(BF16), ((lane & 63) >= 32).astype(BF16))
    col = lambda n: slice(n * LANES, (n + 1) * LANES)
    swap = lambda r: pltpu.roll(r, 64, 1)

    jobs = []
    for kh in range(A_KV_HEADS):
        heads = [dict(out=(kh * A_GROUP + g) // 2, half=g % 2, sink=kh * A_GROUP + g) for g in range(A_GROUP)]
        jobs.append(dict(
            heads=heads,
            q=lambda heads=heads: jnp.concatenate(
                [qa_ref[:, col(h["out"])] * keep[h["half"]] for h in heads], axis=0),
            ks=[lambda kd=kd, rs=rs, kh=kh: kd[rs, col(kh)] for kd, _, rs, _ in a_segs],
            vs=[lambda vd=vd, rs=rs, kh=kh: vd[rs, col(kh)] for _, vd, rs, _ in a_segs],
            valids=[valid for _, _, _, valid in a_segs]))
    for hh in range(B_HEADS):
        jobs.append(dict(
            keys_on_rows=True,
            heads=[dict(out=4 + hh // 2, half=hh % 2)],
            q=lambda hh=hh: qm_ref[:, col(hh)],
            ks=[lambda km=km, hh=hh: km[:, col(hh)] for km, _ in m_segs],
            vs=[lambda vmt=vmt, hh=hh: vmt[hh * VT_ROWS:(hh + 1) * VT_ROWS, :] for _, vmt in m_segs]))

    def scores(job):
        if job.get("keys_on_rows"):
            h = job["heads"][0]
            q = job["q"]()
            h["ss"] = [_dot_nt(k(), q) for k in job["ks"]]
            m = h["ss"][0].max(axis=0, keepdims=True)
            for s in h["ss"][1:]:
                m = jnp.maximum(m, s.max(axis=0, keepdims=True))
            h["m"] = m
            return
        full = [_dot_nt(job["q"](), k()) for k in job["ks"]]
        for b, h in enumerate(job["heads"]):
            blk = slice(b * rows, (b + 1) * rows)
            ss = [s[blk] if valid is None else jnp.where(valid, s[blk], NEG_INF)
                  for s, valid in zip(full, job["valids"])]
            m = ss[0].max(axis=-1, keepdims=True)
            for s in ss[1:]:
                m = jnp.maximum(m, s.max(axis=-1, keepdims=True))
            sink = sink_ref[h["sink"]]
            m = jnp.maximum(m, sink)
            h["e"] = jnp.exp2(sink - m)
            h["ss"], h["m"] = ss, m

    def values(job):
        if job.get("keys_on_rows"):
            h = job["heads"][0]
            r = None
            for s, vt in zip(h.pop("ss"), job["vs"]):
                rs_ = _dot(vt(), jnp.exp2(s - h["m"]).astype(BF16))
                r = rs_ if r is None else r + rs_
            del h["m"]
            h["r"] = r
            return
        r = None
        for i, v in enumerate(job["vs"]):
            p = [jnp.exp2(h["ss"][i] - h["m"]).astype(BF16) for h in job["heads"]]
            rs_ = _dot(jnp.concatenate(p, axis=0), v())
            r = rs_ if r is None else r + rs_
        for b, h in enumerate(job["heads"]):
            del h["ss"], h["m"]
            h["r"] = r[b * rows:(b + 1) * rows]

    def finish(job, done):
        if job.get("keys_on_rows"):
            h = job["heads"][0]
            r = h.pop("r")
            o_t = r[0:V_DIM] / r[V_DIM:V_DIM + 1]
            other = done.pop(h["out"], None)
            if other is None:
                done[h["out"]] = o_t
            else:
                lo, hi = (other, o_t) if h["half"] == 1 else (o_t, other)
                o_ref[:, col(h["out"])] = jnp.concatenate([lo, hi], axis=0).T.astype(BF16)
            return
        for h in job["heads"]:
            r = h.pop("r")
            if h["half"] == 1:
                o = swap(r) / (r + h["e"])
            else:
                o = r / (swap(r) + h["e"])
            other = done.pop(h["out"], None)
            if other is None:
                done[h["out"]] = o
            else:
                lo, hi = (other, o) if h["half"] == 1 else (o, other)
                o_ref[:, col(h["out"])] = jnp.where(lt64, lo, hi).astype(BF16)

    done = {}
    for t in range(len(jobs) + depth):
        if t < len(jobs):
            scores(jobs[t])
        if t >= depth:
            values(jobs[t - depth])
            finish(jobs[t - depth], done)


def _attn_lat_kernel(sink_ref, qa_ref, qm_ref, kd_ref, vd_ref, km_ref, vm_ref,
                     kdc_ref, vdc_ref, kmc_ref, vmc_ref, o_ref):
    j = pl.program_id(1)
    q0 = j * BQ
    nloc = BQ + 2 * WINDOW
    start = pl.multiple_of(jnp.clip(q0 - WINDOW, 0, DEC_SEQ - nloc), WINDOW)
    qpos = q0 + lax.broadcasted_iota(jnp.int32, (BQ, nloc), 0)
    kpos = start + lax.broadcasted_iota(jnp.int32, (BQ, nloc), 1)
    valid = jnp.abs(qpos - kpos) <= WINDOW
    everything = slice(None)
    a_segs = [(kd_ref, vd_ref, pl.ds(start, nloc), valid), (kdc_ref, vdc_ref, everything, None)]
    m_segs = [(km_ref, vm_ref), (kmc_ref, vmc_ref)]
    _attn_heads(qa_ref, qm_ref, a_segs, m_segs, sink_ref, o_ref, BQ, depth=4)


def _attn_ctx_kernel(sink_ref, qa_ref, qm_ref, kd_ref, vd_ref, km_ref, vm_ref, o_ref):
    a_segs = [(kd_ref, vd_ref, slice(None), None)]
    m_segs = [(km_ref, vm_ref)]
    _attn_heads(qa_ref, qm_ref, a_segs, m_segs, sink_ref, o_ref, SEQ, depth=3)


def _attention(sink2, qa, kd, vd, qm, km, vm, kdc, vdc, kmc, vmc, layer_i):
    smem = pl.BlockSpec(memory_space=pltpu.SMEM)
    nq = DEC_SEQ // BQ
    q_off = T_CTX // BQ
    s_off = T_CTX // DEC_SEQ
    q_map = lambda b, j: (q_off + b * nq + j, 0)
    kv_map = lambda b, j: (s_off + b, 0)
    c_map = lambda b, j: (layer_i, b, 0, 0)
    o_lat = pl.pallas_call(
        _attn_lat_kernel,
        grid=(DEC_BATCH, nq),
        in_specs=[
            smem,
            pl.BlockSpec((BQ, 512), q_map),
            pl.BlockSpec((BQ, 1024), q_map),
            pl.BlockSpec((DEC_SEQ, 256), kv_map),
            pl.BlockSpec((DEC_SEQ, 256), kv_map),
            pl.BlockSpec((DEC_SEQ, 1024), kv_map),
            pl.BlockSpec((B_HEADS * VT_ROWS, DEC_SEQ), lambda b, j: (0, s_off + b)),
            pl.BlockSpec((None, None, PAST_LEN, 256), c_map),
            pl.BlockSpec((None, None, PAST_LEN, 256), c_map),
            pl.BlockSpec((None, None, PAST_LEN, 1024), c_map),
            pl.BlockSpec((None, None, B_HEADS * VT_ROWS, PAST_LEN), c_map),
        ],
        out_specs=pl.BlockSpec((BQ, 1024), lambda b, j: (b * nq + j, 0)),
        out_shape=jax.ShapeDtypeStruct((T_LAT, 1024), BF16),
        compiler_params=_cparams(("arbitrary", "arbitrary")),
        name="attn_latent",
    )(sink2, qa, qm, kd, vd, km, vm, kdc, vdc, kmc, vmc)
    b_map = lambda b: (b, 0)
    o_ctx = pl.pallas_call(
        _attn_ctx_kernel,
        grid=(BATCH,),
        in_specs=[
            smem,
            pl.BlockSpec((SEQ, 512), b_map),
            pl.BlockSpec((SEQ, 1024), b_map),
            pl.BlockSpec((SEQ, 256), b_map),
            pl.BlockSpec((SEQ, 256), b_map),
            pl.BlockSpec((SEQ, 1024), b_map),
            pl.BlockSpec((B_HEADS * VT_ROWS, SEQ), lambda b: (0, b)),
        ],
        out_specs=pl.BlockSpec((SEQ, 1024), b_map),
        out_shape=jax.ShapeDtypeStruct((T_CTX, 1024), BF16),
        compiler_params=_cparams(("arbitrary",)),
        name="attn_context",
    )(sink2, qa, qm, kd, vd, km, vm)
    return (o_ctx, o_lat)


def _conv_in_kernel(x_ref, mod_ref, g_ref, w_ref, u_ref, z_ref):
    for rows in _sub_rows(TOK_IN):
        h = _rms_mod(x_ref[rows, :], g_ref[...], mod_ref[0:1, :], mod_ref[1:2, :])
        y = _dot(h.astype(BF16), w_ref[...])
        a = y[:, 0:CONV_CH]
        gate = y[:, CONV_CH:2 * CONV_CH]
        u_ref[rows, :] = a * jax.nn.sigmoid(gate)
        z_ref[rows, :] = y[:, 2 * CONV_CH:]


def _conv_in(x, mod, norm_g, w_in, layer):
    tile_map = lambda i: (i, 0)
    return pl.pallas_call(
        _conv_in_kernel,
        grid=(T_ALL // TOK_IN,),
        in_specs=[
            pl.BlockSpec((TOK_IN, D_MODEL), tile_map),
            pl.BlockSpec((None, None, 6, D_MODEL), lambda i: (layer, _mod_row(i, TOK_IN), 0, 0)),
            _layer_spec((1, D_MODEL), 2 * layer),
            _layer_spec((D_MODEL, 3 * CONV_CH), layer // 2),
        ],
        out_specs=[pl.BlockSpec((TOK_IN, CONV_CH), tile_map), pl.BlockSpec((TOK_IN, POOL_CH), tile_map)],
        out_shape=[jax.ShapeDtypeStruct((T_ALL, CONV_CH), F32), jax.ShapeDtypeStruct((T_ALL, POOL_CH), F32)],
        compiler_params=_cparams(("arbitrary",)),
        name="conv_in",
    )(x, mod, norm_g, w_in)


CONV_ROWS = 32


def _conv_mix_kernel(u_ref, up_ref, un_ref, z_ref, zp_ref, zn_ref, wdw_ref, bdw_ref, lng_ref, lnb_ref,
                     band_ref, wg_ref, ps_ref, o_ref, ue_ref, ze_ref):
    i = pl.program_id(0)
    j = (i - N_CTX_TILES) % LAT_TILES_PER_SEQ
    is_lat = i >= N_CTX_TILES
    has_prev = jnp.logical_and(is_lat, j > 0)
    has_next = jnp.logical_and(is_lat, j < LAT_TILES_PER_SEQ - 1)
    seq_len = jnp.where(is_lat, DEC_SEQ, SEQ)
    t0 = jnp.where(is_lat, j * TILE, 0)

    ue_ref[0, 0:HALO, :] = jnp.where(has_prev, up_ref[...], 0.0)
    ue_ref[0, HALO:HALO + TILE, :] = u_ref[...]
    ue_ref[0, HALO + TILE:, :] = jnp.where(has_next, un_ref[...], 0.0)
    ze_ref[0:HALO, :] = jnp.where(has_prev, zp_ref[...], 0.0)
    ze_ref[HALO:HALO + TILE, :] = z_ref[...]
    ze_ref[HALO + TILE:, :] = jnp.where(has_next, zn_ref[...], 0.0)

    pad = CONV_WIDTH // 2
    reach = (HALO - pad + CONV_WIDTH - 1) // SUBLANES * SUBLANES
    groups = CONV_ROWS // SUBLANES
    n_chunks = TILE // CONV_ROWS

    def shift_rows(lo, hi):
        for b in range(1, SUBLANES):
            ue_ref[b, lo:hi, :] = ue_ref[0, lo + b:hi + b, :]

    def conv_chunk(r):
        r0 = r * CONV_ROWS
        acc = jnp.zeros((groups, SUBLANES, CONV_CH), F32) + bdw_ref[...]
        for k in range(CONV_WIDTH):
            a, b = divmod(HALO + k - pad, SUBLANES)
            e0 = r0 + a * SUBLANES
            acc = acc + ue_ref[b, e0:e0 + CONV_ROWS, :].reshape(groups, SUBLANES, CONV_CH) * wdw_ref[k]
        acc = acc.reshape(CONV_ROWS, CONV_CH)
        mu = jnp.mean(acc, axis=-1, keepdims=True)
        d = acc - mu
        var = jnp.mean(d * d, axis=-1, keepdims=True)
        yn = d * lax.rsqrt(var + EPS) * lng_ref[...] + lnb_ref[...]
        o_ref[r0:r0 + CONV_ROWS, 0:CONV_CH] = (yn * jax.nn.sigmoid(yn)).astype(BF16)

    t = t0 + lax.broadcasted_iota(jnp.int32, (TILE, POOL_GROUP_W), 0)
    ze = ze_ref[...]
    z_hi = ze.astype(BF16)
    z_lo = (ze - z_hi.astype(F32)).astype(BF16)

    def pool_group(gi):
        w = POOL_SIZES[gi]
        lo = w // 2
        hi = w - lo - 1
        cols = slice(gi * POOL_GROUP_W, (gi + 1) * POOL_GROUP_W)
        tot = _dot(band_ref[gi], z_hi[:, cols]) + _dot(band_ref[gi], z_lo[:, cols])
        cnt = jnp.minimum(t + hi + 1, seq_len) - jnp.maximum(t - lo, 0)
        d = tot / cnt.astype(F32) - ze_ref[HALO:HALO + TILE, cols]
        y = _dot(d.astype(BF16), wg_ref[gi]) * ps_ref[:, cols]
        o_ref[:, CONV_CH + gi * POOL_GROUP_W:CONV_CH + (gi + 1) * POOL_GROUP_W] = y.astype(BF16)

    shift_rows(0, reach + CONV_ROWS)
    for r in range(n_chunks):
        if r + 1 < n_chunks:
            shift_rows(reach + (r + 1) * CONV_ROWS, reach + (r + 2) * CONV_ROWS)
        conv_chunk(r)
        if r % (n_chunks // len(POOL_SIZES)) == 0:
            pool_group(r // (n_chunks // len(POOL_SIZES)))


def _conv_mix(u, z, cp, jj):
    tile_map = lambda i: (i, 0)
    hb = TILE // HALO
    prev_map = lambda i: (jnp.maximum(i * hb - 1, 0), 0)
    next_map = lambda i: (jnp.minimum((i + 1) * hb, T_ALL // HALO - 1), 0)
    return pl.pallas_call(
        _conv_mix_kernel,
        grid=(N_TILES,),
        in_specs=[
            pl.BlockSpec((TILE, CONV_CH), tile_map),
            pl.BlockSpec((HALO, CONV_CH), prev_map),
            pl.BlockSpec((HALO, CONV_CH), next_map),
            pl.BlockSpec((TILE, POOL_CH), tile_map),
            pl.BlockSpec((HALO, POOL_CH), prev_map),
            pl.BlockSpec((HALO, POOL_CH), next_map),
            _layer_spec((CONV_WIDTH, SUBLANES, CONV_CH), jj),
            _layer_spec((1, CONV_CH), jj),
            _layer_spec((1, CONV_CH), jj),
            _layer_spec((1, CONV_CH), jj),
            pl.BlockSpec((len(POOL_SIZES), TILE, TILE + 2 * HALO), lambda i: (0, 0, 0)),
            _layer_spec((len(POOL_SIZES), POOL_GROUP_W, POOL_GROUP_W), jj),
            _layer_spec((1, POOL_CH), jj),
        ],
        out_specs=pl.BlockSpec((TILE, D_MODEL), tile_map),
        out_shape=jax.ShapeDtypeStruct((T_ALL, D_MODEL), BF16),
        scratch_shapes=[pltpu.VMEM((SUBLANES, TILE + 2 * HALO, CONV_CH), F32),
                        pltpu.VMEM((TILE + 2 * HALO, POOL_CH), F32)],
        compiler_params=_cparams(("arbitrary",)),
        name="conv_mix",
    )(u, u, u, z, z, z, cp["w_dw"], cp["b_dw"], cp["ln_g"], cp["ln_b"], _pool_bands(), cp["w_grp"], cp["p_scale"])


def _pool_bands():
    t = np.arange(TILE)[:, None]
    e = np.arange(TILE + 2 * HALO)[None, :] - HALO
    bands = []
    for w in POOL_SIZES:
        lo = w // 2
        hi = w - lo - 1
        bands.append((e >= t - lo) & (e <= t + hi))
    return jnp.asarray(np.stack(bands), BF16)


def _post_kernel(*refs, n_o, n_x, final):
    o_refs, x_refs = refs[:n_o], refs[n_o:n_o + n_x]
    mod_ref, g_ref, wo_ref, w1_ref, w2_ref, fg_ref = refs[n_o + n_x:n_o + n_x + 6]
    y_refs = refs[n_o + n_x + 6:]
    subs = _sub_rows()
    x1s, hs, acts, outs = [], [], [], []
    for rows in subs:
        x1 = _tok_load(x_refs, rows) + mod_ref[2:3, :] * _dot(_tok_load(o_refs, rows), wo_ref[...])
        x1s.append(x1)
        hs.append(_rms_mod(x1, g_ref[...], mod_ref[3:4, :], mod_ref[4:5, :]).astype(BF16))
    for h in hs:
        a = jnp.maximum(_dot(h, w1_ref[...]), 0.0)
        acts.append((a * a).astype(BF16))
    for rows, x1, a in zip(subs, x1s, acts):
        x2 = x1 + mod_ref[5:6, :] * _dot(a, w2_ref[...])
        if final:
            ms = jnp.mean(x2 * x2, axis=-1, keepdims=True)
            x2 = x2 * lax.rsqrt(ms + EPS) * fg_ref[...]
        outs.append((rows, x2))
    _tok_store(y_refs, outs)


def _post(o, x, mod, norm_g, w_out, w1, w2, final_g, layer, final):
    const2 = lambda i: (0, 0)
    o_ops, o_specs = _tok_specs(o, D_MODEL)
    x_ops, x_specs = _tok_specs(x, D_MODEL)
    if final:
        y = (jax.ShapeDtypeStruct((T_CTX, D_MODEL), F32), jax.ShapeDtypeStruct((T_LAT, D_MODEL), F32))
    else:
        y = jax.ShapeDtypeStruct((T_ALL, D_MODEL), F32)
    _, y_specs = _tok_specs(y, D_MODEL)
    out = pl.pallas_call(
        functools.partial(_post_kernel, n_o=len(o_ops), n_x=len(x_ops), final=final),
        grid=(N_TOK,),
        in_specs=o_specs + x_specs + [
            pl.BlockSpec((None, None, 6, D_MODEL), lambda i: (layer, _mod_row(i), 0, 0)),
            _layer_spec((1, D_MODEL), 2 * layer + 1),
            pl.BlockSpec((None, D_MODEL, D_MODEL), lambda i: (layer // 2, 0, 0), pipeline_mode=pl.Buffered(1)),
            pl.BlockSpec((None, D_MODEL, D_FF), lambda i: (layer, 0, 0), pipeline_mode=pl.Buffered(1)),
            pl.BlockSpec((None, D_FF, D_MODEL), lambda i: (layer, 0, 0), pipeline_mode=pl.Buffered(1)),
            pl.BlockSpec((1, D_MODEL), const2),
        ],
        out_specs=y_specs if final else y_specs[0],
        out_shape=y,
        compiler_params=_cparams(("arbitrary",)),
        name="post_final" if final else "post",
    )(*o_ops, *x_ops, mod, norm_g, w_out, w1, w2, final_g)
    return out


def _pair_lanes(w):
    lead = w.shape[:-1]
    w = w.reshape(lead + (w.shape[-1] // LANES, 2, 2, HEAD_DIM // 2))
    return jnp.swapaxes(w, -3, -2).reshape(lead + (-1,))


def _mla_lanes(nope, rope, xp=jnp):
    shape = (nope if nope is not None else rope).shape[:-1]
    nope = xp.zeros(shape + (QK_NOPE,), xp.float32) if nope is None else nope
    rope = xp.zeros(shape + (QK_ROPE,), xp.float32) if rope is None else rope
    h = QK_ROPE // 2
    return xp.concatenate([rope[..., :h], nope[..., :64 - h], rope[..., h:], nope[..., 64 - h:],
                           xp.zeros(shape + (LANES - QK_NOPE - QK_ROPE,), xp.float32)], axis=-1)


def _rope_tables():
    f32 = np.float32
    n = DEC_SEQ
    rows = n // GRID_W
    row = np.repeat(np.arange(rows), GRID_W).astype(f32)
    col = np.tile(np.arange(GRID_W), rows).astype(f32)

    def angles(dim):
        quarter = dim // 4
        inv_freq = (f32(ROPE_BASE) ** (-np.arange(quarter, dtype=f32) / f32(quarter))).astype(f32)
        return np.concatenate([row[:, None] * inv_freq, col[:, None] * inv_freq], axis=-1).astype(f32)

    ang_a = angles(HEAD_DIM)
    cos_a, sin_a = np.cos(ang_a), np.sin(ang_a)
    c_a = np.concatenate([cos_a] * 4, axis=-1)
    s_a = np.concatenate([-sin_a, -sin_a, sin_a, sin_a], axis=-1)
    ang_b = angles(QK_ROPE)
    cos_b, sin_b = np.cos(ang_b), np.sin(ang_b)
    c_b = _mla_lanes(np.ones((n, QK_NOPE), f32), np.concatenate([cos_b, cos_b], axis=-1), np)
    s_b = _mla_lanes(None, np.concatenate([-sin_b, sin_b], axis=-1), np)
    lat = np.stack([c_a, s_a, c_b, s_b])
    ident = np.stack([np.ones((TOK_IN, LANES), f32), np.zeros((TOK_IN, LANES), f32)] * 2)
    return jnp.asarray(np.concatenate([ident, lat], axis=1).astype(f32))


def _prep_even(attn_w_in, mla_q_norm, mla_kv_norm, mla_w_qb, mla_w_kvb, attn_w_out):
    w = attn_w_in
    o = np.cumsum((0, 512, 128, 128, Q_LORA, KV_LORA, QK_ROPE))
    zeros = lambda n: jnp.zeros((N_EVEN, D_MODEL, n), F32)
    w_in = jnp.concatenate([
        _pair_lanes(w[..., o[0]:o[1]]), _pair_lanes(w[..., o[1]:o[2]]), w[..., o[2]:o[3]],
        w[..., o[3]:o[4]], zeros(Q_LORA_PAD - Q_LORA),
        w[..., o[4]:o[5]],
        _mla_lanes(None, w[..., o[5]:o[6]]),
    ], axis=-1).astype(BF16)
    q_norm = jnp.pad(mla_q_norm, ((0, 0), (0, Q_LORA_PAD - Q_LORA))).reshape(N_EVEN, 1, Q_LORA_PAD)
    wqb = mla_w_qb.reshape(N_EVEN, Q_LORA, B_HEADS, QK_NOPE + QK_ROPE)
    wqb = jnp.pad(_mla_lanes(wqb[..., :QK_NOPE], wqb[..., QK_NOPE:]),
                  ((0, 0), (0, Q_LORA_PAD - Q_LORA), (0, 0), (0, 0)))
    wkvb = mla_w_kvb.reshape(N_EVEN, KV_LORA, B_HEADS, QK_NOPE + V_DIM)
    w_kk = _mla_lanes(wkvb[..., :QK_NOPE], None)
    w_kv = wkvb[..., QK_NOPE:].reshape(N_EVEN, KV_LORA, B_HEADS * V_DIM)
    return {
        "w_in": w_in,
        "q_norm": q_norm,
        "kv_norm": mla_kv_norm.reshape(N_EVEN, 1, KV_LORA),
        "w_qb": wqb.reshape(N_EVEN, Q_LORA_PAD, B_HEADS * LANES).astype(BF16),
        "w_kk": w_kk.reshape(N_EVEN, KV_LORA, B_HEADS * LANES).astype(BF16),
        "w_kv": jnp.swapaxes(w_kv, 1, 2).astype(BF16),
        "w_out": attn_w_out.astype(BF16),
    }


def kernel(x_prompt, x_sample, cache_win_k, cache_win_v, cache_mla_ckv, cache_mla_krope, c, c_ctx, w_mod, b_mod,
           norm_g, attn_w_in, attn_sink, mla_q_norm, mla_kv_norm, mla_w_qb, mla_w_kvb, attn_w_out, conv_w_in,
           conv_dw, conv_dw_b, conv_ln_g, conv_ln_b, pool_w, pool_scale, conv_w_out, mlp_w1, mlp_w2, final_g):
    x = (x_prompt.reshape(T_CTX, D_MODEL), x_sample.reshape(T_LAT, D_MODEL))

    cond = jnp.concatenate([c_ctx[None, :], c, jnp.zeros((N_COND - 1 - DEC_BATCH, D_MODEL), F32)], axis=0)
    mod = _mod_table(cond, w_mod, b_mod).reshape(DEPTH, N_COND, 6, D_MODEL)

    rope = _rope_tables()
    even = _prep_even(attn_w_in, mla_q_norm, mla_kv_norm, mla_w_qb, mla_w_kvb, attn_w_out)
    kdc, vdc, kmc, vmc = _ctx_kv(
        _pair_lanes(cache_win_k.reshape(DEC_BATCH, N_EVEN, PAST_LEN, LANES)),
        cache_win_v.reshape(DEC_BATCH, N_EVEN, PAST_LEN, LANES),
        cache_mla_ckv, _mla_lanes(None, cache_mla_krope), even["w_kk"], even["w_kv"])
    odd = {
        "w_dw": jnp.broadcast_to(conv_dw[:, :, None, :], (N_ODD, CONV_WIDTH, SUBLANES, CONV_CH)),
        "b_dw": conv_dw_b.reshape(N_ODD, 1, CONV_CH),
        "ln_g": conv_ln_g.reshape(N_ODD, 1, CONV_CH),
        "ln_b": conv_ln_b.reshape(N_ODD, 1, CONV_CH),
        "w_grp": pool_w.astype(BF16),
        "p_scale": pool_scale.reshape(N_ODD, 1, POOL_CH),
    }
    conv_w_in_b, conv_w_out_b = conv_w_in.astype(BF16), conv_w_out.astype(BF16)
    w1_all, w2_all = mlp_w1.astype(BF16), mlp_w2.astype(BF16)
    gains = norm_g.reshape(2 * DEPTH, 1, D_MODEL)
    final_g2 = final_g.reshape(1, D_MODEL)
    sink2 = attn_sink * LOG2E

    caches = []
    for l in range(DEPTH):
        if l % 2 == 0:
            i = l // 2
            qa, kd, vd, qm, km, vm, ka32, va32, ckv32, kr32 = _attn_in(x, mod, gains, even, rope, l)
            caches.append((ka32, va32, ckv32, kr32))
            o = _attention(sink2[i], qa, kd, vd, qm, km, vm, kdc, vdc, kmc, vmc, i)
            w_out = even["w_out"]
        else:
            u, z = _conv_in(x, mod, gains, conv_w_in_b, l)
            o = _conv_mix(u, z, odd, l // 2)
            w_out = conv_w_out_b
        x = _post(o, x, mod, gains, w_out, w1_all, w2_all, final_g2, l, final=(l == DEPTH - 1))

    ka, va, ckv, kr = (jnp.stack(t, axis=1) for t in zip(*caches))
    new_k = jnp.swapaxes(ka.reshape(BATCH, SEQ, N_EVEN, 2, A_KV_HEADS, HEAD_DIM // 2), 3, 4)
    new_k = jnp.swapaxes(new_k.reshape(BATCH, SEQ, N_EVEN, A_KV_HEADS, HEAD_DIM), 1, 2)
    new_v = jnp.swapaxes(va.reshape(BATCH, SEQ, N_EVEN, A_KV_HEADS, HEAD_DIM), 1, 2)
    new_ckv = jnp.swapaxes(ckv.reshape(BATCH, SEQ, N_EVEN, KV_LORA), 1, 2)
    new_kr = jnp.concatenate([kr[..., :QK_ROPE // 2], kr[..., 64:64 + QK_ROPE // 2]], axis=-1)
    new_kr = jnp.swapaxes(new_kr.reshape(BATCH, SEQ, N_EVEN, QK_ROPE), 1, 2)
    y_prompt = x[0].reshape(BATCH, SEQ, D_MODEL)
    y_sample = x[1].reshape(DEC_BATCH, DEC_SEQ, D_MODEL)
    return (y_prompt, y_sample, new_k, new_v, new_ckv, new_kr)
```

```python
import functools
import math

import numpy as np
import jax
import jax.numpy as jnp
from jax import lax
from jax.experimental import pallas as pl
from jax.experimental.pallas import tpu as pltpu

F32 = jnp.float32
BF16 = jnp.bfloat16

D_MODEL = 1024
BATCH = 16
SEQ = 256
DEPTH = 4
DEC_BATCH = 8
DEC_SEQ = 2048
PAST_LEN = 256
GRID_W = 64
N_EVEN = (DEPTH + 1) // 2
N_ODD = DEPTH // 2
A_HEADS = 8
A_KV_HEADS = 2
A_GROUP = A_HEADS // A_KV_HEADS
HEAD_DIM = 64
WINDOW = 128
B_HEADS = 8
Q_LORA = 192
KV_LORA = 128
QK_NOPE = 64
QK_ROPE = 32
V_DIM = 64
MLA_SCALE = (QK_NOPE + QK_ROPE) ** -0.5
CONV_CH = D_MODEL // 2
CONV_WIDTH = 31
POOL_CH = D_MODEL // 2
POOL_SIZES = (2, 4, 8, 16)
POOL_GROUP_W = POOL_CH // len(POOL_SIZES)
D_FF = 4 * D_MODEL
ROPE_BASE = 10000.0
EPS = 1e-6
NEG_INF = -1e30
LOG2E = math.log2(math.e)

LANES = 128
SUBLANES = 8
VMEM_LIMIT_BYTES = 56 * 1024 * 1024

TILE = 256
T_CTX = BATCH * SEQ
T_LAT = DEC_BATCH * DEC_SEQ
T_ALL = T_CTX + T_LAT
N_CTX_TILES = T_CTX // TILE
N_TILES = T_ALL // TILE
LAT_TILES_PER_SEQ = DEC_SEQ // TILE
TOK = 512
N_CTX_TOK = T_CTX // TOK
N_TOK = T_ALL // TOK
TOK_IN = 1024
N_COND = 16
BQ = 256
VT_ROWS = 80
HALO = 16
Q_LORA_PAD = 256
ATTN_IN_COLS = 512 + 128 + 128 + Q_LORA_PAD + 128 + 128


def _cparams(sem, vmem_mib=None):
    limit = VMEM_LIMIT_BYTES if vmem_mib is None else vmem_mib * 1024 * 1024
    return pltpu.CompilerParams(dimension_semantics=sem, vmem_limit_bytes=limit)


def _mod_row(i, tok=TOK):
    return jnp.where(i < T_CTX // tok, 0, 1 + (i - T_CTX // tok) // (DEC_SEQ // tok))


def _pos_block(i, tok):
    return jnp.where(i < T_CTX // tok, 0, 1 + (i - T_CTX // tok) % (DEC_SEQ // tok))


def _tok_specs(a, width, tok=TOK):
    if isinstance(a, tuple):
        return list(a), [pl.BlockSpec((tok, width), lambda i: (jnp.minimum(i, T_CTX // tok - 1), 0)),
                         pl.BlockSpec((tok, width), lambda i: (jnp.maximum(i - T_CTX // tok, 0), 0))]
    return [a], [pl.BlockSpec((tok, width), lambda i: (i, 0))]


def _tok_load(refs, rows, tok=TOK):
    if len(refs) == 1:
        return refs[0][rows, :]
    return jnp.where(pl.program_id(0) < T_CTX // tok, refs[0][rows, :], refs[1][rows, :])


def _tok_store(refs, parts):
    def put(ref):
        for rows, val in parts:
            ref[rows, :] = val

    if len(refs) == 1:
        put(refs[0])
        return
    i = pl.program_id(0)

    @pl.when(i < N_CTX_TOK)
    def _():
        put(refs[0])

    @pl.when(i >= N_CTX_TOK)
    def _():
        put(refs[1])


def _layer_spec(shape, idx):
    zeros = (0,) * len(shape)
    return pl.BlockSpec((None,) + tuple(shape), lambda *_: (idx,) + zeros)


def _sub_rows(tok=TOK):
    return [slice(s * TILE, (s + 1) * TILE) for s in range(tok // TILE)]


def _dot(a, b):
    return jnp.dot(a, b, preferred_element_type=F32)


def _dot_nt(a, b):
    return lax.dot_general(a, b, (((1,), (1,)), ((), ())), preferred_element_type=F32)


def _rms_mod(x, g, shift, scale):
    ms = jnp.mean(x * x, axis=-1, keepdims=True)
    return x * lax.rsqrt(ms + EPS) * (g * (1.0 + scale)) + shift


def _rope(x, c, s):
    return x * c + pltpu.roll(x, 64, 1) * s


def _lane_lt64(shape):
    return lax.broadcasted_iota(jnp.int32, shape, len(shape) - 1) < 64


def _lane_even32(shape):
    return (lax.broadcasted_iota(jnp.int32, shape, len(shape) - 1) & 63) < 32


def _store_kv_dup(kd_ref, vd_ref, rows, k, v):
    even32 = _lane_even32(k.shape)
    lt64 = _lane_lt64(v.shape)
    kd_ref[rows, 0:LANES] = jnp.where(even32, k, pltpu.roll(k, 32, 1)).astype(BF16)
    kd_ref[rows, LANES:2 * LANES] = jnp.where(even32, pltpu.roll(k, LANES - 32, 1), k).astype(BF16)
    vd_ref[rows, 0:LANES] = jnp.where(lt64, v, 1.0).astype(BF16)
    vd_ref[rows, LANES:2 * LANES] = jnp.where(lt64, pltpu.roll(v, 64, 1), 1.0).astype(BF16)


def _store_vt(vt_ref, cols, ckvn, wkvt_ref):
    vt = _dot(wkvt_ref[...], ckvn.T.astype(BF16))
    ones = jnp.ones((VT_ROWS - V_DIM, ckvn.shape[0]), BF16)
    for hh in range(B_HEADS):
        vt_ref[hh * VT_ROWS:hh * VT_ROWS + V_DIM, cols] = vt[hh * V_DIM:(hh + 1) * V_DIM].astype(BF16)
        vt_ref[hh * VT_ROWS + V_DIM:(hh + 1) * VT_ROWS, cols] = ones


def _mod_kernel(cond_ref, w_ref, b_ref, o_ref):
    c = cond_ref[...]
    s = c * jax.nn.sigmoid(c)
    o_ref[...] = _dot(s.astype(BF16), w_ref[...].astype(BF16)) + b_ref[...]


def _mod_table(cond, w_mod, b_mod):
    nb = 6 * D_MODEL // 1024
    return pl.pallas_call(
        _mod_kernel,
        grid=(DEPTH, nb),
        in_specs=[
            pl.BlockSpec((N_COND, D_MODEL), lambda l, n: (0, 0)),
            pl.BlockSpec((None, D_MODEL, 1024), lambda l, n: (l, 0, n)),
            pl.BlockSpec((None, 1, 1024), lambda l, n: (l, 0, n)),
        ],
        out_specs=pl.BlockSpec((None, N_COND, 1024), lambda l, n: (l, 0, n)),
        out_shape=jax.ShapeDtypeStruct((DEPTH, N_COND, 6 * D_MODEL), F32),
        compiler_params=_cparams(("arbitrary", "arbitrary"), 24),
        name="mod_table",
    )(cond, w_mod, b_mod.reshape(DEPTH, 1, 6 * D_MODEL))


def _attn_in_kernel(*refs, n_x):
    x_refs = refs[:n_x]
    (mod_ref, g_ref, w_ref, qn_ref, kvn_ref, wqb_ref, wkk_ref, wkv_ref, rope_ref,
     qa_ref, kd_ref, vd_ref, qm_ref, km_ref, vm_ref, ka32_ref, va32_ref, ckv32_ref, kr32_ref) = refs[n_x:]
    col = lambda n: slice(n * LANES, (n + 1) * LANES)

    def project(rows):
        h = _rms_mod(_tok_load(x_refs, rows, TOK_IN), g_ref[...], mod_ref[0:1, :], mod_ref[1:2, :])
        return _dot(h.astype(BF16), w_ref[...])

    def derive(rows, y):
        ca, sa = rope_ref[0, rows, :], rope_ref[1, rows, :]
        cb, sb = rope_ref[2, rows, :], rope_ref[3, rows, :]

        for c in range(4):
            q = _rope(y[:, col(c)], ca, sa)
            qa_ref[rows, col(c)] = (q * (HEAD_DIM ** -0.5 * LOG2E)).astype(BF16)

        ka = _rope(y[:, 512:640], ca, sa)
        va = y[:, 640:768]
        _store_kv_dup(kd_ref, vd_ref, rows, ka, va)

        cq = y[:, 768:768 + Q_LORA_PAD]
        cqn = cq * lax.rsqrt(jnp.sum(cq * cq, axis=-1, keepdims=True) * (1.0 / Q_LORA) + EPS) * qn_ref[...]
        qm = _dot(cqn.astype(BF16), wqb_ref[...])
        for hh in range(B_HEADS):
            q = _rope(qm[:, col(hh)], cb, sb)
            qm_ref[rows, col(hh)] = (q * (MLA_SCALE * LOG2E)).astype(BF16)

        ckv = y[:, 1024:1152]
        ckvn = ckv * lax.rsqrt(jnp.mean(ckv * ckv, axis=-1, keepdims=True) + EPS) * kvn_ref[...]
        kr = y[:, 1152:1280]
        ckvn_b = ckvn.astype(BF16)
        kn = _dot(ckvn_b, wkk_ref[...])
        krr = _rope(kr, cb, sb)
        for hh in range(B_HEADS):
            km_ref[rows, col(hh)] = (kn[:, col(hh)] + krr).astype(BF16)
        _store_vt(vm_ref, rows, ckvn, wkv_ref)
        return ka, va, ckvn, kr

    subs = _sub_rows(TOK_IN)
    cache = []
    y = project(subs[0])
    for s, rows in enumerate(subs):
        y_next = project(subs[s + 1]) if s + 1 < len(subs) else None
        cache.append(derive(rows, y))
        y = y_next

    @pl.when(pl.program_id(0) < T_CTX // TOK_IN)
    def _():
        for rows, (ka, va, ckvn, kr) in zip(subs, cache):
            ka32_ref[rows, :] = ka
            va32_ref[rows, :] = va
            ckv32_ref[rows, :] = ckvn
            kr32_ref[rows, :] = kr


def _attn_in(x, mod, norm_g, wp, rope, layer):
    tile_map = lambda i: (i, 0)
    li = layer // 2
    ctx_map = lambda i: (jnp.minimum(i, T_CTX // TOK_IN - 1), 0)
    bf = lambda w: jax.ShapeDtypeStruct((T_ALL, w), BF16)
    c32 = jax.ShapeDtypeStruct((T_CTX, LANES), F32)
    x_ops, x_specs = _tok_specs(x, D_MODEL, TOK_IN)
    return pl.pallas_call(
        functools.partial(_attn_in_kernel, n_x=len(x_ops)),
        grid=(T_ALL // TOK_IN,),
        in_specs=x_specs + [
            pl.BlockSpec((None, None, 6, D_MODEL), lambda i: (layer, _mod_row(i, TOK_IN), 0, 0)),
            _layer_spec((1, D_MODEL), 2 * layer),
            _layer_spec((D_MODEL, ATTN_IN_COLS), li),
            _layer_spec((1, Q_LORA_PAD), li),
            _layer_spec((1, KV_LORA), li),
            _layer_spec((Q_LORA_PAD, B_HEADS * LANES), li),
            _layer_spec((KV_LORA, B_HEADS * LANES), li),
            _layer_spec((B_HEADS * V_DIM, KV_LORA), li),
            pl.BlockSpec((4, TOK_IN, LANES), lambda i: (0, _pos_block(i, TOK_IN), 0)),
        ],
        out_specs=[
            pl.BlockSpec((TOK_IN, 512), tile_map),
            pl.BlockSpec((TOK_IN, 256), tile_map),
            pl.BlockSpec((TOK_IN, 256), tile_map),
            pl.BlockSpec((TOK_IN, 1024), tile_map),
            pl.BlockSpec((TOK_IN, 1024), tile_map),
            pl.BlockSpec((B_HEADS * VT_ROWS, TOK_IN), lambda i: (0, i)),
            pl.BlockSpec((TOK_IN, LANES), ctx_map),
            pl.BlockSpec((TOK_IN, LANES), ctx_map),
            pl.BlockSpec((TOK_IN, LANES), ctx_map),
            pl.BlockSpec((TOK_IN, LANES), ctx_map),
        ],
        out_shape=[bf(512), bf(256), bf(256), bf(1024), bf(1024),
                   jax.ShapeDtypeStruct((B_HEADS * VT_ROWS, T_ALL), BF16), c32, c32, c32, c32],
        compiler_params=_cparams(("arbitrary",)),
        name="attn_in",
    )(*x_ops, mod, norm_g, wp["w_in"], wp["q_norm"], wp["kv_norm"], wp["w_qb"], wp["w_kk"], wp["w_kv"], rope)


def _ctx_kv_kernel(ck_ref, cv_ref, cckv_ref, ckr_ref, wkk_ref, wkv_ref, kd_ref, vd_ref, km_ref, vm_ref):
    _store_kv_dup(kd_ref, vd_ref, slice(None), ck_ref[...], cv_ref[...])
    cb = cckv_ref[...].astype(BF16)
    kn = _dot(cb, wkk_ref[...])
    kr = ckr_ref[...]
    for hh in range(B_HEADS):
        km_ref[:, hh * LANES:(hh + 1) * LANES] = (kn[:, hh * LANES:(hh + 1) * LANES] + kr).astype(BF16)
    _store_vt(vm_ref, slice(None), cckv_ref[...], wkv_ref)


def _ctx_kv(cache_k, cache_v, cache_ckv, cache_kr128, w_kk, w_kv):
    cache_map = lambda i, b: (b, i, 0, 0)
    w_map = lambda i, b: (i, 0, 0)
    out_map = lambda i, b: (i, b, 0, 0)
    o = lambda w: jax.ShapeDtypeStruct((N_EVEN, DEC_BATCH, PAST_LEN, w), BF16)
    return pl.pallas_call(
        _ctx_kv_kernel,
        grid=(N_EVEN, DEC_BATCH),
        in_specs=[
            pl.BlockSpec((None, None, PAST_LEN, LANES), cache_map),
            pl.BlockSpec((None, None, PAST_LEN, LANES), cache_map),
            pl.BlockSpec((None, None, PAST_LEN, KV_LORA), cache_map),
            pl.BlockSpec((None, None, PAST_LEN, LANES), cache_map),
            pl.BlockSpec((None, KV_LORA, B_HEADS * LANES), w_map),
            pl.BlockSpec((None, B_HEADS * V_DIM, KV_LORA), w_map),
        ],
        out_specs=[
            pl.BlockSpec((None, None, PAST_LEN, 256), out_map),
            pl.BlockSpec((None, None, PAST_LEN, 256), out_map),
            pl.BlockSpec((None, None, PAST_LEN, 1024), out_map),
            pl.BlockSpec((None, None, B_HEADS * VT_ROWS, PAST_LEN), out_map),
        ],
        out_shape=[o(256), o(256), o(1024),
                   jax.ShapeDtypeStruct((N_EVEN, DEC_BATCH, B_HEADS * VT_ROWS, PAST_LEN), BF16)],
        compiler_params=_cparams(("arbitrary", "arbitrary"), 16),
        name="ctx_kv",
    )(cache_k, cache_v, cache_ckv, cache_kr128, w_kk, w_kv)


def _attn_heads(qa_ref, qm_ref, a_segs, m_segs, sink_ref, o_ref, rows, depth):
    lt64 = _lane_lt64((rows, LANES))
    lane = lax.broadcasted_iota(jnp.int32, (1, LANES), 1)
    keep = (((lane & 63) < 32).astype(BF16), ((lane & 63) >= 32).astype(BF16))
    col = lambda n: slice(n * LANES, (n + 1) * LANES)
    swap = lambda r: pltpu.roll(r, 64, 1)

    jobs = []
    for kh in range(A_KV_HEADS):
        heads = [dict(out=(kh * A_GROUP + g) // 2, half=g % 2, sink=kh * A_GROUP + g) for g in range(A_GROUP)]
        jobs.append(dict(
            heads=heads,
            q=lambda heads=heads: jnp.concatenate(
                [qa_ref[:, col(h["out"])] * keep[h["half"]] for h in heads], axis=0),
            ks=[lambda kd=kd, rs=rs, kh=kh: kd[rs, col(kh)] for kd, _, rs, _ in a_segs],
            vs=[lambda vd=vd, rs=rs, kh=kh: vd[rs, col(kh)] for _, vd, rs, _ in a_segs],
            valids=[valid for _, _, _, valid in a_segs]))
    for hh in range(B_HEADS):
        jobs.append(dict(
            keys_on_rows=True,
            heads=[dict(out=4 + hh // 2, half=hh % 2)],
            q=lambda hh=hh: qm_ref[:, col(hh)],
            ks=[lambda km=km, hh=hh: km[:, col(hh)] for km, _ in m_segs],
            vs=[lambda vmt=vmt, hh=hh: vmt[hh * VT_ROWS:(hh + 1) * VT_ROWS, :] for _, vmt in m_segs]))

    def scores(job):
        if job.get("keys_on_rows"):
            h = job["heads"][0]
            q = job["q"]()
            h["ss"] = [_dot_nt(k(), q) for k in job["ks"]]
            m = h["ss"][0].max(axis=0, keepdims=True)
            for s in h["ss"][1:]:
                m = jnp.maximum(m, s.max(axis=0, keepdims=True))
            h["m"] = m
            return
        full = [_dot_nt(job["q"](), k()) for k in job["ks"]]
        for b, h in enumerate(job["heads"]):
            blk = slice(b * rows, (b + 1) * rows)
            ss = [s[blk] if valid is None else jnp.where(valid, s[blk], NEG_INF)
                  for s, valid in zip(full, job["valids"])]
            m = ss[0].max(axis=-1, keepdims=True)
            for s in ss[1:]:
                m = jnp.maximum(m, s.max(axis=-1, keepdims=True))
            sink = sink_ref[h["sink"]]
            m = jnp.maximum(m, sink)
            h["e"] = jnp.exp2(sink - m)
            h["ss"], h["m"] = ss, m

    def values(job):
        if job.get("keys_on_rows"):
            h = job["heads"][0]
            r = None
            for s, vt in zip(h.pop("ss"), job["vs"]):
                rs_ = _dot(vt(), jnp.exp2(s - h["m"]).astype(BF16))
                r = rs_ if r is None else r + rs_
            del h["m"]
            h["r"] = r
            return
        r = None
        for i, v in enumerate(job["vs"]):
            p = [jnp.exp2(h["ss"][i] - h["m"]).astype(BF16) for h in job["heads"]]
            rs_ = _dot(jnp.concatenate(p, axis=0), v())
            r = rs_ if r is None else r + rs_
        for b, h in enumerate(job["heads"]):
            del h["ss"], h["m"]
            h["r"] = r[b * rows:(b + 1) * rows]

    def finish(job, done):
        if job.get("keys_on_rows"):
            h = job["heads"][0]
            r = h.pop("r")
            o_t = r[0:V_DIM] / r[V_DIM:V_DIM + 1]
            other = done.pop(h["out"], None)
            if other is None:
                done[h["out"]] = o_t
            else:
                lo, hi = (other, o_t) if h["half"] == 1 else (o_t, other)
                o_ref[:, col(h["out"])] = jnp.concatenate([lo, hi], axis=0).T.astype(BF16)
            return
        for h in job["heads"]:
            r = h.pop("r")
            if h["half"] == 1:
                o = swap(r) / (r + h["e"])
            else:
                o = r / (swap(r) + h["e"])
            other = done.pop(h["out"], None)
            if other is None:
                done[h["out"]] = o
            else:
                lo, hi = (other, o) if h["half"] == 1 else (o, other)
                o_ref[:, col(h["out"])] = jnp.where(lt64, lo, hi).astype(BF16)

    done = {}
    for t in range(len(jobs) + depth):
        if t < len(jobs):
            scores(jobs[t])
        if t >= depth:
            values(jobs[t - depth])
            finish(jobs[t - depth], done)


def _attn_lat_kernel(sink_ref, qa_ref, qm_ref, kd_ref, vd_ref, km_ref, vm_ref,
                     kdc_ref, vdc_ref, kmc_ref, vmc_ref, o_ref):
    j = pl.program_id(1)
    q0 = j * BQ
    nloc = BQ + 2 * WINDOW
    start = pl.multiple_of(jnp.clip(q0 - WINDOW, 0, DEC_SEQ - nloc), WINDOW)
    qpos = q0 + lax.broadcasted_iota(jnp.int32, (BQ, nloc), 0)
    kpos = start + lax.broadcasted_iota(jnp.int32, (BQ, nloc), 1)
    valid = jnp.abs(qpos - kpos) <= WINDOW
    everything = slice(None)
    a_segs = [(kd_ref, vd_ref, pl.ds(start, nloc), valid), (kdc_ref, vdc_ref, everything, None)]
    m_segs = [(km_ref, vm_ref), (kmc_ref, vmc_ref)]
    _attn_heads(qa_ref, qm_ref, a_segs, m_segs, sink_ref, o_ref, BQ, depth=3)


def _attn_ctx_kernel(sink_ref, qa_ref, qm_ref, kd_ref, vd_ref, km_ref, vm_ref, o_ref):
    a_segs = [(kd_ref, vd_ref, slice(None), None)]
    m_segs = [(km_ref, vm_ref)]
    _attn_heads(qa_ref, qm_ref, a_segs, m_segs, sink_ref, o_ref, SEQ, depth=3)


def _attention(sink2, qa, kd, vd, qm, km, vm, kdc, vdc, kmc, vmc, layer_i):
    smem = pl.BlockSpec(memory_space=pltpu.SMEM)
    nq = DEC_SEQ // BQ
    q_off = T_CTX // BQ
    s_off = T_CTX // DEC_SEQ
    q_map = lambda b, j: (q_off + b * nq + j, 0)
    kv_map = lambda b, j: (s_off + b, 0)
    c_map = lambda b, j: (layer_i, b, 0, 0)
    o_lat = pl.pallas_call(
        _attn_lat_kernel,
        grid=(DEC_BATCH, nq),
        in_specs=[
            smem,
            pl.BlockSpec((BQ, 512), q_map),
            pl.BlockSpec((BQ, 1024), q_map),
            pl.BlockSpec((DEC_SEQ, 256), kv_map),
            pl.BlockSpec((DEC_SEQ, 256), kv_map),
            pl.BlockSpec((DEC_SEQ, 1024), kv_map),
            pl.BlockSpec((B_HEADS * VT_ROWS, DEC_SEQ), lambda b, j: (0, s_off + b)),
            pl.BlockSpec((None, None, PAST_LEN, 256), c_map),
            pl.BlockSpec((None, None, PAST_LEN, 256), c_map),
            pl.BlockSpec((None, None, PAST_LEN, 1024), c_map),
            pl.BlockSpec((None, None, B_HEADS * VT_ROWS, PAST_LEN), c_map),
        ],
        out_specs=pl.BlockSpec((BQ, 1024), lambda b, j: (b * nq + j, 0)),
        out_shape=jax.ShapeDtypeStruct((T_LAT, 1024), BF16),
        compiler_params=_cparams(("arbitrary", "arbitrary"), 40),
        name="attn_latent",
    )(sink2, qa, qm, kd, vd, km, vm, kdc, vdc, kmc, vmc)
    b_map = lambda b: (b, 0)
    o_ctx = pl.pallas_call(
        _attn_ctx_kernel,
        grid=(BATCH,),
        in_specs=[
            smem,
            pl.BlockSpec((SEQ, 512), b_map),
            pl.BlockSpec((SEQ, 1024), b_map),
            pl.BlockSpec((SEQ, 256), b_map),
            pl.BlockSpec((SEQ, 256), b_map),
            pl.BlockSpec((SEQ, 1024), b_map),
            pl.BlockSpec((B_HEADS * VT_ROWS, SEQ), lambda b: (0, b)),
        ],
        out_specs=pl.BlockSpec((SEQ, 1024), b_map),
        out_shape=jax.ShapeDtypeStruct((T_CTX, 1024), BF16),
        compiler_params=_cparams(("arbitrary",), 24),
        name="attn_context",
    )(sink2, qa, qm, kd, vd, km, vm)
    return (o_ctx, o_lat)


def _conv_in_kernel(x_ref, mod_ref, g_ref, w_ref, u_ref, z_ref):
    for rows in _sub_rows(TOK_IN):
        h = _rms_mod(x_ref[rows, :], g_ref[...], mod_ref[0:1, :], mod_ref[1:2, :])
        y = _dot(h.astype(BF16), w_ref[...])
        a = y[:, 0:CONV_CH]
        gate = y[:, CONV_CH:2 * CONV_CH]
        u_ref[rows, :] = a * jax.nn.sigmoid(gate)
        z_ref[rows, :] = y[:, 2 * CONV_CH:]


def _conv_in(x, mod, norm_g, w_in, layer):
    tile_map = lambda i: (i, 0)
    return pl.pallas_call(
        _conv_in_kernel,
        grid=(T_ALL // TOK_IN,),
        in_specs=[
            pl.BlockSpec((TOK_IN, D_MODEL), tile_map),
            pl.BlockSpec((None, None, 6, D_MODEL), lambda i: (layer, _mod_row(i, TOK_IN), 0, 0)),
            _layer_spec((1, D_MODEL), 2 * layer),
            _layer_spec((D_MODEL, 3 * CONV_CH), layer // 2),
        ],
        out_specs=[pl.BlockSpec((TOK_IN, CONV_CH), tile_map), pl.BlockSpec((TOK_IN, POOL_CH), tile_map)],
        out_shape=[jax.ShapeDtypeStruct((T_ALL, CONV_CH), F32), jax.ShapeDtypeStruct((T_ALL, POOL_CH), F32)],
        compiler_params=_cparams(("arbitrary",), 40),
        name="conv_in",
    )(x, mod, norm_g, w_in)


CONV_ROWS = 32


def _conv_mix_kernel(u_ref, up_ref, un_ref, z_ref, zp_ref, zn_ref, wdw_ref, bdw_ref, lng_ref, lnb_ref,
                     band_ref, wg_ref, ps_ref, o_ref, ue_ref, ze_ref):
    i = pl.program_id(0)
    j = (i - N_CTX_TILES) % LAT_TILES_PER_SEQ
    is_lat = i >= N_CTX_TILES
    has_prev = jnp.logical_and(is_lat, j > 0)
    has_next = jnp.logical_and(is_lat, j < LAT_TILES_PER_SEQ - 1)
    seq_len = jnp.where(is_lat, DEC_SEQ, SEQ)
    t0 = jnp.where(is_lat, j * TILE, 0)

    ue_ref[0, 0:HALO, :] = jnp.where(has_prev, up_ref[...], 0.0)
    ue_ref[0, HALO:HALO + TILE, :] = u_ref[...]
    ue_ref[0, HALO + TILE:, :] = jnp.where(has_next, un_ref[...], 0.0)
    ze_ref[0:HALO, :] = jnp.where(has_prev, zp_ref[...], 0.0)
    ze_ref[HALO:HALO + TILE, :] = z_ref[...]
    ze_ref[HALO + TILE:, :] = jnp.where(has_next, zn_ref[...], 0.0)

    pad = CONV_WIDTH // 2
    reach = (HALO - pad + CONV_WIDTH - 1) // SUBLANES * SUBLANES
    groups = CONV_ROWS // SUBLANES
    n_chunks = TILE // CONV_ROWS

    def shift_rows(lo, hi):
        for b in range(1, SUBLANES):
            ue_ref[b, lo:hi, :] = ue_ref[0, lo + b:hi + b, :]

    def conv_chunk(r):
        r0 = r * CONV_ROWS
        acc = jnp.zeros((groups, SUBLANES, CONV_CH), F32) + bdw_ref[...]
        for k in range(CONV_WIDTH):
            a, b = divmod(HALO + k - pad, SUBLANES)
            e0 = r0 + a * SUBLANES
            acc = acc + ue_ref[b, e0:e0 + CONV_ROWS, :].reshape(groups, SUBLANES, CONV_CH) * wdw_ref[k]
        acc = acc.reshape(CONV_ROWS, CONV_CH)
        mu = jnp.mean(acc, axis=-1, keepdims=True)
        d = acc - mu
        var = jnp.mean(d * d, axis=-1, keepdims=True)
        yn = d * lax.rsqrt(var + EPS) * lng_ref[...] + lnb_ref[...]
        o_ref[r0:r0 + CONV_ROWS, 0:CONV_CH] = (yn * jax.nn.sigmoid(yn)).astype(BF16)

    t = t0 + lax.broadcasted_iota(jnp.int32, (TILE, POOL_GROUP_W), 0)
    ze = ze_ref[...]
    z_hi = ze.astype(BF16)
    z_lo = (ze - z_hi.astype(F32)).astype(BF16)

    def pool_group(gi):
        w = POOL_SIZES[gi]
        lo = w // 2
        hi = w - lo - 1
        cols = slice(gi * POOL_GROUP_W, (gi + 1) * POOL_GROUP_W)
        tot = _dot(band_ref[gi], z_hi[:, cols]) + _dot(band_ref[gi], z_lo[:, cols])
        cnt = jnp.minimum(t + hi + 1, seq_len) - jnp.maximum(t - lo, 0)
        d = tot / cnt.astype(F32) - ze_ref[HALO:HALO + TILE, cols]
        y = _dot(d.astype(BF16), wg_ref[gi]) * ps_ref[:, cols]
        o_ref[:, CONV_CH + gi * POOL_GROUP_W:CONV_CH + (gi + 1) * POOL_GROUP_W] = y.astype(BF16)

    shift_rows(0, reach + CONV_ROWS)
    for r in range(n_chunks):
        if r + 1 < n_chunks:
            shift_rows(reach + (r + 1) * CONV_ROWS, reach + (r + 2) * CONV_ROWS)
        conv_chunk(r)
        if r % (n_chunks // len(POOL_SIZES)) == 0:
            pool_group(r // (n_chunks // len(POOL_SIZES)))


def _conv_mix(u, z, cp, jj):
    tile_map = lambda i: (i, 0)
    hb = TILE // HALO
    prev_map = lambda i: (jnp.maximum(i * hb - 1, 0), 0)
    next_map = lambda i: (jnp.minimum((i + 1) * hb, T_ALL // HALO - 1), 0)
    return pl.pallas_call(
        _conv_mix_kernel,
        grid=(N_TILES,),
        in_specs=[
            pl.BlockSpec((TILE, CONV_CH), tile_map),
            pl.BlockSpec((HALO, CONV_CH), prev_map),
            pl.BlockSpec((HALO, CONV_CH), next_map),
            pl.BlockSpec((TILE, POOL_CH), tile_map),
            pl.BlockSpec((HALO, POOL_CH), prev_map),
            pl.BlockSpec((HALO, POOL_CH), next_map),
            _layer_spec((CONV_WIDTH, SUBLANES, CONV_CH), jj),
            _layer_spec((1, CONV_CH), jj),
            _layer_spec((1, CONV_CH), jj),
            _layer_spec((1, CONV_CH), jj),
            pl.BlockSpec((len(POOL_SIZES), TILE, TILE + 2 * HALO), lambda i: (0, 0, 0)),
            _layer_spec((len(POOL_SIZES), POOL_GROUP_W, POOL_GROUP_W), jj),
            _layer_spec((1, POOL_CH), jj),
        ],
        out_specs=pl.BlockSpec((TILE, D_MODEL), tile_map),
        out_shape=jax.ShapeDtypeStruct((T_ALL, D_MODEL), BF16),
        scratch_shapes=[pltpu.VMEM((SUBLANES, TILE + 2 * HALO, CONV_CH), F32),
                        pltpu.VMEM((TILE + 2 * HALO, POOL_CH), F32)],
        compiler_params=_cparams(("arbitrary",), 32),
        name="conv_mix",
    )(u, u, u, z, z, z, cp["w_dw"], cp["b_dw"], cp["ln_g"], cp["ln_b"], _pool_bands(), cp["w_grp"], cp["p_scale"])


def _pool_bands():
    t = np.arange(TILE)[:, None]
    e = np.arange(TILE + 2 * HALO)[None, :] - HALO
    bands = []
    for w in POOL_SIZES:
        lo = w // 2
        hi = w - lo - 1
        bands.append((e >= t - lo) & (e <= t + hi))
    return jnp.asarray(np.stack(bands), BF16)


def _post_kernel(*refs, n_o, n_x, final):
    o_refs, x_refs = refs[:n_o], refs[n_o:n_o + n_x]
    mod_ref, g_ref, wo_ref, w1_ref, w2_ref, fg_ref = refs[n_o + n_x:n_o + n_x + 6]
    y_refs = refs[n_o + n_x + 6:]
    subs = _sub_rows()
    x1s, hs, acts, outs = [], [], [], []
    for rows in subs:
        x1 = _tok_load(x_refs, rows) + mod_ref[2:3, :] * _dot(_tok_load(o_refs, rows), wo_ref[...])
        x1s.append(x1)
        hs.append(_rms_mod(x1, g_ref[...], mod_ref[3:4, :], mod_ref[4:5, :]).astype(BF16))
    for h in hs:
        a = jnp.maximum(_dot(h, w1_ref[...]), 0.0)
        acts.append((a * a).astype(BF16))
    for rows, x1, a in zip(subs, x1s, acts):
        x2 = x1 + mod_ref[5:6, :] * _dot(a, w2_ref[...])
        if final:
            ms = jnp.mean(x2 * x2, axis=-1, keepdims=True)
            x2 = x2 * lax.rsqrt(ms + EPS) * fg_ref[...]
        outs.append((rows, x2))
    _tok_store(y_refs, outs)


def _post(o, x, mod, norm_g, w_out, w1, w2, final_g, layer, final):
    const2 = lambda i: (0, 0)
    o_ops, o_specs = _tok_specs(o, D_MODEL)
    x_ops, x_specs = _tok_specs(x, D_MODEL)
    if final:
        y = (jax.ShapeDtypeStruct((T_CTX, D_MODEL), F32), jax.ShapeDtypeStruct((T_LAT, D_MODEL), F32))
    else:
        y = jax.ShapeDtypeStruct((T_ALL, D_MODEL), F32)
    _, y_specs = _tok_specs(y, D_MODEL)
    out = pl.pallas_call(
        functools.partial(_post_kernel, n_o=len(o_ops), n_x=len(x_ops), final=final),
        grid=(N_TOK,),
        in_specs=o_specs + x_specs + [
            pl.BlockSpec((None, None, 6, D_MODEL), lambda i: (layer, _mod_row(i), 0, 0)),
            _layer_spec((1, D_MODEL), 2 * layer + 1),
            pl.BlockSpec((None, D_MODEL, D_MODEL), lambda i: (layer // 2, 0, 0), pipeline_mode=pl.Buffered(1)),
            pl.BlockSpec((None, D_MODEL, D_FF), lambda i: (layer, 0, 0), pipeline_mode=pl.Buffered(1)),
            pl.BlockSpec((None, D_FF, D_MODEL), lambda i: (layer, 0, 0), pipeline_mode=pl.Buffered(1)),
            pl.BlockSpec((1, D_MODEL), const2),
        ],
        out_specs=y_specs if final else y_specs[0],
        out_shape=y,
        compiler_params=_cparams(("arbitrary",)),
        name="post_final" if final else "post",
    )(*o_ops, *x_ops, mod, norm_g, w_out, w1, w2, final_g)
    return out


def _pair_lanes(w):
    lead = w.shape[:-1]
    w = w.reshape(lead + (w.shape[-1] // LANES, 2, 2, HEAD_DIM // 2))
    return jnp.swapaxes(w, -3, -2).reshape(lead + (-1,))


def _mla_lanes(nope, rope, xp=jnp):
    shape = (nope if nope is not None else rope).shape[:-1]
    nope = xp.zeros(shape + (QK_NOPE,), xp.float32) if nope is None else nope
    rope = xp.zeros(shape + (QK_ROPE,), xp.float32) if rope is None else rope
    h = QK_ROPE // 2
    return xp.concatenate([rope[..., :h], nope[..., :64 - h], rope[..., h:], nope[..., 64 - h:],
                           xp.zeros(shape + (LANES - QK_NOPE - QK_ROPE,), xp.float32)], axis=-1)


def _rope_tables():
    f32 = np.float32
    n = DEC_SEQ
    rows = n // GRID_W
    row = np.repeat(np.arange(rows), GRID_W).astype(f32)
    col = np.tile(np.arange(GRID_W), rows).astype(f32)

    def angles(dim):
        quarter = dim // 4
        inv_freq = (f32(ROPE_BASE) ** (-np.arange(quarter, dtype=f32) / f32(quarter))).astype(f32)
        return np.concatenate([row[:, None] * inv_freq, col[:, None] * inv_freq], axis=-1).astype(f32)

    ang_a = angles(HEAD_DIM)
    cos_a, sin_a = np.cos(ang_a), np.sin(ang_a)
    c_a = np.concatenate([cos_a] * 4, axis=-1)
    s_a = np.concatenate([-sin_a, -sin_a, sin_a, sin_a], axis=-1)
    ang_b = angles(QK_ROPE)
    cos_b, sin_b = np.cos(ang_b), np.sin(ang_b)
    c_b = _mla_lanes(np.ones((n, QK_NOPE), f32), np.concatenate([cos_b, cos_b], axis=-1), np)
    s_b = _mla_lanes(None, np.concatenate([-sin_b, sin_b], axis=-1), np)
    lat = np.stack([c_a, s_a, c_b, s_b])
    ident = np.stack([np.ones((TOK_IN, LANES), f32), np.zeros((TOK_IN, LANES), f32)] * 2)
    return jnp.asarray(np.concatenate([ident, lat], axis=1).astype(f32))


def _prep_even(attn_w_in, mla_q_norm, mla_kv_norm, mla_w_qb, mla_w_kvb, attn_w_out):
    w = attn_w_in
    o = np.cumsum((0, 512, 128, 128, Q_LORA, KV_LORA, QK_ROPE))
    zeros = lambda n: jnp.zeros((N_EVEN, D_MODEL, n), F32)
    w_in = jnp.concatenate([
        _pair_lanes(w[..., o[0]:o[1]]), _pair_lanes(w[..., o[1]:o[2]]), w[..., o[2]:o[3]],
        w[..., o[3]:o[4]], zeros(Q_LORA_PAD - Q_LORA),
        w[..., o[4]:o[5]],
        _mla_lanes(None, w[..., o[5]:o[6]]),
    ], axis=-1).astype(BF16)
    q_norm = jnp.pad(mla_q_norm, ((0, 0), (0, Q_LORA_PAD - Q_LORA))).reshape(N_EVEN, 1, Q_LORA_PAD)
    wqb = mla_w_qb.reshape(N_EVEN, Q_LORA, B_HEADS, QK_NOPE + QK_ROPE)
    wqb = jnp.pad(_mla_lanes(wqb[..., :QK_NOPE], wqb[..., QK_NOPE:]),
                  ((0, 0), (0, Q_LORA_PAD - Q_LORA), (0, 0), (0, 0)))
    wkvb = mla_w_kvb.reshape(N_EVEN, KV_LORA, B_HEADS, QK_NOPE + V_DIM)
    w_kk = _mla_lanes(wkvb[..., :QK_NOPE], None)
    w_kv = wkvb[..., QK_NOPE:].reshape(N_EVEN, KV_LORA, B_HEADS * V_DIM)
    return {
        "w_in": w_in,
        "q_norm": q_norm,
        "kv_norm": mla_kv_norm.reshape(N_EVEN, 1, KV_LORA),
        "w_qb": wqb.reshape(N_EVEN, Q_LORA_PAD, B_HEADS * LANES).astype(BF16),
        "w_kk": w_kk.reshape(N_EVEN, KV_LORA, B_HEADS * LANES).astype(BF16),
        "w_kv": jnp.swapaxes(w_kv, 1, 2).astype(BF16),
        "w_out": attn_w_out.astype(BF16),
    }


def kernel(x_prompt, x_sample, cache_win_k, cache_win_v, cache_mla_ckv, cache_mla_krope, c, c_ctx, w_mod, b_mod,
           norm_g, attn_w_in, attn_sink, mla_q_norm, mla_kv_norm, mla_w_qb, mla_w_kvb, attn_w_out, conv_w_in,
           conv_dw, conv_dw_b, conv_ln_g, conv_ln_b, pool_w, pool_scale, conv_w_out, mlp_w1, mlp_w2, final_g):
    x = (x_prompt.reshape(T_CTX, D_MODEL), x_sample.reshape(T_LAT, D_MODEL))

    cond = jnp.concatenate([c_ctx[None, :], c, jnp.zeros((N_COND - 1 - DEC_BATCH, D_MODEL), F32)], axis=0)
    mod = _mod_table(cond, w_mod, b_mod).reshape(DEPTH, N_COND, 6, D_MODEL)

    rope = _rope_tables()
    even = _prep_even(attn_w_in, mla_q_norm, mla_kv_norm, mla_w_qb, mla_w_kvb, attn_w_out)
    kdc, vdc, kmc, vmc = _ctx_kv(
        _pair_lanes(cache_win_k.reshape(DEC_BATCH, N_EVEN, PAST_LEN, LANES)),
        cache_win_v.reshape(DEC_BATCH, N_EVEN, PAST_LEN, LANES),
        cache_mla_ckv, _mla_lanes(None, cache_mla_krope), even["w_kk"], even["w_kv"])
    odd = {
        "w_dw": jnp.broadcast_to(conv_dw[:, :, None, :], (N_ODD, CONV_WIDTH, SUBLANES, CONV_CH)),
        "b_dw": conv_dw_b.reshape(N_ODD, 1, CONV_CH),
        "ln_g": conv_ln_g.reshape(N_ODD, 1, CONV_CH),
        "ln_b": conv_ln_b.reshape(N_ODD, 1, CONV_CH),
        "w_grp": pool_w.astype(BF16),
        "p_scale": pool_scale.reshape(N_ODD, 1, POOL_CH),
    }
    conv_w_in_b, conv_w_out_b = conv_w_in.astype(BF16), conv_w_out.astype(BF16)
    w1_all, w2_all = mlp_w1.astype(BF16), mlp_w2.astype(BF16)
    gains = norm_g.reshape(2 * DEPTH, 1, D_MODEL)
    final_g2 = final_g.reshape(1, D_MODEL)
    sink2 = attn_sink * LOG2E

    caches = []
    for l in range(DEPTH):
        if l % 2 == 0:
            i = l // 2
            qa, kd, vd, qm, km, vm, ka32, va32, ckv32, kr32 = _attn_in(x, mod, gains, even, rope, l)
            caches.append((ka32, va32, ckv32, kr32))
            o = _attention(sink2[i], qa, kd, vd, qm, km, vm, kdc, vdc, kmc, vmc, i)
            w_out = even["w_out"]
        else:
            u, z = _conv_in(x, mod, gains, conv_w_in_b, l)
            o = _conv_mix(u, z, odd, l // 2)
            w_out = conv_w_out_b
        x = _post(o, x, mod, gains, w_out, w1_all, w2_all, final_g2, l, final=(l == DEPTH - 1))

    ka, va, ckv, kr = (jnp.stack(t, axis=1) for t in zip(*caches))
    new_k = jnp.swapaxes(ka.reshape(BATCH, SEQ, N_EVEN, 2, A_KV_HEADS, HEAD_DIM // 2), 3, 4)
    new_k = jnp.swapaxes(new_k.reshape(BATCH, SEQ, N_EVEN, A_KV_HEADS, HEAD_DIM), 1, 2)
    new_v = jnp.swapaxes(va.reshape(BATCH, SEQ, N_EVEN, A_KV_HEADS, HEAD_DIM), 1, 2)
    new_ckv = jnp.swapaxes(ckv.reshape(BATCH, SEQ, N_EVEN, KV_LORA), 1, 2)
    new_kr = jnp.concatenate([kr[..., :QK_ROPE // 2], kr[..., 64:64 + QK_ROPE // 2]], axis=-1)
    new_kr = jnp.swapaxes(new_kr.reshape(BATCH, SEQ, N_EVEN, QK_ROPE), 1, 2)
    y_prompt = x[0].reshape(BATCH, SEQ, D_MODEL)
    y_sample = x[1].reshape(DEC_BATCH, DEC_SEQ, D_MODEL)
    return (y_prompt, y_sample, new_k, new_v, new_ckv, new_kr)
```
